```python
import math, functools
import jax, jax.numpy as jnp
from jax import lax
import numpy as np

D_MODEL = 1024
BATCH = 16
SEQ = 2048
DEPTH = 1
DEC_BATCH = 128
DEC_SEQ = 8
PAST_LEN = 16384
PAGE_SIZE = 128

MLA_HEADS = 8
MLA_NOPE = 64
MLA_ROPE = 32
MLA_V = 64
MLA_Q_LORA = 384
MLA_KV_LORA = 256
MLA_SCALE = (MLA_NOPE + MLA_ROPE) ** -0.5
Q_BLOCK = 128
ROPE_THETA = 10000.0
GLA_HEADS = 4
GLA_DK = 64
GLA_DV = 128
GLA_GATE_RANK = 16
GLA_GATE_TAU = 16.0
GLA_CHUNK = 64
D_MIX = MLA_HEADS * MLA_V + GLA_HEADS * GLA_DV
D_FF = 2816
N_ADA = 9
EPS = 1e-6
IN_SIZES = (MLA_Q_LORA, MLA_KV_LORA, MLA_ROPE, GLA_HEADS * GLA_DK, GLA_HEADS * GLA_DK,
            GLA_HEADS * GLA_DV, GLA_GATE_RANK, GLA_HEADS * GLA_DV)
D_IN = sum(IN_SIZES)
IN_SPLIT_POINTS = tuple(sum(IN_SIZES[:i + 1]) for i in range(len(IN_SIZES) - 1))

kernel_name = 'hymba_mla_gla_macaron_adaln_step'

F32 = jnp.float32


def rmsnorm(x, g):
    xf = x.astype(F32)
    y = xf * lax.rsqrt(jnp.mean(xf * xf, axis=-1, keepdims=True) + EPS)
    return (y * g.astype(F32)).astype(x.dtype)


def rope(x, pos):
    R = x.shape[-1]
    inv = ROPE_THETA ** (-jnp.arange(0, R, 2, dtype=F32) / R)
    ang = pos[:, None] * inv[None, :]
    shape = (pos.shape[0],) + (1,) * (x.ndim - 3) + (R // 2,)
    cos = jnp.cos(ang).reshape(shape)
    sin = jnp.sin(ang).reshape(shape)
    xf = x.astype(F32)
    x1, x2 = xf[..., :R // 2], xf[..., R // 2:]
    return jnp.concatenate([x1 * cos - x2 * sin, x1 * sin + x2 * cos], axis=-1).astype(x.dtype)


def swiglu(h, w1, w3, w2):
    return (jax.nn.silu(h @ w1) * (h @ w3)) @ w2


def modulate(h, shift, scale):
    return h * (1.0 + scale) + shift


def mla_prompt_attend(q_lat, q_rope, ckv, krope):
    B, T, H, C = q_lat.shape
    nb = T // Q_BLOCK
    kpos = jnp.arange(T)
    neg = jnp.finfo(F32).min

    def block(args):
        i, ql, qr = args
        s = (jnp.einsum('bqhc,bsc->bhqs', ql, ckv, preferred_element_type=F32)
             + jnp.einsum('bqhr,bsr->bhqs', qr, krope, preferred_element_type=F32)) * MLA_SCALE
        qpos = i * Q_BLOCK + jnp.arange(Q_BLOCK)
        s = jnp.where(kpos[None, :] <= qpos[:, None], s, neg)
        p = jax.nn.softmax(s, axis=-1)
        return jnp.einsum('bhqs,bsc->bqhc', p.astype(ckv.dtype), ckv)

    qlb = q_lat.reshape(B, nb, Q_BLOCK, H, C).transpose(1, 0, 2, 3, 4)
    qrb = q_rope.reshape(B, nb, Q_BLOCK, H, -1).transpose(1, 0, 2, 3, 4)
    o = lax.map(block, (jnp.arange(nb), qlb, qrb))
    return o.transpose(1, 0, 2, 3, 4).reshape(B, T, H, C)


def mla_sample_attend(q_lat, q_rope, ckv, krope, ckv_past, krope_past):
    T = q_lat.shape[1]
    P = ckv_past.shape[1]
    s_past = (jnp.einsum('bthc,bpc->bhtp', q_lat, ckv_past, preferred_element_type=F32)
              + jnp.einsum('bthr,bpr->bhtp', q_rope, krope_past, preferred_element_type=F32)) * MLA_SCALE
    s_new = (jnp.einsum('bthc,bsc->bhts', q_lat, ckv, preferred_element_type=F32)
             + jnp.einsum('bthr,bsr->bhts', q_rope, krope, preferred_element_type=F32)) * MLA_SCALE
    causal = jnp.tril(jnp.ones((T, T), bool))
    s_new = jnp.where(causal, s_new, jnp.finfo(F32).min)
    p = jax.nn.softmax(jnp.concatenate([s_past, s_new], axis=-1), axis=-1).astype(ckv.dtype)
    return (jnp.einsum('bhtp,bpc->bthc', p[..., :P], ckv_past)
            + jnp.einsum('bhts,bsc->bthc', p[..., P:], ckv))


def gla_chunked(q, k, v, log_a, S0):
    B, T, H, DK = q.shape
    C = GLA_CHUNK if T % GLA_CHUNK == 0 else T
    n = T // C

    def to_chunks(a):
        return a.astype(F32).reshape(B, n, C, H, -1).transpose(1, 0, 3, 2, 4)

    mask = jnp.tril(jnp.ones((C, C), bool))

    def step(S, inp):
        qc, kc, vc, gc = inp
        b = jnp.cumsum(gc, axis=2)
        qe = qc * jnp.exp(b)
        ke = kc * jnp.exp(-b)
        A = jnp.where(mask, jnp.einsum('bhtk,bhsk->bhts', qe, ke), 0.0)
        o = jnp.einsum('bhtk,bhkv->bhtv', qe, S) + jnp.einsum('bhts,bhsv->bhtv', A, vc)
        b_last = b[:, :, -1:, :]
        S = (jnp.exp(b_last[:, :, 0, :])[..., None] * S
             + jnp.einsum('bhck,bhcv->bhkv', kc * jnp.exp(b_last - b), vc))
        return S, o

    S, o = lax.scan(step, S0.astype(F32), (to_chunks(q), to_chunks(k), to_chunks(v), to_chunks(log_a)))
    o = o.transpose(1, 0, 3, 2, 4).reshape(B, T, H, -1)
    return o, S


def mixer(h, pos, attend, gla_S0, w_in, g_qa, w_qb, g_kva, w_kvb, w_gate_b, b_gate, g_gla_o, w_o):
    B, T, _ = h.shape
    q_a, kv_a, k_r, gq, gk, gv, ga, gg = jnp.split(h @ w_in, IN_SPLIT_POINTS, axis=-1)
    q = jnp.einsum('btq,qhd->bthd', rmsnorm(q_a, g_qa), w_qb)
    q_rope = rope(q[..., MLA_NOPE:], pos)
    q_lat = jnp.einsum('bthn,chn->bthc', q[..., :MLA_NOPE], w_kvb[..., :MLA_NOPE])
    ckv = rmsnorm(kv_a, g_kva)
    krope = rope(k_r, pos)
    o_lat = attend(q_lat, q_rope, ckv, krope)
    o_mla = jnp.einsum('bthc,chv->bthv', o_lat, w_kvb[..., MLA_NOPE:]).reshape(B, T, MLA_HEADS * MLA_V)
    gq = gq.reshape(B, T, GLA_HEADS, GLA_DK) * (GLA_DK ** -0.5)
    gk = gk.reshape(B, T, GLA_HEADS, GLA_DK)
    gv = gv.reshape(B, T, GLA_HEADS, GLA_DV)
    log_a = (jax.nn.log_sigmoid((ga @ w_gate_b + b_gate).astype(F32)) / GLA_GATE_TAU).reshape(B, T, GLA_HEADS, GLA_DK)
    o_gla, S = gla_chunked(gq, gk, gv, log_a, gla_S0)
    o_gla = rmsnorm(o_gla.astype(h.dtype), g_gla_o) * jax.nn.silu(gg.reshape(B, T, GLA_HEADS, GLA_DV))
    out = jnp.concatenate([o_mla, o_gla.reshape(B, T, GLA_HEADS * GLA_DV)], axis=-1) @ w_o
    return out, ckv, krope, S.astype(h.dtype)


def layer(x, c, pos, attend, gla_S0, w_ada, b_ada, norm_ffn1, ffn1_w1, ffn1_w3, ffn1_w2, norm_mix,
          w_in, g_qa, w_qb, g_kva, w_kvb, w_gate_b, b_gate, g_gla_o, w_o,
          norm_ffn2, ffn2_w1, ffn2_w3, ffn2_w2):
    m = jax.nn.silu(c) @ w_ada + b_ada
    sh1, sc1, g1, sh2, sc2, g2, sh3, sc3, g3 = jnp.split(m[:, None, :], N_ADA, axis=-1)
    x = x + 0.5 * g1 * swiglu(modulate(rmsnorm(x, norm_ffn1), sh1, sc1), ffn1_w1, ffn1_w3, ffn1_w2)
    mix, ckv, krope, S = mixer(modulate(rmsnorm(x, norm_mix), sh2, sc2), pos, attend, gla_S0,
                               w_in, g_qa, w_qb, g_kva, w_kvb, w_gate_b, b_gate, g_gla_o, w_o)
    x = x + g2 * mix
    x = x + 0.5 * g3 * swiglu(modulate(rmsnorm(x, norm_ffn2), sh3, sc3), ffn2_w1, ffn2_w3, ffn2_w2)
    return x, ckv, krope, S


def setup_inputs(seed: int = 0) -> dict:
    key = jax.random.key(seed)
    ks = iter(jax.random.split(key, 48))

    def nrm(shape, s):
        return jax.random.normal(next(ks), shape, F32) * s

    def gain(n):
        return 1.0 + nrm((DEPTH, n), 0.05)

    n_pages = PAST_LEN // PAGE_SIZE
    n_phys = (DEC_BATCH * n_pages * 5) // 4
    page_table = jax.random.permutation(next(ks), n_phys)[:DEC_BATCH * n_pages].reshape(DEC_BATCH, n_pages).astype(jnp.int32)
    return {
        'x_prompt': nrm((BATCH, SEQ, D_MODEL), 1.0),
        'x_sample': nrm((DEC_BATCH, DEC_SEQ, D_MODEL), 1.0),
        'cache_ckv': nrm((DEPTH, n_phys, PAGE_SIZE, MLA_KV_LORA), 1.0),
        'cache_krope': nrm((DEPTH, n_phys, PAGE_SIZE, MLA_ROPE), 1.0),
        'state_gla': nrm((DEPTH, DEC_BATCH, GLA_HEADS, GLA_DK, GLA_DV), 0.3),
        'page_table': page_table,
        'c_prompt': nrm((BATCH, D_MODEL), 1.0),
        'c_sample': nrm((DEC_BATCH, D_MODEL), 1.0),
        'w_ada': nrm((DEPTH, D_MODEL, N_ADA * D_MODEL), 0.5 * D_MODEL ** -0.5),
        'b_ada': nrm((DEPTH, N_ADA * D_MODEL), 0.02),
        'norm_ffn1': gain(D_MODEL),
        'ffn1_w1': nrm((DEPTH, D_MODEL, D_FF), D_MODEL ** -0.5),
        'ffn1_w3': nrm((DEPTH, D_MODEL, D_FF), D_MODEL ** -0.5),
        'ffn1_w2': nrm((DEPTH, D_FF, D_MODEL), D_FF ** -0.5),
        'norm_mix': gain(D_MODEL),
        'w_in': nrm((DEPTH, D_MODEL, D_IN), D_MODEL ** -0.5),
        'g_qa': gain(MLA_Q_LORA),
        'w_qb': nrm((DEPTH, MLA_Q_LORA, MLA_HEADS, MLA_NOPE + MLA_ROPE), MLA_Q_LORA ** -0.5),
        'g_kva': gain(MLA_KV_LORA),
        'w_kvb': nrm((DEPTH, MLA_KV_LORA, MLA_HEADS, MLA_NOPE + MLA_V), MLA_KV_LORA ** -0.5),
        'w_gate_b': nrm((DEPTH, GLA_GATE_RANK, GLA_HEADS * GLA_DK), GLA_GATE_RANK ** -0.5),
        'b_gate': nrm((DEPTH, GLA_HEADS * GLA_DK), 0.1),
        'g_gla_o': gain(GLA_DV),
        'w_o': nrm((DEPTH, D_MIX, D_MODEL), D_MIX ** -0.5),
        'norm_ffn2': gain(D_MODEL),
        'ffn2_w1': nrm((DEPTH, D_MODEL, D_FF), D_MODEL ** -0.5),
        'ffn2_w3': nrm((DEPTH, D_MODEL, D_FF), D_MODEL ** -0.5),
        'ffn2_w2': nrm((DEPTH, D_FF, D_MODEL), D_FF ** -0.5),
        'norm_final': 1.0 + nrm((D_MODEL,), 0.05),
    }


def reference(x_prompt, x_sample, cache_ckv, cache_krope, state_gla, page_table, c_prompt, c_sample,
              w_ada, b_ada, norm_ffn1, ffn1_w1, ffn1_w3, ffn1_w2, norm_mix, w_in, g_qa, w_qb, g_kva, w_kvb,
              w_gate_b, b_gate, g_gla_o, w_o, norm_ffn2, ffn2_w1, ffn2_w3, ffn2_w2, norm_final):
    n_seq_s, n_pages = page_table.shape
    past_len = n_pages * PAGE_SIZE
    pos_p = jnp.arange(x_prompt.shape[1], dtype=F32)
    pos_s = past_len + jnp.arange(x_sample.shape[1], dtype=F32)
    hp, hs = x_prompt, x_sample
    ckv_p_l, kr_p_l, gla_p_l, ckv_s_l, kr_s_l, gla_s_l = [], [], [], [], [], []
    for l in range(DEPTH):
        lw = (w_ada[l], b_ada[l], norm_ffn1[l], ffn1_w1[l], ffn1_w3[l], ffn1_w2[l], norm_mix[l],
              w_in[l], g_qa[l], w_qb[l], g_kva[l], w_kvb[l], w_gate_b[l], b_gate[l], g_gla_o[l], w_o[l],
              norm_ffn2[l], ffn2_w1[l], ffn2_w3[l], ffn2_w2[l])
        ckv_past = cache_ckv[l, page_table].reshape(n_seq_s, past_len, MLA_KV_LORA)
        krope_past = cache_krope[l, page_table].reshape(n_seq_s, past_len, MLA_ROPE)
        attend_s = functools.partial(mla_sample_attend, ckv_past=ckv_past, krope_past=krope_past)
        S0_p = jnp.zeros((hp.shape[0], GLA_HEADS, GLA_DK, GLA_DV), hp.dtype)
        hp, ckv_p, kr_p, S_p = layer(hp, c_prompt, pos_p, mla_prompt_attend, S0_p, *lw)
        hs, ckv_s, kr_s, S_s = layer(hs, c_sample, pos_s, attend_s, state_gla[l], *lw)
        ckv_p_l.append(ckv_p); kr_p_l.append(kr_p); gla_p_l.append(S_p)
        ckv_s_l.append(ckv_s); kr_s_l.append(kr_s); gla_s_l.append(S_s)
    y_prompt = rmsnorm(hp, norm_final)
    y_sample = rmsnorm(hs, norm_final)
    return (y_prompt, y_sample, jnp.stack(ckv_p_l), jnp.stack(kr_p_l), jnp.stack(gla_p_l),
            jnp.stack(ckv_s_l), jnp.stack(kr_s_l), jnp.stack(gla_s_l))
```

```python
import functools

import jax
import jax.numpy as jnp
from jax import lax
from jax.experimental import pallas as pl
from jax.experimental.pallas import tpu as pltpu

F32 = jnp.float32
BF16 = jnp.bfloat16

PAGE_SIZE = 128
MLA_HEADS = 8
MLA_NOPE = 64
MLA_ROPE = 32
MLA_V = 64
MLA_Q_LORA = 384
MLA_KV_LORA = 256
MLA_SCALE = (MLA_NOPE + MLA_ROPE) ** -0.5
ROPE_THETA = 10000.0
GLA_HEADS = 4
GLA_DK = 64
GLA_DV = 128
GLA_GATE_RANK = 16
GLA_GATE_TAU = 16.0
GLA_CHUNK = 64
N_ADA = 9
EPS = 1e-6

LANE = 128
HEAD_PAD = 128
VMEM_LIMIT_BYTES = 56 * 1024 * 1024

C_QA = 0
C_KVA = C_QA + MLA_Q_LORA
C_GQ = C_KVA + MLA_KV_LORA
C_GK = C_GQ + GLA_HEADS * GLA_DK
C_GV = C_GK + GLA_HEADS * GLA_DK
C_GG = C_GV + GLA_HEADS * GLA_DV
C_BLKA = C_GG + GLA_HEADS * GLA_DV
C_BLKB = C_BLKA + LANE
C_END = C_BLKB + LANE


def _cparams(sem):
    return pltpu.CompilerParams(dimension_semantics=sem, vmem_limit_bytes=VMEM_LIMIT_BYTES)


def _resident(shape):
    return pl.BlockSpec(shape, lambda *_: (0,) * len(shape), pipeline_mode=pl.Buffered(1))


def _rms(x, g):
    return x * lax.rsqrt(jnp.mean(x * x, axis=-1, keepdims=True) + EPS) * g


def _silu(x):
    return x * jax.nn.sigmoid(x)


def _dot(a, b):
    return jnp.dot(a, b, preferred_element_type=F32)


def _dot_nt(a, b):
    return lax.dot_general(a, b, (((1,), (1,)), ((), ())), preferred_element_type=F32)


def _dot_tn(a, b):
    return lax.dot_general(a, b, (((0,), (0,)), ((), ())), preferred_element_type=F32)


def _ada_body(c_ref, w_ref, b_ref, o_ref):
    c = _silu(c_ref[...]).astype(BF16)
    o_ref[...] = _dot(c, w_ref[...].astype(BF16)) + b_ref[...]


def _ada(c_all, w_ada, b_ada):
    n, d = c_all.shape
    n_out = w_ada.shape[1]
    tn = 1024
    return pl.pallas_call(
        _ada_body,
        grid=(n_out // tn,),
        in_specs=[pl.BlockSpec((n, d), lambda j: (0, 0)),
                  pl.BlockSpec((d, tn), lambda j: (0, j)),
                  pl.BlockSpec((1, tn), lambda j: (0, j))],
        out_specs=pl.BlockSpec((n, tn), lambda j: (0, j)),
        out_shape=jax.ShapeDtypeStruct((n, n_out), F32),
        compiler_params=_cparams(("arbitrary",)),
        name="ada",
    )(c_all, w_ada, b_ada.reshape(1, n_out))


class _Mod:
    def __init__(self, m, per_token, tokens_per_seq):
        self.per_token = per_token
        self.tokens_per_seq = tokens_per_seq
        self.arr = m if per_token else m.reshape(m.shape[0], 1, m.shape[1])

    def spec(self, tm, d, third):
        if self.per_token:
            return pl.BlockSpec((tm, 3 * d), lambda i: (i, third))
        per = self.tokens_per_seq // tm
        return pl.BlockSpec((None, 1, 3 * d), lambda i: (i // per, 0, third))


def _ffn_body(x_ref, mod_ref, g_ref, w1_ref, w3_ref, w2_ref, gf_ref, o_ref, *, d, ff_chunks, final):
    x = x_ref[...]
    shift, scale, gate = mod_ref[:, 0:d], mod_ref[:, d:2 * d], mod_ref[:, 2 * d:3 * d]
    h = (_rms(x, g_ref[...]) * (1.0 + scale) + shift).astype(BF16)
    y = None
    for c0, c1 in ff_chunks:
        a = _dot(h, w1_ref[:, c0:c1])
        b = _dot(h, w3_ref[:, c0:c1])
        u = (_silu(a) * b).astype(BF16)
        part = _dot(u, w2_ref[c0:c1, :])
        y = part if y is None else y + part
    out = x + 0.5 * gate * y
    if final:
        out = _rms(out, gf_ref[...])
    o_ref[...] = out


def _ffn(x, mod, third, g, w1, w3, w2, g_final, final, tm):
    n, d = x.shape
    ff = w1.shape[1]
    step = 1024
    ff_chunks = tuple((c, min(c + step, ff)) for c in range(0, ff, step))
    body = functools.partial(_ffn_body, d=d, ff_chunks=ff_chunks, final=final)
    return pl.pallas_call(
        body,
        grid=(n // tm,),
        in_specs=[pl.BlockSpec((tm, d), lambda i: (i, 0)),
                  mod.spec(tm, d, third),
                  _resident((1, d)), _resident((d, ff)), _resident((d, ff)), _resident((ff, d)),
                  _resident((1, d))],
        out_specs=pl.BlockSpec((tm, d), lambda i: (i, 0)),
        out_shape=jax.ShapeDtypeStruct((n, d), F32),
        compiler_params=_cparams(("parallel",)),
        name="ffn_final" if final else "ffn",
    )(x, mod.arr, g.reshape(1, d), w1, w3, w2, g_final.reshape(1, d))


def _mix_in_body(x_ref, mod_ref, g_ref, win_ref, gqa_ref, wq_ref, gkva_ref, wkv_ref, wgate_ref,
                 bgate_ref, tab_ref, *out_refs, d, sample):
    if sample:
        (ql_ref, ckv_ref, kr_ref, gq_ref, gk_ref, gv_ref, la_ref, gg_ref) = out_refs
    else:
        (q_ref, k_ref, v_ref, ckv_ref, kr_ref, gq_ref, gk_ref, gv_ref, la_ref, gg_ref) = out_refs
    x = x_ref[...]
    shift, scale = mod_ref[:, 0:d], mod_ref[:, d:2 * d]
    h = (_rms(x, g_ref[...]) * (1.0 + scale) + shift).astype(BF16)
    proj = _dot(h, win_ref[...])

    cq, sq, ck, sk = (tab_ref[:, i * LANE:(i + 1) * LANE] for i in range(4))

    qn = _rms(proj[:, C_QA:C_KVA], gqa_ref[...]).astype(BF16)
    qq = _dot(qn, wq_ref[...])
    half = MLA_HEADS * HEAD_PAD
    q_heads = [qq[:, hh * HEAD_PAD:(hh + 1) * HEAD_PAD] * cq
               + qq[:, half + hh * HEAD_PAD:half + (hh + 1) * HEAD_PAD] * sq
               for hh in range(MLA_HEADS)]

    ckv = _rms(proj[:, C_KVA:C_GQ], gkva_ref[...])
    ckv_ref[...] = ckv
    blk_a = proj[:, C_BLKA:C_BLKB]
    kr_blk = blk_a * ck + proj[:, C_BLKB:C_END] * sk
    kr_ref[...] = kr_blk[:, 0:MLA_ROPE]

    if sample:
        q_all = jnp.concatenate(q_heads, axis=1).astype(BF16)
        ql_ref[...] = _dot(q_all, wkv_ref[...]).astype(BF16)
    else:
        for hh in range(MLA_HEADS):
            q_ref[hh] = q_heads[hh].astype(BF16)
        lhs = jnp.concatenate([ckv.astype(BF16), kr_blk.astype(BF16)], axis=1)
        kv = _dot(lhs, wkv_ref[...])
        for hh in range(MLA_HEADS):
            k_ref[hh] = kv[:, hh * HEAD_PAD:(hh + 1) * HEAD_PAD].astype(BF16)
            v_ref[hh] = kv[:, half + hh * HEAD_PAD:half + (hh + 1) * HEAD_PAD].astype(BF16)

    gq_ref[...] = proj[:, C_GQ:C_GK]
    gk_ref[...] = proj[:, C_GK:C_GV]
    gv_ref[...] = proj[:, C_GV:C_GG]
    gg_ref[...] = proj[:, C_GG:C_BLKA]
    z = _dot(blk_a.astype(BF16), wgate_ref[...]) + bgate_ref[...]
    la_ref[...] = (jnp.minimum(z, 0.0) - jnp.log1p(jnp.exp(-jnp.abs(z)))) / GLA_GATE_TAU


def _mix_in(x, mod, g, w_in_r, g_qa, wq, g_kva, wkv, wgate, b_gate, tab, sample, tm, n_seq):
    n, d = x.shape
    t_seq = n // n_seq
    n_pos_tiles = tab.shape[0] // tm
    hk = GLA_HEADS * GLA_DK
    hv = GLA_HEADS * GLA_DV
    tok = lambda w: pl.BlockSpec((tm, w), lambda i: (i, 0))
    tok_shape = lambda w, dt=F32: jax.ShapeDtypeStruct((n, w), dt)
    gla_specs = [tok(hk), tok(hk), tok(hv), tok(hk), tok(hv)]
    gla_shapes = [tok_shape(hk), tok_shape(hk), tok_shape(hv), tok_shape(hk), tok_shape(hv)]
    if sample:
        wl = wkv.shape[1]
        out_specs = [tok(wl), tok(MLA_KV_LORA), tok(MLA_ROPE)] + gla_specs
        out_shape = [tok_shape(wl, BF16), tok_shape(MLA_KV_LORA), tok_shape(MLA_ROPE)] + gla_shapes
    else:
        per = t_seq // tm
        head_spec = pl.BlockSpec((None, MLA_HEADS, tm, HEAD_PAD), lambda i: (i // per, 0, i % per, 0))
        head_shape = jax.ShapeDtypeStruct((n_seq, MLA_HEADS, t_seq, HEAD_PAD), BF16)
        out_specs = [head_spec] * 3 + [tok(MLA_KV_LORA), tok(MLA_ROPE)] + gla_specs
        out_shape = [head_shape] * 3 + [tok_shape(MLA_KV_LORA), tok_shape(MLA_ROPE)] + gla_shapes
    body = functools.partial(_mix_in_body, d=d, sample=sample)
    return pl.pallas_call(
        body,
        grid=(n // tm,),
        in_specs=[pl.BlockSpec((tm, d), lambda i: (i, 0)),
                  mod.spec(tm, d, 1),
                  _resident((1, d)), _resident(w_in_r.shape), _resident((1, MLA_Q_LORA)),
                  _resident(wq.shape), _resident((1, MLA_KV_LORA)), _resident(wkv.shape),
                  _resident(wgate.shape), _resident((1, hk)),
                  pl.BlockSpec((tm, 4 * LANE), lambda i: (i % n_pos_tiles, 0))],
        out_specs=out_specs,
        out_shape=out_shape,
        compiler_params=_cparams(("parallel",)),
        name="mix_in_sample" if sample else "mix_in_prompt",
    )(x, mod.arr, g.reshape(1, d), w_in_r, g_qa.reshape(1, -1), wq, g_kva.reshape(1, -1), wkv,
      wgate, b_gate.reshape(1, hk), tab)


def _attn_prompt_body(q_ref, k_ref, v_ref, o_ref, *, tq):
    i = pl.program_id(2)
    q = q_ref[...]

    def scores(j):
        off = pl.multiple_of(j * tq, tq)
        return _dot_nt(q, k_ref[pl.ds(off, tq), :]), v_ref[pl.ds(off, tq), :]

    s, v = scores(i)
    row = lax.broadcasted_iota(jnp.int32, s.shape, 0)
    col = lax.broadcasted_iota(jnp.int32, s.shape, 1)
    s = jnp.where(col <= row, s, jnp.finfo(F32).min)
    m = jnp.max(s, axis=-1, keepdims=True)
    p = jnp.exp(s - m)
    l = jnp.sum(p, axis=-1, keepdims=True)
    acc = _dot(p.astype(BF16), v)

    def step(j, carry):
        m, l, acc = carry
        s, v = scores(j)
        m_new = jnp.maximum(m, jnp.max(s, axis=-1, keepdims=True))
        alpha = jnp.exp(m - m_new)
        p = jnp.exp(s - m_new)
        l = alpha * l + jnp.sum(p, axis=-1, keepdims=True)
        acc = alpha * acc + _dot(p.astype(BF16), v)
        return m_new, l, acc

    m, l, acc = lax.fori_loop(0, i, step, (m, l, acc))
    o_ref[...] = (acc / l).astype(o_ref.dtype)


def _attn_prompt(q, k, v, tq):
    b, hh, t, w = q.shape
    body = functools.partial(_attn_prompt_body, tq=tq)
    return pl.pallas_call(
        body,
        grid=(b, hh, t // tq),
        in_specs=[pl.BlockSpec((None, None, tq, w), lambda bi, hi, i: (bi, hi, i, 0)),
                  pl.BlockSpec((None, None, t, w), lambda bi, hi, i: (bi, hi, 0, 0)),
                  pl.BlockSpec((None, None, t, w), lambda bi, hi, i: (bi, hi, 0, 0))],
        out_specs=pl.BlockSpec((None, tq, w), lambda bi, hi, i: (bi, i, hi)),
        out_shape=jax.ShapeDtypeStruct((b, t, hh * w), BF16),
        compiler_params=_cparams(("parallel", "parallel", "arbitrary")),
        name="attn_prompt",
    )(q, k, v)


def _attn_sample_body(pt_ref, ql_ref, qr_ref, cn_ref, kn_ref, *rest, n_pg, t_new):
    ck_refs = rest[:n_pg]
    kr_refs = rest[n_pg:2 * n_pg]
    o_ref = rest[2 * n_pg]
    m_sc, l_sc, acc_sc = rest[2 * n_pg + 1:]
    j = pl.program_id(1)
    ql = ql_ref[...]
    qr = qr_ref[...]

    @pl.when(j == 0)
    def _():
        cn = cn_ref[...]
        s = _dot_nt(ql.astype(F32), cn) + _dot_nt(qr.astype(F32), kn_ref[...])
        row_tok = lax.broadcasted_iota(jnp.int32, s.shape, 0) // MLA_HEADS
        col = lax.broadcasted_iota(jnp.int32, s.shape, 1)
        s = jnp.where(col <= row_tok, s, jnp.finfo(F32).min)
        m = jnp.max(s, axis=-1, keepdims=True)
        p = jnp.exp(s - m)
        m_sc[...] = m
        l_sc[...] = jnp.sum(p, axis=-1, keepdims=True)
        acc_sc[...] = _dot(p, cn)

    cks = [r[...].astype(BF16) for r in ck_refs]
    s = jnp.concatenate(
        [_dot_nt(ql, ck) + _dot_nt(qr, kr[...].astype(BF16)) for ck, kr in zip(cks, kr_refs)], axis=1)
    m = m_sc[...]
    m_new = jnp.maximum(m, jnp.max(s, axis=-1, keepdims=True))
    alpha = jnp.exp(m - m_new)
    p = jnp.exp(s - m_new)
    l_sc[...] = alpha * l_sc[...] + jnp.sum(p, axis=-1, keepdims=True)
    pb = p.astype(BF16)
    pv = _dot(pb[:, 0:PAGE_SIZE], cks[0])
    for g in range(1, n_pg):
        pv = pv + _dot(pb[:, g * PAGE_SIZE:(g + 1) * PAGE_SIZE], cks[g])
    acc_sc[...] = alpha * acc_sc[...] + pv
    m_sc[...] = m_new

    @pl.when(j == pl.num_programs(1) - 1)
    def _():
        o_ref[...] = acc_sc[...] / l_sc[...]


def _attn_sample(page_table, q_lat, q_rope, ckv_new, kr_new, cache_ckv, cache_krope, n_pg):
    n_seq, rows, c = q_lat.shape
    t_new = ckv_new.shape[1]
    r = q_rope.shape[2]
    n_pages = page_table.shape[1]
    seq = lambda shape: pl.BlockSpec((None,) + shape, lambda b, j, pt: (b, 0, 0))
    page = lambda w, g: pl.BlockSpec((None, PAGE_SIZE, w), lambda b, j, pt: (pt[b, j * n_pg + g], 0, 0))
    body = functools.partial(_attn_sample_body, n_pg=n_pg, t_new=t_new)
    grid_spec = pltpu.PrefetchScalarGridSpec(
        num_scalar_prefetch=1,
        grid=(n_seq, n_pages // n_pg),
        in_specs=[seq((rows, c)), seq((rows, r)), seq((t_new, c)), seq((t_new, r))]
                 + [page(c, g) for g in range(n_pg)] + [page(r, g) for g in range(n_pg)],
        out_specs=seq((rows, c)),
        scratch_shapes=[pltpu.VMEM((rows, 1), F32), pltpu.VMEM((rows, 1), F32), pltpu.VMEM((rows, c), F32)],
    )
    return pl.pallas_call(
        body,
        grid_spec=grid_spec,
        out_shape=jax.ShapeDtypeStruct((n_seq, rows, c), F32),
        compiler_params=_cparams(("parallel", "arbitrary")),
        name="attn_sample",
    )(page_table, q_lat, q_rope, ckv_new, kr_new, *([cache_ckv] * n_pg), *([cache_krope] * n_pg))


def _gla_body(q_ref, k_ref, v_ref, la_ref, s0_ref, o_ref, s_ref, st_sc, *, n_seq, n_chunks, chunk):
    for sq in range(n_seq):
        for hh in range(GLA_HEADS):
            st_sc[sq, hh] = s0_ref[sq, hh].T
    r = lax.broadcasted_iota(jnp.int32, (chunk, chunk), 0)
    c = lax.broadcasted_iota(jnp.int32, (chunk, chunk), 1)
    causal = c <= r
    tri = causal.astype(BF16)

    def one_chunk(ci, _):
        off = pl.multiple_of(ci * chunk, chunk)
        for sq in range(n_seq):
            g = la_ref[sq, pl.ds(off, chunk), :]
            g_hi = g.astype(BF16)
            g_lo = (g - g_hi.astype(F32)).astype(BF16)
            b = _dot(tri, g_hi) + _dot(tri, g_lo)
            b_last = b[chunk - 1:chunk, :]
            q = q_ref[sq, pl.ds(off, chunk), :]
            k = k_ref[sq, pl.ds(off, chunk), :]
            v = v_ref[sq, pl.ds(off, chunk), :].astype(BF16)
            qe = (q * jnp.exp(b)).astype(BF16)
            ke = (k * jnp.exp(-b)).astype(BF16)
            kd = (k * jnp.exp(b_last - b)).astype(BF16)
            decay = jnp.exp(b_last)
            outs = []
            for hh in range(GLA_HEADS):
                ks = slice(hh * GLA_DK, (hh + 1) * GLA_DK)
                vs = slice(hh * GLA_DV, (hh + 1) * GLA_DV)
                st = st_sc[sq, hh]
                a = jnp.where(causal, _dot_nt(qe[:, ks], ke[:, ks]), 0.0)
                outs.append(_dot_nt(qe[:, ks], st.astype(BF16)) + _dot(a.astype(BF16), v[:, vs]))
                st_sc[sq, hh] = st * decay[:, ks] + _dot_tn(v[:, vs], kd[:, ks])
            o_ref[sq, pl.ds(off, chunk), :] = jnp.concatenate(outs, axis=1)
        return 0

    lax.fori_loop(0, n_chunks, one_chunk, 0)
    for sq in range(n_seq):
        for hh in range(GLA_HEADS):
            s_ref[sq, hh] = st_sc[sq, hh].T


def _gla(gq, gk, gv, la, s0, chunk, seqs_per_step):
    b, t, hk = gq.shape
    hv = gv.shape[2]
    g = seqs_per_step
    blk = lambda w: pl.BlockSpec((g, t, w), lambda i: (i, 0, 0))
    st = pl.BlockSpec((g, GLA_HEADS, GLA_DK, GLA_DV), lambda i: (i, 0, 0, 0))
    body = functools.partial(_gla_body, n_seq=g, n_chunks=t // chunk, chunk=chunk)
    return pl.pallas_call(
        body,
        grid=(b // g,),
        in_specs=[blk(hk), blk(hk), blk(hv), blk(hk), st],
        out_specs=[blk(hv), st],
        out_shape=[jax.ShapeDtypeStruct((b, t, hv), F32),
                   jax.ShapeDtypeStruct((b, GLA_HEADS, GLA_DK, GLA_DV), F32)],
        scratch_shapes=[pltpu.VMEM((g, GLA_HEADS, GLA_DV, GLA_DK), F32)],
        compiler_params=_cparams(("parallel",)),
        name="gla",
    )(gq, gk, gv, la, s0)


def _mix_out_body(x_ref, mod_ref, om_ref, og_ref, gg_ref, go_ref, wlat_ref, wom_ref, wog_ref, o_ref,
                  *, d, sample):
    gate = mod_ref[:, 2 * d:3 * d]
    om = om_ref[...]
    if sample:
        om = _dot(om.astype(BF16), wlat_ref[...]).astype(BF16)
    parts = []
    for hh in range(GLA_HEADS):
        vs = slice(hh * GLA_DV, (hh + 1) * GLA_DV)
        parts.append(_rms(og_ref[:, vs], go_ref[...]) * _silu(gg_ref[:, vs]))
    og = jnp.concatenate(parts, axis=1).astype(BF16)
    mix = _dot(om, wom_ref[...]) + _dot(og, wog_ref[...])
    o_ref[...] = x_ref[...] + gate * mix


def _mix_out(x, mod, o_mla, o_gla, gg, g_gla_o, w_lat, w_o_mla, w_o_gla, sample, tm):
    n, d = x.shape
    hv = GLA_HEADS * GLA_DV
    tok = lambda w: pl.BlockSpec((tm, w), lambda i: (i, 0))
    body = functools.partial(_mix_out_body, d=d, sample=sample)
    return pl.pallas_call(
        body,
        grid=(n // tm,),
        in_specs=[tok(d), mod.spec(tm, d, 1), tok(o_mla.shape[1]), tok(hv), tok(hv),
                  _resident((1, GLA_DV)), _resident(w_lat.shape), _resident(w_o_mla.shape),
                  _resident(w_o_gla.shape)],
        out_specs=tok(d),
        out_shape=jax.ShapeDtypeStruct((n, d), F32),
        compiler_params=_cparams(("parallel",)),
        name="mix_out_sample" if sample else "mix_out_prompt",
    )(x, mod.arr, o_mla, o_gla, gg, g_gla_o.reshape(1, GLA_DV), w_lat, w_o_mla, w_o_gla)


def _prep_w_in(w_in):
    d = w_in.shape[0]
    o = 0
    parts = {}
    for name, size in (("qa", MLA_Q_LORA), ("kva", MLA_KV_LORA), ("kr", MLA_ROPE),
                       ("gq", GLA_HEADS * GLA_DK), ("gk", GLA_HEADS * GLA_DK), ("gv", GLA_HEADS * GLA_DV),
                       ("ga", GLA_GATE_RANK), ("gg", GLA_HEADS * GLA_DV)):
        parts[name] = w_in[:, o:o + size]
        o += size
    hr = MLA_ROPE // 2
    kr = parts["kr"]
    z = lambda w: jnp.zeros((d, w), w_in.dtype)
    blk_a = jnp.concatenate([kr, parts["ga"], z(LANE - MLA_ROPE - GLA_GATE_RANK)], axis=1)
    blk_b = jnp.concatenate([kr[:, hr:], kr[:, :hr], z(LANE - MLA_ROPE)], axis=1)
    w = jnp.concatenate([parts["qa"], parts["kva"], parts["gq"] * (GLA_DK ** -0.5), parts["gk"],
                         parts["gv"], parts["gg"], blk_a, blk_b], axis=1)
    return w.astype(BF16)


def _prep_wq(w_qb):
    ql = w_qb.shape[0]
    hr = MLA_ROPE // 2
    nope, r1, r2 = w_qb[..., :MLA_NOPE], w_qb[..., MLA_NOPE:MLA_NOPE + hr], w_qb[..., MLA_NOPE + hr:]
    zp = jnp.zeros((ql, MLA_HEADS, HEAD_PAD - MLA_NOPE - MLA_ROPE), w_qb.dtype)
    zn = jnp.zeros_like(nope)
    a = jnp.concatenate([nope, r1, r2, zp], axis=-1).reshape(ql, MLA_HEADS * HEAD_PAD)
    b = jnp.concatenate([zn, r2, r1, zp], axis=-1).reshape(ql, MLA_HEADS * HEAD_PAD)
    return (jnp.concatenate([a, b], axis=1) * MLA_SCALE).astype(BF16)


def _prep_wkv_prompt(w_kvb):
    c = w_kvb.shape[0]
    rows = c + LANE
    k = jnp.zeros((rows, MLA_HEADS, HEAD_PAD), F32)
    k = k.at[:c, :, :MLA_NOPE].set(w_kvb[..., :MLA_NOPE])
    eye = jnp.broadcast_to(jnp.eye(MLA_ROPE, dtype=F32)[:, None, :], (MLA_ROPE, MLA_HEADS, MLA_ROPE))
    k = k.at[c:c + MLA_ROPE, :, MLA_NOPE:MLA_NOPE + MLA_ROPE].set(eye)
    v = jnp.zeros((rows, MLA_HEADS, HEAD_PAD), F32)
    v = v.at[:c, :, :MLA_V].set(w_kvb[..., MLA_NOPE:])
    return jnp.concatenate([k.reshape(rows, -1), v.reshape(rows, -1)], axis=1).astype(BF16)


def _prep_wq_sample(w_kvb):
    c = w_kvb.shape[0]
    lat = jnp.zeros((MLA_HEADS, HEAD_PAD, MLA_HEADS, c), F32)
    rope = jnp.zeros((MLA_HEADS, HEAD_PAD, MLA_HEADS, MLA_ROPE), F32)
    for hh in range(MLA_HEADS):
        lat = lat.at[hh, :MLA_NOPE, hh, :].set(w_kvb[:, hh, :MLA_NOPE].T)
        rope = rope.at[hh, MLA_NOPE:MLA_NOPE + MLA_ROPE, hh, :].set(jnp.eye(MLA_ROPE, dtype=F32))
    rows = MLA_HEADS * HEAD_PAD
    return jnp.concatenate([lat.reshape(rows, -1), rope.reshape(rows, -1)], axis=1).astype(BF16)


def _prep_w_lat_out(w_kvb):
    c = w_kvb.shape[0]
    w = jnp.zeros((MLA_HEADS, c, MLA_HEADS, HEAD_PAD), F32)
    for hh in range(MLA_HEADS):
        w = w.at[hh, :, hh, :MLA_V].set(w_kvb[:, hh, MLA_NOPE:])
    return w.reshape(MLA_HEADS * c, MLA_HEADS * HEAD_PAD).astype(BF16)


def _prep_w_o(w_o):
    d = w_o.shape[1]
    n_mla = MLA_HEADS * MLA_V
    w_mla = jnp.zeros((MLA_HEADS, HEAD_PAD, d), F32)
    w_mla = w_mla.at[:, :MLA_V, :].set(w_o[:n_mla].reshape(MLA_HEADS, MLA_V, d))
    return w_mla.reshape(MLA_HEADS * HEAD_PAD, d).astype(BF16), w_o[n_mla:].astype(BF16)


def _prep_gate(w_gate_b):
    w = jnp.zeros((LANE, w_gate_b.shape[1]), F32)
    return w.at[MLA_ROPE:MLA_ROPE + GLA_GATE_RANK].set(w_gate_b).astype(BF16)


def _rope_table(pos):
    hr = MLA_ROPE // 2
    inv = ROPE_THETA ** (-jnp.arange(0, MLA_ROPE, 2, dtype=F32) / MLA_ROPE)
    ang = pos[:, None] * inv[None, :]
    cos, sin = jnp.cos(ang), jnp.sin(ang)
    t = pos.shape[0]
    z = lambda w: jnp.zeros((t, w), F32)
    pad_q = HEAD_PAD - MLA_NOPE - MLA_ROPE
    cq = jnp.concatenate([jnp.ones((t, MLA_NOPE), F32), cos, cos, z(pad_q)], axis=1)
    sq = jnp.concatenate([z(MLA_NOPE), -sin, sin, z(pad_q)], axis=1)
    ck = jnp.concatenate([cos, cos, z(LANE - 2 * hr)], axis=1)
    sk = jnp.concatenate([-sin, sin, z(LANE - 2 * hr)], axis=1)
    return jnp.concatenate([cq, sq, ck, sk], axis=1)


TM_TOKENS = 512
TQ_PROMPT = 512
PAGES_PER_STEP = 16
GLA_SAMPLE_PAD = 16
GLA_SAMPLE_SEQS = 8


def kernel(x_prompt, x_sample, cache_ckv, cache_krope, state_gla, page_table, c_prompt, c_sample,
           w_ada, b_ada, norm_ffn1, ffn1_w1, ffn1_w3, ffn1_w2, norm_mix, w_in, g_qa, w_qb, g_kva, w_kvb,
           w_gate_b, b_gate, g_gla_o, w_o, norm_ffn2, ffn2_w1, ffn2_w3, ffn2_w2, norm_final):
    bp, tp, d = x_prompt.shape
    bs, ts, _ = x_sample.shape
    depth = w_ada.shape[0]
    n_pages = page_table.shape[1]
    past_len = n_pages * PAGE_SIZE

    tab_p = _rope_table(jnp.arange(tp, dtype=F32))
    tm_s = min(TM_TOKENS, bs * ts)
    tab_s = jnp.tile(_rope_table(past_len + jnp.arange(ts, dtype=F32)), (tm_s // ts, 1))

    hp = x_prompt.reshape(bp * tp, d)
    hs = x_sample.reshape(bs * ts, d)
    outs = [[] for _ in range(6)]
    for l in range(depth):
        m = _ada(jnp.concatenate([c_prompt, c_sample], axis=0), w_ada[l], b_ada[l])
        mod_p = _Mod(m[:bp], False, tp)
        mod_s = _Mod(jnp.repeat(m[bp:], ts, axis=0), True, ts)

        f1 = (norm_ffn1[l], ffn1_w1[l].astype(BF16), ffn1_w3[l].astype(BF16), ffn1_w2[l].astype(BF16))
        f2 = (norm_ffn2[l], ffn2_w1[l].astype(BF16), ffn2_w3[l].astype(BF16), ffn2_w2[l].astype(BF16))
        w_in_r = _prep_w_in(w_in[l])
        wq = _prep_wq(w_qb[l])
        wkv_p = _prep_wkv_prompt(w_kvb[l])
        wq_s = _prep_wq_sample(w_kvb[l])
        w_lat = _prep_w_lat_out(w_kvb[l])
        w_o_mla, w_o_gla = _prep_w_o(w_o[l])
        wgate = _prep_gate(w_gate_b[l])
        last = l == depth - 1

        hp = _ffn(hp, mod_p, 0, *f1, norm_final, False, TM_TOKENS)
        q, k, v, ckv_p, kr_p, gq, gk, gv, la, gg = _mix_in(
            hp, mod_p, norm_mix[l], w_in_r, g_qa[l], wq, g_kva[l], wkv_p, wgate, b_gate[l], tab_p,
            False, TM_TOKENS, bp)
        o_mla = _attn_prompt(q, k, v, TQ_PROMPT).reshape(bp * tp, -1)
        seq3 = lambda a: a.reshape(bp, tp, -1)
        s0_p = jnp.zeros((bp, GLA_HEADS, GLA_DK, GLA_DV), F32)
        o_gla, s_p = _gla(seq3(gq), seq3(gk), seq3(gv), seq3(la), s0_p, GLA_CHUNK, 1)
        hp = _mix_out(hp, mod_p, o_mla, o_gla.reshape(bp * tp, -1), gg, g_gla_o[l], w_lat, w_o_mla,
                      w_o_gla, False, TM_TOKENS)
        hp = _ffn(hp, mod_p, 2, *f2, norm_final, last, TM_TOKENS)

        hs = _ffn(hs, mod_s, 0, *f1, norm_final, False, tm_s)
        ql, ckv_s, kr_s, gq, gk, gv, la, gg = _mix_in(
            hs, mod_s, norm_mix[l], w_in_r, g_qa[l], wq, g_kva[l], wq_s, wgate, b_gate[l], tab_s,
            True, tm_s, bs)
        n_lat = MLA_HEADS * MLA_KV_LORA
        rows = ts * MLA_HEADS
        q_lat = ql[:, :n_lat].reshape(bs, rows, MLA_KV_LORA)
        q_rope = ql[:, n_lat:].reshape(bs, rows, MLA_ROPE)
        o_lat = _attn_sample(page_table, q_lat, q_rope, ckv_s.reshape(bs, ts, -1), kr_s.reshape(bs, ts, -1),
                             cache_ckv[l], cache_krope[l], PAGES_PER_STEP)
        pad = lambda a: jnp.pad(a.reshape(bs, ts, -1), ((0, 0), (0, GLA_SAMPLE_PAD - ts), (0, 0)))
        o_gla, s_s = _gla(pad(gq), pad(gk), pad(gv), pad(la), state_gla[l], GLA_SAMPLE_PAD, GLA_SAMPLE_SEQS)
        hs = _mix_out(hs, mod_s, o_lat.reshape(bs * ts, n_lat), o_gla[:, :ts].reshape(bs * ts, -1), gg,
                      g_gla_o[l], w_lat, w_o_mla, w_o_gla, True, tm_s)
        hs = _ffn(hs, mod_s, 2, *f2, norm_final, last, tm_s)

        for lst, a in zip(outs, (ckv_p.reshape(bp, tp, -1), kr_p.reshape(bp, tp, -1), s_p,
                                 ckv_s.reshape(bs, ts, -1), kr_s.reshape(bs, ts, -1), s_s)):
            lst.append(a)

    return (hp.reshape(bp, tp, d), hs.reshape(bs, ts, d)) + tuple(jnp.stack(o) for o in outs)
```

```python
import functools
import math

import jax
import jax.numpy as jnp
from jax import lax
from jax.experimental import pallas as pl
from jax.experimental.pallas import tpu as pltpu

F32 = jnp.float32
BF16 = jnp.bfloat16

PAGE_SIZE = 128
MLA_HEADS = 8
MLA_NOPE = 64
MLA_ROPE = 32
MLA_V = 64
MLA_Q_LORA = 384
MLA_KV_LORA = 256
MLA_SCALE = (MLA_NOPE + MLA_ROPE) ** -0.5
LOG2_E = math.log2(math.e)
ROPE_THETA = 10000.0
GLA_HEADS = 4
GLA_DK = 64
GLA_DV = 128
GLA_GATE_RANK = 16
GLA_GATE_TAU = 16.0
GLA_CHUNK = 64
N_ADA = 9
EPS = 1e-6

LANE = 128
HEAD_PAD = 128
VMEM_LIMIT_BYTES = 56 * 1024 * 1024

C_QA = 0
C_KVA = C_QA + MLA_Q_LORA
C_GQ = C_KVA + MLA_KV_LORA
C_GK = C_GQ + GLA_HEADS * GLA_DK
C_GV = C_GK + GLA_HEADS * GLA_DK
C_GG = C_GV + GLA_HEADS * GLA_DV
C_BLKA = C_GG + GLA_HEADS * GLA_DV
C_BLKB = C_BLKA + LANE
C_END = C_BLKB + LANE


def _cparams(sem):
    return pltpu.CompilerParams(dimension_semantics=sem, vmem_limit_bytes=VMEM_LIMIT_BYTES)


def _resident(shape):
    return pl.BlockSpec(shape, lambda *_: (0,) * len(shape), pipeline_mode=pl.Buffered(1))


def _rms(x, g):
    return x * lax.rsqrt(jnp.mean(x * x, axis=-1, keepdims=True) + EPS) * g


def _silu(x):
    return x * jax.nn.sigmoid(x)


def _dot(a, b):
    return jnp.dot(a, b, preferred_element_type=F32)


def _dot_nt(a, b):
    return lax.dot_general(a, b, (((1,), (1,)), ((), ())), preferred_element_type=F32)


def _dot_tn(a, b):
    return lax.dot_general(a, b, (((0,), (0,)), ((), ())), preferred_element_type=F32)


def _ada_body(c_ref, w_ref, b_ref, o_ref):
    c = _silu(c_ref[...]).astype(BF16)
    o_ref[...] = _dot(c, w_ref[...].astype(BF16)) + b_ref[...]


def _ada(c_all, w_ada, b_ada):
    n, d = c_all.shape
    n_out = w_ada.shape[1]
    tn = 1024
    return pl.pallas_call(
        _ada_body,
        grid=(n_out // tn,),
        in_specs=[pl.BlockSpec((n, d), lambda j: (0, 0)),
                  pl.BlockSpec((d, tn), lambda j: (0, j)),
                  pl.BlockSpec((1, tn), lambda j: (0, j))],
        out_specs=pl.BlockSpec((n, tn), lambda j: (0, j)),
        out_shape=jax.ShapeDtypeStruct((n, n_out), F32),
        compiler_params=_cparams(("arbitrary",)),
        name="ada",
    )(c_all, w_ada, b_ada.reshape(1, n_out))


class _Mod:
    def __init__(self, m, per_token, tokens_per_seq):
        self.per_token = per_token
        self.tokens_per_seq = tokens_per_seq
        self.arr = m if per_token else m.reshape(m.shape[0], 1, m.shape[1])

    def spec(self, tm, d, third):
        if self.per_token:
            return pl.BlockSpec((tm, 3 * d), lambda i: (i, third))
        per = self.tokens_per_seq // tm
        return pl.BlockSpec((None, 1, 3 * d), lambda i: (i // per, 0, third))


def _ffn_body(x_ref, mod_ref, g_ref, w1_ref, w3_ref, w2_ref, gf_ref, o_ref, *, d, ff_chunks, final):
    x = x_ref[...]
    shift, scale, gate = mod_ref[:, 0:d], mod_ref[:, d:2 * d], mod_ref[:, 2 * d:3 * d]
    h = (_rms(x, g_ref[...]) * (1.0 + scale) + shift).astype(BF16)
    y = None
    for c0, c1 in ff_chunks:
        a = _dot(h, w1_ref[:, c0:c1])
        b = _dot(h, w3_ref[:, c0:c1])
        u = (_silu(a) * b).astype(BF16)
        part = _dot(u, w2_ref[c0:c1, :])
        y = part if y is None else y + part
    out = x + 0.5 * gate * y
    if final:
        out = _rms(out, gf_ref[...])
    o_ref[...] = out


def _ffn(x, mod, third, g, w1, w3, w2, g_final, final, tm):
    n, d = x.shape
    ff = w1.shape[1]
    step = 1024
    ff_chunks = tuple((c, min(c + step, ff)) for c in range(0, ff, step))
    body = functools.partial(_ffn_body, d=d, ff_chunks=ff_chunks, final=final)
    return pl.pallas_call(
        body,
        grid=(n // tm,),
        in_specs=[pl.BlockSpec((tm, d), lambda i: (i, 0)),
                  mod.spec(tm, d, third),
                  _resident((1, d)), _resident((d, ff)), _resident((d, ff)), _resident((ff, d)),
                  _resident((1, d))],
        out_specs=pl.BlockSpec((tm, d), lambda i: (i, 0)),
        out_shape=jax.ShapeDtypeStruct((n, d), F32),
        compiler_params=_cparams(("parallel",)),
        name="ffn_final" if final else "ffn",
    )(x, mod.arr, g.reshape(1, d), w1, w3, w2, g_final.reshape(1, d))


def _mix_in_body(x_ref, mod_ref, g_ref, win_ref, gqa_ref, wq_ref, gkva_ref, wkv_ref, wgate_ref,
                 bgate_ref, tab_ref, *out_refs, d, sample):
    if sample:
        (ql_ref, ckv_ref, kr_ref, gq_ref, gk_ref, gv_ref, la_ref, gg_ref) = out_refs
    else:
        (q_ref, k_ref, v_ref, ckv_ref, kr_ref, gq_ref, gk_ref, gv_ref, la_ref, gg_ref) = out_refs
    x = x_ref[...]
    shift, scale = mod_ref[:, 0:d], mod_ref[:, d:2 * d]
    h = (_rms(x, g_ref[...]) * (1.0 + scale) + shift).astype(BF16)
    proj = _dot(h, win_ref[...])

    cq, sq, ck, sk = (tab_ref[:, i * LANE:(i + 1) * LANE] for i in range(4))

    qn = _rms(proj[:, C_QA:C_KVA], gqa_ref[...]).astype(BF16)
    qq = _dot(qn, wq_ref[...])
    half = MLA_HEADS * HEAD_PAD
    q_heads = [qq[:, hh * HEAD_PAD:(hh + 1) * HEAD_PAD] * cq
               + qq[:, half + hh * HEAD_PAD:half + (hh + 1) * HEAD_PAD] * sq
               for hh in range(MLA_HEADS)]

    ckv = _rms(proj[:, C_KVA:C_GQ], gkva_ref[...])
    ckv_ref[...] = ckv
    blk_a = proj[:, C_BLKA:C_BLKB]
    kr_blk = blk_a * ck + proj[:, C_BLKB:C_END] * sk
    kr_ref[...] = kr_blk[:, 0:MLA_ROPE]

    if sample:
        q_all = jnp.concatenate(q_heads, axis=1).astype(BF16)
        ql_ref[...] = _dot(q_all, wkv_ref[...]).astype(BF16)
    else:
        for hh in range(MLA_HEADS):
            q_ref[hh] = q_heads[hh].astype(BF16)
        lhs = jnp.concatenate([ckv.astype(BF16), kr_blk.astype(BF16)], axis=1)
        kv = _dot(lhs, wkv_ref[...])
        for hh in range(MLA_HEADS):
            k_ref[hh] = kv[:, hh * HEAD_PAD:(hh + 1) * HEAD_PAD].astype(BF16)
        for pr in range(MLA_HEADS * MLA_V // LANE):
            v_ref[pr] = kv[:, half + pr * LANE:half + (pr + 1) * LANE].astype(BF16)

    gq_ref[...] = proj[:, C_GQ:C_GK]
    gk_ref[...] = proj[:, C_GK:C_GV]
    gv_ref[...] = proj[:, C_GV:C_GG]
    gg_ref[...] = proj[:, C_GG:C_BLKA]
    z = _dot(blk_a.astype(BF16), wgate_ref[...]) + bgate_ref[...]
    la_ref[...] = (jnp.minimum(z, 0.0) - jnp.log1p(jnp.exp(-jnp.abs(z)))) / GLA_GATE_TAU


def _mix_in(x, mod, g, w_in_r, g_qa, wq, g_kva, wkv, wgate, b_gate, tab, sample, tm, n_seq):
    n, d = x.shape
    t_seq = n // n_seq
    n_pos_tiles = tab.shape[0] // tm
    hk = GLA_HEADS * GLA_DK
    hv = GLA_HEADS * GLA_DV
    tok = lambda w: pl.BlockSpec((tm, w), lambda i: (i, 0))
    tok_shape = lambda w, dt=F32: jax.ShapeDtypeStruct((n, w), dt)
    gla_specs = [tok(hk), tok(hk), tok(hv), tok(hk), tok(hv)]
    gla_shapes = [tok_shape(hk), tok_shape(hk), tok_shape(hv), tok_shape(hk), tok_shape(hv)]
    if sample:
        wl = wkv.shape[1]
        out_specs = [tok(wl), tok(MLA_KV_LORA), tok(MLA_ROPE)] + gla_specs
        out_shape = [tok_shape(wl, BF16), tok_shape(MLA_KV_LORA), tok_shape(MLA_ROPE)] + gla_shapes
    else:
        per = t_seq // tm
        n_vp = MLA_HEADS * MLA_V // LANE
        head_spec = lambda nh: pl.BlockSpec((None, nh, tm, HEAD_PAD), lambda i: (i // per, 0, i % per, 0))
        head_shape = lambda nh: jax.ShapeDtypeStruct((n_seq, nh, t_seq, HEAD_PAD), BF16)
        out_specs = ([head_spec(MLA_HEADS)] * 2 + [head_spec(n_vp), tok(MLA_KV_LORA), tok(MLA_ROPE)]
                     + gla_specs)
        out_shape = ([head_shape(MLA_HEADS)] * 2 + [head_shape(n_vp), tok_shape(MLA_KV_LORA),
                                                    tok_shape(MLA_ROPE)] + gla_shapes)
    body = functools.partial(_mix_in_body, d=d, sample=sample)
    return pl.pallas_call(
        body,
        grid=(n // tm,),
        in_specs=[pl.BlockSpec((tm, d), lambda i: (i, 0)),
                  mod.spec(tm, d, 1),
                  _resident((1, d)), _resident(w_in_r.shape), _resident((1, MLA_Q_LORA)),
                  _resident(wq.shape), _resident((1, MLA_KV_LORA)), _resident(wkv.shape),
                  _resident(wgate.shape), _resident((1, hk)),
                  pl.BlockSpec((tm, 4 * LANE), lambda i: (i % n_pos_tiles, 0))],
        out_specs=out_specs,
        out_shape=out_shape,
        compiler_params=_cparams(("parallel",)),
        name="mix_in_sample" if sample else "mix_in_prompt",
    )(x, mod.arr, g.reshape(1, d), w_in_r, g_qa.reshape(1, -1), wq, g_kva.reshape(1, -1), wkv,
      wgate, b_gate.reshape(1, hk), tab)


def _attn_prompt_body(q_ref, k_ref, v_ref, o_ref, *, tq):
    n_heads = 2
    i = pl.program_id(2)
    qs = [q_ref[hh] for hh in range(n_heads)]

    def scores(hh, j):
        off = pl.multiple_of(j * tq, tq)
        return _dot_nt(qs[hh], k_ref[hh, pl.ds(off, tq), :]), v_ref[pl.ds(off, tq), :]

    init = []
    for hh in range(n_heads):
        s, v = scores(hh, i)
        row = lax.broadcasted_iota(jnp.int32, s.shape, 0)
        col = lax.broadcasted_iota(jnp.int32, s.shape, 1)
        s = jnp.where(col <= row, s, jnp.finfo(F32).min)
        m = jnp.max(s, axis=-1, keepdims=True)
        p = jnp.exp2(s - m)
        init += [m, jnp.sum(p, axis=-1, keepdims=True), _dot(p.astype(BF16), v)]

    def step(j, carry):
        out = []
        for hh in range(n_heads):
            m, l, acc = carry[3 * hh:3 * hh + 3]
            s, v = scores(hh, j)
            m_new = jnp.maximum(m, jnp.max(s, axis=-1, keepdims=True))
            alpha = jnp.exp2(m - m_new)
            p = jnp.exp2(s - m_new)
            out += [m_new, alpha * l + jnp.sum(p, axis=-1, keepdims=True),
                    alpha * acc + _dot(p.astype(BF16), v)]
        return tuple(out)

    fin = lax.fori_loop(0, i, step, tuple(init))
    o_even, o_odd = fin[2] / fin[1], fin[5] / fin[4]
    lane = lax.broadcasted_iota(jnp.int32, o_even.shape, 1)
    o_ref[...] = jnp.where(lane < MLA_V, o_even, o_odd).astype(o_ref.dtype)


def _attn_prompt(q, k, v, tq):
    b, hh, t, w = q.shape
    body = functools.partial(_attn_prompt_body, tq=tq)
    return pl.pallas_call(
        body,
        grid=(b, hh // 2, t // tq),
        in_specs=[pl.BlockSpec((None, 2, tq, w), lambda bi, hi, i: (bi, hi, i, 0)),
                  pl.BlockSpec((None, 2, t, w), lambda bi, hi, i: (bi, hi, 0, 0)),
                  pl.BlockSpec((None, None, t, w), lambda bi, hi, i: (bi, hi, 0, 0))],
        out_specs=pl.BlockSpec((None, tq, w), lambda bi, hi, i: (bi, i, hi)),
        out_shape=jax.ShapeDtypeStruct((b, t, hh // 2 * w), BF16),
        compiler_params=_cparams(("parallel", "parallel", "arbitrary")),
        name="attn_prompt",
    )(q, k, v)


def _attn_sample_body(pt_ref, ql_ref, qr_ref, cn_ref, kn_ref, ckv_hbm, kr_hbm, o_ref, ckv_buf, kr_buf, sems,
                      *, layer, n_chunks, pg):
    b = pl.program_id(0)
    n_seq = pl.num_programs(0)

    def copies(seq, c):
        out = []
        for p in range(pg):
            page = pt_ref[seq, c * pg + p]
            rows = pl.ds(p * PAGE_SIZE, PAGE_SIZE)
            out.append(pltpu.make_async_copy(ckv_hbm.at[layer, page], ckv_buf.at[c, rows, :], sems.at[c, 0]))
            out.append(pltpu.make_async_copy(kr_hbm.at[layer, page], kr_buf.at[c, rows, :], sems.at[c, 1]))
        return out

    @pl.when(b == 0)
    def _():
        for c in range(n_chunks):
            for cp in copies(0, c):
                cp.start()

    ql = ql_ref[...]
    qr = qr_ref[...]

    cn = cn_ref[...]
    s = _dot_nt(ql.astype(F32), cn) + _dot_nt(qr.astype(F32), kn_ref[...])
    row_tok = lax.broadcasted_iota(jnp.int32, s.shape, 0) // MLA_HEADS
    col = lax.broadcasted_iota(jnp.int32, s.shape, 1)
    s = jnp.where(col <= row_tok, s, jnp.finfo(F32).min)
    m = jnp.max(s, axis=-1, keepdims=True)
    p = jnp.exp2(s - m)
    l = jnp.sum(p, axis=-1, keepdims=True)
    acc = _dot(p, cn)

    def scores(c):
        for cp in copies(b, c):
            cp.wait()
        ck = ckv_buf[c].astype(BF16)
        return _dot_nt(ql, ck) + _dot_nt(qr, kr_buf[c].astype(BF16)), ck

    nxt = jnp.minimum(b + 1, n_seq - 1)
    s_next, ck_next = scores(0)
    for c in range(n_chunks):
        s, ck = s_next, ck_next
        if c + 1 < n_chunks:
            s_next, ck_next = scores(c + 1)
        m_new = jnp.maximum(m, jnp.max(s, axis=-1, keepdims=True))
        alpha = jnp.exp2(m - m_new)
        p = jnp.exp2(s - m_new)
        l = alpha * l + jnp.sum(p, axis=-1, keepdims=True)
        acc = alpha * acc + _dot(p.astype(BF16), ck)
        m = m_new
        for cp in copies(nxt, c):
            cp.start()
    o_ref[...] = acc / l

    @pl.when(b == n_seq - 1)
    def _():
        for c in range(n_chunks):
            for cp in copies(nxt, c):
                cp.wait()


def _attn_sample(page_table, q_lat, q_rope, ckv_new, kr_new, cache_ckv, cache_krope, layer, pg):
    n_seq, rows, c = q_lat.shape
    t_new = ckv_new.shape[1]
    r = q_rope.shape[2]
    n_chunks = page_table.shape[1] // pg
    seq = lambda shape: pl.BlockSpec((None,) + shape, lambda b, pt: (b, 0, 0))
    body = functools.partial(_attn_sample_body, layer=layer, n_chunks=n_chunks, pg=pg)
    grid_spec = pltpu.PrefetchScalarGridSpec(
        num_scalar_prefetch=1,
        grid=(n_seq,),
        in_specs=[seq((rows, c)), seq((rows, r)), seq((t_new, c)), seq((t_new, r)),
                  pl.BlockSpec(memory_space=pl.ANY), pl.BlockSpec(memory_space=pl.ANY)],
        out_specs=seq((rows, c)),
        scratch_shapes=[pltpu.VMEM((n_chunks, pg * PAGE_SIZE, c), F32),
                        pltpu.VMEM((n_chunks, pg * PAGE_SIZE, r), F32),
                        pltpu.SemaphoreType.DMA((n_chunks, 2))],
    )
    return pl.pallas_call(
        body,
        grid_spec=grid_spec,
        out_shape=jax.ShapeDtypeStruct((n_seq, rows, c), F32),
        compiler_params=_cparams(("arbitrary",)),
        name="attn_sample",
    )(page_table, q_lat, q_rope, ckv_new, kr_new, cache_ckv, cache_krope)


def _gla_body(q_ref, k_ref, v_ref, la_ref, s0_ref, o_ref, s_ref, st_sc, *, n_seq, n_chunks, chunk):
    for sq in range(n_seq):
        for hh in range(GLA_HEADS):
            st_sc[sq, hh] = s0_ref[sq, hh].T
    r = lax.broadcasted_iota(jnp.int32, (chunk, chunk), 0)
    c = lax.broadcasted_iota(jnp.int32, (chunk, chunk), 1)
    causal = c <= r
    tri = causal.astype(BF16)

    def one_chunk(ci, _):
        off = pl.multiple_of(ci * chunk, chunk)
        for sq in range(n_seq):
            g = la_ref[sq, pl.ds(off, chunk), :]
            g_hi = g.astype(BF16)
            g_lo = (g - g_hi.astype(F32)).astype(BF16)
            b = _dot(tri, g_hi) + _dot(tri, g_lo)
            b_last = b[chunk - 1:chunk, :]
            q = q_ref[sq, pl.ds(off, chunk), :]
            k = k_ref[sq, pl.ds(off, chunk), :]
            v = v_ref[sq, pl.ds(off, chunk), :].astype(BF16)
            qe = (q * jnp.exp(b)).astype(BF16)
            ke = (k * jnp.exp(-b)).astype(BF16)
            kd = (k * jnp.exp(b_last - b)).astype(BF16)
            decay = jnp.exp(b_last)
            outs = []
            for hh in range(GLA_HEADS):
                ks = slice(hh * GLA_DK, (hh + 1) * GLA_DK)
                vs = slice(hh * GLA_DV, (hh + 1) * GLA_DV)
                st = st_sc[sq, hh]
                a = jnp.where(causal, _dot_nt(qe[:, ks], ke[:, ks]), 0.0)
                outs.append(_dot_nt(qe[:, ks], st.astype(BF16)) + _dot(a.astype(BF16), v[:, vs]))
                st_sc[sq, hh] = st * decay[:, ks] + _dot_tn(v[:, vs], kd[:, ks])
            o_ref[sq, pl.ds(off, chunk), :] = jnp.concatenate(outs, axis=1)
        return 0

    lax.fori_loop(0, n_chunks, one_chunk, 0, unroll=min(GLA_UNROLL, n_chunks))
    for sq in range(n_seq):
        for hh in range(GLA_HEADS):
            s_ref[sq, hh] = st_sc[sq, hh].T


def _gla(gq, gk, gv, la, s0, chunk, seqs_per_step):
    b, t, hk = gq.shape
    hv = gv.shape[2]
    g = seqs_per_step
    blk = lambda w: pl.BlockSpec((g, t, w), lambda i: (i, 0, 0))
    st = pl.BlockSpec((g, GLA_HEADS, GLA_DK, GLA_DV), lambda i: (i, 0, 0, 0))
    body = functools.partial(_gla_body, n_seq=g, n_chunks=t // chunk, chunk=chunk)
    return pl.pallas_call(
        body,
        grid=(b // g,),
        in_specs=[blk(hk), blk(hk), blk(hv), blk(hk), st],
        out_specs=[blk(hv), st],
        out_shape=[jax.ShapeDtypeStruct((b, t, hv), F32),
                   jax.ShapeDtypeStruct((b, GLA_HEADS, GLA_DK, GLA_DV), F32)],
        scratch_shapes=[pltpu.VMEM((g, GLA_HEADS, GLA_DV, GLA_DK), F32)],
        compiler_params=_cparams(("parallel",)),
        name="gla",
    )(gq, gk, gv, la, s0)


def _mix_out_body(x_ref, mod_ref, om_ref, og_ref, gg_ref, go_ref, wlat_ref, wom_ref, wog_ref, o_ref,
                  *, d, sample):
    gate = mod_ref[:, 2 * d:3 * d]
    om = om_ref[...]
    if sample:
        om = _dot(om.astype(BF16), wlat_ref[...]).astype(BF16)
    parts = []
    for hh in range(GLA_HEADS):
        vs = slice(hh * GLA_DV, (hh + 1) * GLA_DV)
        parts.append(_rms(og_ref[:, vs], go_ref[...]) * _silu(gg_ref[:, vs]))
    og = jnp.concatenate(parts, axis=1).astype(BF16)
    mix = _dot(om, wom_ref[...]) + _dot(og, wog_ref[...])
    o_ref[...] = x_ref[...] + gate * mix


def _mix_out(x, mod, o_mla, o_gla, gg, g_gla_o, w_lat, w_o_mla, w_o_gla, sample, tm):
    n, d = x.shape
    hv = GLA_HEADS * GLA_DV
    tok = lambda w: pl.BlockSpec((tm, w), lambda i: (i, 0))
    body = functools.partial(_mix_out_body, d=d, sample=sample)
    return pl.pallas_call(
        body,
        grid=(n // tm,),
        in_specs=[tok(d), mod.spec(tm, d, 1), tok(o_mla.shape[1]), tok(hv), tok(hv),
                  _resident((1, GLA_DV)), _resident(w_lat.shape), _resident(w_o_mla.shape),
                  _resident(w_o_gla.shape)],
        out_specs=tok(d),
        out_shape=jax.ShapeDtypeStruct((n, d), F32),
        compiler_params=_cparams(("parallel",)),
        name="mix_out_sample" if sample else "mix_out_prompt",
    )(x, mod.arr, o_mla, o_gla, gg, g_gla_o.reshape(1, GLA_DV), w_lat, w_o_mla, w_o_gla)


def _prep_w_in(w_in):
    d = w_in.shape[0]
    o = 0
    parts = {}
    for name, size in (("qa", MLA_Q_LORA), ("kva", MLA_KV_LORA), ("kr", MLA_ROPE),
                       ("gq", GLA_HEADS * GLA_DK), ("gk", GLA_HEADS * GLA_DK), ("gv", GLA_HEADS * GLA_DV),
                       ("ga", GLA_GATE_RANK), ("gg", GLA_HEADS * GLA_DV)):
        parts[name] = w_in[:, o:o + size]
        o += size
    hr = MLA_ROPE // 2
    kr = parts["kr"]
    z = lambda w: jnp.zeros((d, w), w_in.dtype)
    blk_a = jnp.concatenate([kr, parts["ga"], z(LANE - MLA_ROPE - GLA_GATE_RANK)], axis=1)
    blk_b = jnp.concatenate([kr[:, hr:], kr[:, :hr], z(LANE - MLA_ROPE)], axis=1)
    w = jnp.concatenate([parts["qa"], parts["kva"], parts["gq"] * (GLA_DK ** -0.5), parts["gk"],
                         parts["gv"], parts["gg"], blk_a, blk_b], axis=1)
    return w.astype(BF16)


def _prep_wq(w_qb):
    ql = w_qb.shape[0]
    hr = MLA_ROPE // 2
    nope, r1, r2 = w_qb[..., :MLA_NOPE], w_qb[..., MLA_NOPE:MLA_NOPE + hr], w_qb[..., MLA_NOPE + hr:]
    zp = jnp.zeros((ql, MLA_HEADS, HEAD_PAD - MLA_NOPE - MLA_ROPE), w_qb.dtype)
    zn = jnp.zeros_like(nope)
    a = jnp.concatenate([nope, r1, r2, zp], axis=-1).reshape(ql, MLA_HEADS * HEAD_PAD)
    b = jnp.concatenate([zn, r2, r1, zp], axis=-1).reshape(ql, MLA_HEADS * HEAD_PAD)
    return (jnp.concatenate([a, b], axis=1) * (MLA_SCALE * LOG2_E)).astype(BF16)


def _prep_wkv_prompt(w_kvb):
    c = w_kvb.shape[0]
    rows = c + LANE
    k = jnp.zeros((rows, MLA_HEADS, HEAD_PAD), F32)
    k = k.at[:c, :, :MLA_NOPE].set(w_kvb[..., :MLA_NOPE])
    eye = jnp.broadcast_to(jnp.eye(MLA_ROPE, dtype=F32)[:, None, :], (MLA_ROPE, MLA_HEADS, MLA_ROPE))
    k = k.at[c:c + MLA_ROPE, :, MLA_NOPE:MLA_NOPE + MLA_ROPE].set(eye)
    v = jnp.zeros((rows, MLA_HEADS, MLA_V), F32)
    v = v.at[:c].set(w_kvb[..., MLA_NOPE:])
    return jnp.concatenate([k.reshape(rows, -1), v.reshape(rows, -1)], axis=1).astype(BF16)


def _prep_wq_sample(w_kvb):
    c = w_kvb.shape[0]
    lat = jnp.zeros((MLA_HEADS, HEAD_PAD, MLA_HEADS, c), F32)
    rope = jnp.zeros((MLA_HEADS, HEAD_PAD, MLA_HEADS, MLA_ROPE), F32)
    for hh in range(MLA_HEADS):
        lat = lat.at[hh, :MLA_NOPE, hh, :].set(w_kvb[:, hh, :MLA_NOPE].T)
        rope = rope.at[hh, MLA_NOPE:MLA_NOPE + MLA_ROPE, hh, :].set(jnp.eye(MLA_ROPE, dtype=F32))
    rows = MLA_HEADS * HEAD_PAD
    return jnp.concatenate([lat.reshape(rows, -1), rope.reshape(rows, -1)], axis=1).astype(BF16)


def _prep_w_lat_out(w_kvb):
    c = w_kvb.shape[0]
    w = jnp.zeros((MLA_HEADS, c, MLA_HEADS, MLA_V), F32)
    for hh in range(MLA_HEADS):
        w = w.at[hh, :, hh, :].set(w_kvb[:, hh, MLA_NOPE:])
    return w.reshape(MLA_HEADS * c, MLA_HEADS * MLA_V).astype(BF16)


def _prep_w_o(w_o):
    n_mla = MLA_HEADS * MLA_V
    return w_o[:n_mla].astype(BF16), w_o[n_mla:].astype(BF16)


def _prep_gate(w_gate_b):
    w = jnp.zeros((LANE, w_gate_b.shape[1]), F32)
    return w.at[MLA_ROPE:MLA_ROPE + GLA_GATE_RANK].set(w_gate_b).astype(BF16)


def _rope_table(pos):
    hr = MLA_ROPE // 2
    inv = ROPE_THETA ** (-jnp.arange(0, MLA_ROPE, 2, dtype=F32) / MLA_ROPE)
    ang = pos[:, None] * inv[None, :]
    cos, sin = jnp.cos(ang), jnp.sin(ang)
    t = pos.shape[0]
    z = lambda w: jnp.zeros((t, w), F32)
    pad_q = HEAD_PAD - MLA_NOPE - MLA_ROPE
    cq = jnp.concatenate([jnp.ones((t, MLA_NOPE), F32), cos, cos, z(pad_q)], axis=1)
    sq = jnp.concatenate([z(MLA_NOPE), -sin, sin, z(pad_q)], axis=1)
    ck = jnp.concatenate([cos, cos, z(LANE - 2 * hr)], axis=1)
    sk = jnp.concatenate([-sin, sin, z(LANE - 2 * hr)], axis=1)
    return jnp.concatenate([cq, sq, ck, sk], axis=1)


TM_TOKENS = 512
TQ_PROMPT = 512
PAGES_PER_CHUNK = 32
GLA_SAMPLE_PAD = 16
GLA_SAMPLE_SEQS = 8
GLA_UNROLL = 8


def kernel(x_prompt, x_sample, cache_ckv, cache_krope, state_gla, page_table, c_prompt, c_sample,
           w_ada, b_ada, norm_ffn1, ffn1_w1, ffn1_w3, ffn1_w2, norm_mix, w_in, g_qa, w_qb, g_kva, w_kvb,
           w_gate_b, b_gate, g_gla_o, w_o, norm_ffn2, ffn2_w1, ffn2_w3, ffn2_w2, norm_final):
    bp, tp, d = x_prompt.shape
    bs, ts, _ = x_sample.shape
    depth = w_ada.shape[0]
    n_pages = page_table.shape[1]
    past_len = n_pages * PAGE_SIZE

    tab_p = _rope_table(jnp.arange(tp, dtype=F32))
    tm_s = min(TM_TOKENS, bs * ts)
    tab_s = jnp.tile(_rope_table(past_len + jnp.arange(ts, dtype=F32)), (tm_s // ts, 1))

    hp = x_prompt.reshape(bp * tp, d)
    hs = x_sample.reshape(bs * ts, d)
    outs = [[] for _ in range(6)]
    for l in range(depth):
        m = _ada(jnp.concatenate([c_prompt, c_sample], axis=0), w_ada[l], b_ada[l])
        mod_p = _Mod(m[:bp], False, tp)
        mod_s = _Mod(jnp.repeat(m[bp:], ts, axis=0), True, ts)

        f1 = (norm_ffn1[l], ffn1_w1[l].astype(BF16), ffn1_w3[l].astype(BF16), ffn1_w2[l].astype(BF16))
        f2 = (norm_ffn2[l], ffn2_w1[l].astype(BF16), ffn2_w3[l].astype(BF16), ffn2_w2[l].astype(BF16))
        w_in_r = _prep_w_in(w_in[l])
        wq = _prep_wq(w_qb[l])
        wkv_p = _prep_wkv_prompt(w_kvb[l])
        wq_s = _prep_wq_sample(w_kvb[l])
        w_lat = _prep_w_lat_out(w_kvb[l])
        w_o_mla, w_o_gla = _prep_w_o(w_o[l])
        wgate = _prep_gate(w_gate_b[l])
        last = l == depth - 1

        hp = _ffn(hp, mod_p, 0, *f1, norm_final, False, TM_TOKENS)
        q, k, v, ckv_p, kr_p, gq, gk, gv, la, gg = _mix_in(
            hp, mod_p, norm_mix[l], w_in_r, g_qa[l], wq, g_kva[l], wkv_p, wgate, b_gate[l], tab_p,
            False, TM_TOKENS, bp)
        o_mla = _attn_prompt(q, k, v, TQ_PROMPT).reshape(bp * tp, -1)
        seq3 = lambda a: a.reshape(bp, tp, -1)
        s0_p = jnp.zeros((bp, GLA_HEADS, GLA_DK, GLA_DV), F32)
        o_gla, s_p = _gla(seq3(gq), seq3(gk), seq3(gv), seq3(la), s0_p, GLA_CHUNK, 1)
        hp = _mix_out(hp, mod_p, o_mla, o_gla.reshape(bp * tp, -1), gg, g_gla_o[l], w_lat, w_o_mla,
                      w_o_gla, False, TM_TOKENS)
        hp = _ffn(hp, mod_p, 2, *f2, norm_final, last, TM_TOKENS)

        hs = _ffn(hs, mod_s, 0, *f1, norm_final, False, tm_s)
        ql, ckv_s, kr_s, gq, gk, gv, la, gg = _mix_in(
            hs, mod_s, norm_mix[l], w_in_r, g_qa[l], wq, g_kva[l], wq_s, wgate, b_gate[l], tab_s,
            True, tm_s, bs)
        n_lat = MLA_HEADS * MLA_KV_LORA
        rows = ts * MLA_HEADS
        q_lat = ql[:, :n_lat].reshape(bs, rows, MLA_KV_LORA)
        q_rope = ql[:, n_lat:].reshape(bs, rows, MLA_ROPE)
        o_lat = _attn_sample(page_table, q_lat, q_rope, ckv_s.reshape(bs, ts, -1), kr_s.reshape(bs, ts, -1),
                             cache_ckv, cache_krope, l, PAGES_PER_CHUNK)
        pad = lambda a: jnp.pad(a.reshape(bs, ts, -1), ((0, 0), (0, GLA_SAMPLE_PAD - ts), (0, 0)))
        o_gla, s_s = _gla(pad(gq), pad(gk), pad(gv), pad(la), state_gla[l], GLA_SAMPLE_PAD, GLA_SAMPLE_SEQS)
        hs = _mix_out(hs, mod_s, o_lat.reshape(bs * ts, n_lat), o_gla[:, :ts].reshape(bs * ts, -1), gg,
                      g_gla_o[l], w_lat, w_o_mla, w_o_gla, True, tm_s)
        hs = _ffn(hs, mod_s, 2, *f2, norm_final, last, tm_s)

        for lst, a in zip(outs, (ckv_p.reshape(bp, tp, -1), kr_p.reshape(bp, tp, -1), s_p,
                                 ckv_s.reshape(bs, ts, -1), kr_s.reshape(bs, ts, -1), s_s)):
            lst.append(a)

    return (hp.reshape(bp, tp, d), hs.reshape(bs, ts, d)) + tuple(jnp.stack(o) for o in outs)
```

```python
import functools
import math

import jax
import jax.numpy as jnp
from jax import lax
from jax.experimental import pallas as pl
from jax.experimental.pallas import tpu as pltpu

F32 = jnp.float32
BF16 = jnp.bfloat16

PAGE_SIZE = 128
MLA_HEADS = 8
MLA_NOPE = 64
MLA_ROPE = 32
MLA_V = 64
MLA_Q_LORA = 384
MLA_KV_LORA = 256
MLA_SCALE = (MLA_NOPE + MLA_ROPE) ** -0.5
LOG2_E = math.log2(math.e)
ROPE_THETA = 10000.0
GLA_HEADS = 4
GLA_DK = 64
GLA_DV = 128
GLA_GATE_RANK = 16
GLA_GATE_TAU = 16.0
GLA_CHUNK = 64
N_ADA = 9
EPS = 1e-6

LANE = 128
HEAD_PAD = 128
VMEM_LIMIT_BYTES = 56 * 1024 * 1024

C_QA = 0
C_KVA = C_QA + MLA_Q_LORA
C_GQ = C_KVA + MLA_KV_LORA
C_GK = C_GQ + GLA_HEADS * GLA_DK
C_GV = C_GK + GLA_HEADS * GLA_DK
C_GG = C_GV + GLA_HEADS * GLA_DV
C_BLKA = C_GG + GLA_HEADS * GLA_DV
C_BLKB = C_BLKA + LANE
C_END = C_BLKB + LANE


def _cparams(sem):
    return pltpu.CompilerParams(dimension_semantics=sem, vmem_limit_bytes=VMEM_LIMIT_BYTES)


def _resident(shape):
    return pl.BlockSpec(shape, lambda *_: (0,) * len(shape), pipeline_mode=pl.Buffered(1))


def _rms(x, g):
    return x * lax.rsqrt(jnp.mean(x * x, axis=-1, keepdims=True) + EPS) * g


def _silu(x):
    return x * jax.nn.sigmoid(x)


def _dot(a, b):
    return jnp.dot(a, b, preferred_element_type=F32)


def _dot_nt(a, b):
    return lax.dot_general(a, b, (((1,), (1,)), ((), ())), preferred_element_type=F32)


def _dot_tn(a, b):
    return lax.dot_general(a, b, (((0,), (0,)), ((), ())), preferred_element_type=F32)


def _ada_body(c_ref, w_ref, b_ref, o_ref):
    c = _silu(c_ref[...]).astype(BF16)
    o_ref[...] = _dot(c, w_ref[...].astype(BF16)) + b_ref[...]


def _ada(c_all, w_ada, b_ada):
    n, d = c_all.shape
    n_out = w_ada.shape[1]
    tn = 1024
    return pl.pallas_call(
        _ada_body,
        grid=(n_out // tn,),
        in_specs=[pl.BlockSpec((n, d), lambda j: (0, 0)),
                  pl.BlockSpec((d, tn), lambda j: (0, j)),
                  pl.BlockSpec((1, tn), lambda j: (0, j))],
        out_specs=pl.BlockSpec((n, tn), lambda j: (0, j)),
        out_shape=jax.ShapeDtypeStruct((n, n_out), F32),
        compiler_params=_cparams(("arbitrary",)),
        name="ada",
    )(c_all, w_ada, b_ada.reshape(1, n_out))


class _Mod:
    def __init__(self, m, per_token, tokens_per_seq):
        self.per_token = per_token
        self.tokens_per_seq = tokens_per_seq
        self.arr = m if per_token else m.reshape(m.shape[0], 1, m.shape[1])

    def spec(self, tm, d, third):
        if self.per_token:
            return pl.BlockSpec((tm, 3 * d), lambda i: (i, third))
        per = self.tokens_per_seq // tm
        return pl.BlockSpec((None, 1, 3 * d), lambda i: (i // per, 0, third))


def _ffn_body(x_ref, mod_ref, g_ref, w1_ref, w3_ref, w2_ref, gf_ref, o_ref, *, d, ff_chunks, final):
    x = x_ref[...]
    shift, scale, gate = mod_ref[:, 0:d], mod_ref[:, d:2 * d], mod_ref[:, 2 * d:3 * d]
    h = (_rms(x, g_ref[...]) * (1.0 + scale) + shift).astype(BF16)
    y = None
    for c0, c1 in ff_chunks:
        a = _dot(h, w1_ref[:, c0:c1])
        b = _dot(h, w3_ref[:, c0:c1])
        u = (_silu(a) * b).astype(BF16)
        part = _dot(u, w2_ref[c0:c1, :])
        y = part if y is None else y + part
    out = x + 0.5 * gate * y
    if final:
        out = _rms(out, gf_ref[...])
    o_ref[...] = out


def _ffn(x, mod, third, g, w1, w3, w2, g_final, final, tm):
    n, d = x.shape
    ff = w1.shape[1]
    step = 1024
    ff_chunks = tuple((c, min(c + step, ff)) for c in range(0, ff, step))
    body = functools.partial(_ffn_body, d=d, ff_chunks=ff_chunks, final=final)
    return pl.pallas_call(
        body,
        grid=(n // tm,),
        in_specs=[pl.BlockSpec((tm, d), lambda i: (i, 0)),
                  mod.spec(tm, d, third),
                  _resident((1, d)), _resident((d, ff)), _resident((d, ff)), _resident((ff, d)),
                  _resident((1, d))],
        out_specs=pl.BlockSpec((tm, d), lambda i: (i, 0)),
        out_shape=jax.ShapeDtypeStruct((n, d), F32),
        compiler_params=_cparams(("parallel",)),
        name="ffn_final" if final else "ffn",
    )(x, mod.arr, g.reshape(1, d), w1, w3, w2, g_final.reshape(1, d))


def _mix_in_body(x_ref, mod_ref, g_ref, win_ref, gqa_ref, wq_ref, gkva_ref, wkv_ref, wgate_ref,
                 bgate_ref, tab_ref, *out_refs, d, sample):
    if sample:
        (ql_ref, ckv_ref, kr_ref, gq_ref, gk_ref, gv_ref, la_ref, gg_ref) = out_refs
    else:
        (q_ref, k_ref, v_ref, ckv_ref, kr_ref, gq_ref, gk_ref, gv_ref, la_ref, gg_ref) = out_refs
    x = x_ref[...]
    shift, scale = mod_ref[:, 0:d], mod_ref[:, d:2 * d]
    h = (_rms(x, g_ref[...]) * (1.0 + scale) + shift).astype(BF16)
    proj = _dot(h, win_ref[...])

    cq, sq, ck, sk = (tab_ref[:, i * LANE:(i + 1) * LANE] for i in range(4))

    qn = _rms(proj[:, C_QA:C_KVA], gqa_ref[...]).astype(BF16)
    qq = _dot(qn, wq_ref[...])
    half = MLA_HEADS * HEAD_PAD
    q_heads = [qq[:, hh * HEAD_PAD:(hh + 1) * HEAD_PAD] * cq
               + qq[:, half + hh * HEAD_PAD:half + (hh + 1) * HEAD_PAD] * sq
               for hh in range(MLA_HEADS)]

    ckv = _rms(proj[:, C_KVA:C_GQ], gkva_ref[...])
    ckv_ref[...] = ckv
    blk_a = proj[:, C_BLKA:C_BLKB]
    kr_blk = blk_a * ck + proj[:, C_BLKB:C_END] * sk
    kr_ref[...] = kr_blk[:, 0:MLA_ROPE]

    if sample:
        q_all = jnp.concatenate(q_heads, axis=1).astype(BF16)
        ql_ref[...] = _dot(q_all, wkv_ref[...]).astype(BF16)
    else:
        for hh in range(MLA_HEADS):
            q_ref[hh] = q_heads[hh].astype(BF16)
        lhs = jnp.concatenate([ckv.astype(BF16), kr_blk.astype(BF16)], axis=1)
        kv = _dot(lhs, wkv_ref[...])
        for hh in range(MLA_HEADS):
            k_ref[hh] = kv[:, hh * HEAD_PAD:(hh + 1) * HEAD_PAD].astype(BF16)
        for pr in range(MLA_HEADS * MLA_V // LANE):
            v_ref[pr] = kv[:, half + pr * LANE:half + (pr + 1) * LANE].astype(BF16)

    gq_ref[...] = proj[:, C_GQ:C_GK]
    gk_ref[...] = proj[:, C_GK:C_GV]
    gv_ref[...] = proj[:, C_GV:C_GG]
    gg_ref[...] = proj[:, C_GG:C_BLKA]
    z = _dot(blk_a.astype(BF16), wgate_ref[...]) + bgate_ref[...]
    la_ref[...] = (jnp.minimum(z, 0.0) - jnp.log1p(jnp.exp(-jnp.abs(z)))) / GLA_GATE_TAU


def _mix_in(x, mod, g, w_in_r, g_qa, wq, g_kva, wkv, wgate, b_gate, tab, sample, tm, n_seq):
    n, d = x.shape
    t_seq = n // n_seq
    n_pos_tiles = tab.shape[0] // tm
    hk = GLA_HEADS * GLA_DK
    hv = GLA_HEADS * GLA_DV
    tok = lambda w: pl.BlockSpec((tm, w), lambda i: (i, 0))
    tok_shape = lambda w, dt=F32: jax.ShapeDtypeStruct((n, w), dt)
    gla_specs = [tok(hk), tok(hk), tok(hv), tok(hk), tok(hv)]
    gla_shapes = [tok_shape(hk), tok_shape(hk), tok_shape(hv), tok_shape(hk), tok_shape(hv)]
    if sample:
        wl = wkv.shape[1]
        out_specs = [tok(wl), tok(MLA_KV_LORA), tok(MLA_ROPE)] + gla_specs
        out_shape = [tok_shape(wl, BF16), tok_shape(MLA_KV_LORA), tok_shape(MLA_ROPE)] + gla_shapes
    else:
        per = t_seq // tm
        n_vp = MLA_HEADS * MLA_V // LANE
        head_spec = lambda nh: pl.BlockSpec((None, nh, tm, HEAD_PAD), lambda i: (i // per, 0, i % per, 0))
        head_shape = lambda nh: jax.ShapeDtypeStruct((n_seq, nh, t_seq, HEAD_PAD), BF16)
        out_specs = ([head_spec(MLA_HEADS)] * 2 + [head_spec(n_vp), tok(MLA_KV_LORA), tok(MLA_ROPE)]
                     + gla_specs)
        out_shape = ([head_shape(MLA_HEADS)] * 2 + [head_shape(n_vp), tok_shape(MLA_KV_LORA),
                                                    tok_shape(MLA_ROPE)] + gla_shapes)
    body = functools.partial(_mix_in_body, d=d, sample=sample)
    return pl.pallas_call(
        body,
        grid=(n // tm,),
        in_specs=[pl.BlockSpec((tm, d), lambda i: (i, 0)),
                  mod.spec(tm, d, 1),
                  _resident((1, d)), _resident(w_in_r.shape), _resident((1, MLA_Q_LORA)),
                  _resident(wq.shape), _resident((1, MLA_KV_LORA)), _resident(wkv.shape),
                  _resident(wgate.shape), _resident((1, hk)),
                  pl.BlockSpec((tm, 4 * LANE), lambda i: (i % n_pos_tiles, 0))],
        out_specs=out_specs,
        out_shape=out_shape,
        compiler_params=_cparams(("parallel",)),
        name="mix_in_sample" if sample else "mix_in_prompt",
    )(x, mod.arr, g.reshape(1, d), w_in_r, g_qa.reshape(1, -1), wq, g_kva.reshape(1, -1), wkv,
      wgate, b_gate.reshape(1, hk), tab)


def _attn_prompt_body(q_ref, k_ref, v_ref, o_ref, *, tq):
    n_heads = 2
    i = pl.program_id(2)
    qs = [q_ref[hh] for hh in range(n_heads)]

    def scores(hh, j):
        off = pl.multiple_of(j * tq, tq)
        return _dot_nt(qs[hh], k_ref[hh, pl.ds(off, tq), :]), v_ref[pl.ds(off, tq), :]

    init = []
    for hh in range(n_heads):
        s, v = scores(hh, i)
        row = lax.broadcasted_iota(jnp.int32, s.shape, 0)
        col = lax.broadcasted_iota(jnp.int32, s.shape, 1)
        s = jnp.where(col <= row, s, jnp.finfo(F32).min)
        m = jnp.max(s, axis=-1, keepdims=True)
        p = jnp.exp2(s - m)
        init += [m, jnp.sum(p, axis=-1, keepdims=True), _dot(p.astype(BF16), v)]

    def step(j, carry):
        out = []
        for hh in range(n_heads):
            m, l, acc = carry[3 * hh:3 * hh + 3]
            s, v = scores(hh, j)
            m_new = jnp.maximum(m, jnp.max(s, axis=-1, keepdims=True))
            alpha = jnp.exp2(m - m_new)
            p = jnp.exp2(s - m_new)
            out += [m_new, alpha * l + jnp.sum(p, axis=-1, keepdims=True),
                    alpha * acc + _dot(p.astype(BF16), v)]
        return tuple(out)

    fin = lax.fori_loop(0, i, step, tuple(init))
    o_even, o_odd = fin[2] / fin[1], fin[5] / fin[4]
    lane = lax.broadcasted_iota(jnp.int32, o_even.shape, 1)
    o_ref[...] = jnp.where(lane < MLA_V, o_even, o_odd).astype(o_ref.dtype)


def _attn_prompt(q, k, v, tq):
    b, hh, t, w = q.shape
    body = functools.partial(_attn_prompt_body, tq=tq)
    return pl.pallas_call(
        body,
        grid=(b, hh // 2, t // tq),
        in_specs=[pl.BlockSpec((None, 2, tq, w), lambda bi, hi, i: (bi, hi, i, 0)),
                  pl.BlockSpec((None, 2, t, w), lambda bi, hi, i: (bi, hi, 0, 0)),
                  pl.BlockSpec((None, None, t, w), lambda bi, hi, i: (bi, hi, 0, 0))],
        out_specs=pl.BlockSpec((None, tq, w), lambda bi, hi, i: (bi, i, hi)),
        out_shape=jax.ShapeDtypeStruct((b, t, hh // 2 * w), BF16),
        compiler_params=_cparams(("parallel", "parallel", "arbitrary")),
        name="attn_prompt",
    )(q, k, v)


def _attn_sample_body(pt_ref, ql_ref, qr_ref, cn_ref, kn_ref, ckv_hbm, kr_hbm, o_ref, ckv_buf, kr_buf, sems,
                      *, layer, n_chunks, pg):
    b = pl.program_id(0)
    n_seq = pl.num_programs(0)

    def copies(seq, c):
        out = []
        for p in range(pg):
            page = pt_ref[seq, c * pg + p]
            rows = pl.ds(p * PAGE_SIZE, PAGE_SIZE)
            out.append(pltpu.make_async_copy(ckv_hbm.at[layer, page], ckv_buf.at[c, rows, :], sems.at[c, 0]))
            out.append(pltpu.make_async_copy(kr_hbm.at[layer, page], kr_buf.at[c, :, rows], sems.at[c, 1]))
        return out

    @pl.when(b == 0)
    def _():
        for c in range(n_chunks):
            for cp in copies(0, c):
                cp.start()

    ql = ql_ref[...]
    qr = qr_ref[...]

    cn = cn_ref[...]
    s = _dot_nt(ql.astype(F32), cn) + _dot_nt(qr.astype(F32), kn_ref[...])
    row_tok = lax.broadcasted_iota(jnp.int32, s.shape, 0) // MLA_HEADS
    col = lax.broadcasted_iota(jnp.int32, s.shape, 1)
    s = jnp.where(col <= row_tok, s, jnp.finfo(F32).min)
    m = jnp.max(s, axis=-1, keepdims=True)
    p = jnp.exp2(s - m)
    l = jnp.sum(p, axis=-1, keepdims=True)
    acc = _dot(p, cn)

    def scores(c):
        for cp in copies(b, c):
            cp.wait()
        ck = ckv_buf[c].astype(BF16)
        return _dot_nt(ql, ck) + _dot(qr, kr_buf[c].astype(BF16)), ck

    nxt = jnp.minimum(b + 1, n_seq - 1)
    s_next, ck_next = scores(0)
    for c in range(n_chunks):
        s, ck = s_next, ck_next
        if c + 1 < n_chunks:
            s_next, ck_next = scores(c + 1)
        m_new = jnp.maximum(m, jnp.max(s, axis=-1, keepdims=True))
        alpha = jnp.exp2(m - m_new)
        p = jnp.exp2(s - m_new)
        l = alpha * l + jnp.sum(p, axis=-1, keepdims=True)
        acc = alpha * acc + _dot(p.astype(BF16), ck)
        m = m_new
        for cp in copies(nxt, c):
            cp.start()
    o_ref[...] = acc / l

    @pl.when(b == n_seq - 1)
    def _():
        for c in range(n_chunks):
            for cp in copies(nxt, c):
                cp.wait()


def _attn_sample(page_table, q_lat, q_rope, ckv_new, kr_new, cache_ckv, cache_krope_t, layer, pg):
    n_seq, rows, c = q_lat.shape
    t_new = ckv_new.shape[1]
    r = q_rope.shape[2]
    n_chunks = page_table.shape[1] // pg
    seq = lambda shape: pl.BlockSpec((None,) + shape, lambda b, pt: (b, 0, 0))
    body = functools.partial(_attn_sample_body, layer=layer, n_chunks=n_chunks, pg=pg)
    grid_spec = pltpu.PrefetchScalarGridSpec(
        num_scalar_prefetch=1,
        grid=(n_seq,),
        in_specs=[seq((rows, c)), seq((rows, r)), seq((t_new, c)), seq((t_new, r)),
                  pl.BlockSpec(memory_space=pl.ANY), pl.BlockSpec(memory_space=pl.ANY)],
        out_specs=seq((rows, c)),
        scratch_shapes=[pltpu.VMEM((n_chunks, pg * PAGE_SIZE, c), F32),
                        pltpu.VMEM((n_chunks, r, pg * PAGE_SIZE), F32),
                        pltpu.SemaphoreType.DMA((n_chunks, 2))],
    )
    return pl.pallas_call(
        body,
        grid_spec=grid_spec,
        out_shape=jax.ShapeDtypeStruct((n_seq, rows, c), F32),
        compiler_params=_cparams(("arbitrary",)),
        name="attn_sample",
    )(page_table, q_lat, q_rope, ckv_new, kr_new, cache_ckv, cache_krope_t)


def _gla_body(q_ref, k_ref, v_ref, la_ref, s0_ref, o_ref, s_ref, st_sc, *, n_seq, n_chunks, chunk):
    for sq in range(n_seq):
        for hh in range(GLA_HEADS):
            st_sc[sq, hh] = s0_ref[sq, hh].T
    r = lax.broadcasted_iota(jnp.int32, (chunk, chunk), 0)
    c = lax.broadcasted_iota(jnp.int32, (chunk, chunk), 1)
    causal = c <= r
    tri = causal.astype(BF16)

    def one_chunk(ci, _):
        off = pl.multiple_of(ci * chunk, chunk)
        for sq in range(n_seq):
            g = la_ref[sq, pl.ds(off, chunk), :]
            g_hi = g.astype(BF16)
            g_lo = (g - g_hi.astype(F32)).astype(BF16)
            b = _dot(tri, g_hi) + _dot(tri, g_lo)
            b_last = b[chunk - 1:chunk, :]
            q = q_ref[sq, pl.ds(off, chunk), :]
            k = k_ref[sq, pl.ds(off, chunk), :]
            v = v_ref[sq, pl.ds(off, chunk), :].astype(BF16)
            qe = (q * jnp.exp(b)).astype(BF16)
            ke = (k * jnp.exp(-b)).astype(BF16)
            kd = (k * jnp.exp(b_last - b)).astype(BF16)
            decay = jnp.exp(b_last)
            outs = []
            for hh in range(GLA_HEADS):
                ks = slice(hh * GLA_DK, (hh + 1) * GLA_DK)
                vs = slice(hh * GLA_DV, (hh + 1) * GLA_DV)
                st = st_sc[sq, hh]
                a = jnp.where(causal, _dot_nt(qe[:, ks], ke[:, ks]), 0.0)
                outs.append(_dot_nt(qe[:, ks], st.astype(BF16)) + _dot(a.astype(BF16), v[:, vs]))
                st_sc[sq, hh] = st * decay[:, ks] + _dot_tn(v[:, vs], kd[:, ks])
            o_ref[sq, pl.ds(off, chunk), :] = jnp.concatenate(outs, axis=1)
        return 0

    lax.fori_loop(0, n_chunks, one_chunk, 0, unroll=min(GLA_UNROLL, n_chunks))
    for sq in range(n_seq):
        for hh in range(GLA_HEADS):
            s_ref[sq, hh] = st_sc[sq, hh].T


def _gla(gq, gk, gv, la, s0, chunk, seqs_per_step):
    b, t, hk = gq.shape
    hv = gv.shape[2]
    g = seqs_per_step
    blk = lambda w: pl.BlockSpec((g, t, w), lambda i: (i, 0, 0))
    st = pl.BlockSpec((g, GLA_HEADS, GLA_DK, GLA_DV), lambda i: (i, 0, 0, 0))
    body = functools.partial(_gla_body, n_seq=g, n_chunks=t // chunk, chunk=chunk)
    return pl.pallas_call(
        body,
        grid=(b // g,),
        in_specs=[blk(hk), blk(hk), blk(hv), blk(hk), st],
        out_specs=[blk(hv), st],
        out_shape=[jax.ShapeDtypeStruct((b, t, hv), F32),
                   jax.ShapeDtypeStruct((b, GLA_HEADS, GLA_DK, GLA_DV), F32)],
        scratch_shapes=[pltpu.VMEM((g, GLA_HEADS, GLA_DV, GLA_DK), F32)],
        compiler_params=_cparams(("parallel",)),
        name="gla",
    )(gq, gk, gv, la, s0)


def _mix_out_body(x_ref, mod_ref, om_ref, og_ref, gg_ref, go_ref, wlat_ref, wom_ref, wog_ref, o_ref,
                  *, d, sample):
    gate = mod_ref[:, 2 * d:3 * d]
    om = om_ref[...]
    if sample:
        om = _dot(om.astype(BF16), wlat_ref[...]).astype(BF16)
    parts = []
    for hh in range(GLA_HEADS):
        vs = slice(hh * GLA_DV, (hh + 1) * GLA_DV)
        parts.append(_rms(og_ref[:, vs], go_ref[...]) * _silu(gg_ref[:, vs]))
    og = jnp.concatenate(parts, axis=1).astype(BF16)
    mix = _dot(om, wom_ref[...]) + _dot(og, wog_ref[...])
    o_ref[...] = x_ref[...] + gate * mix


def _mix_out(x, mod, o_mla, o_gla, gg, g_gla_o, w_lat, w_o_mla, w_o_gla, sample, tm):
    n, d = x.shape
    hv = GLA_HEADS * GLA_DV
    tok = lambda w: pl.BlockSpec((tm, w), lambda i: (i, 0))
    body = functools.partial(_mix_out_body, d=d, sample=sample)
    return pl.pallas_call(
        body,
        grid=(n // tm,),
        in_specs=[tok(d), mod.spec(tm, d, 1), tok(o_mla.shape[1]), tok(hv), tok(hv),
                  _resident((1, GLA_DV)), _resident(w_lat.shape), _resident(w_o_mla.shape),
                  _resident(w_o_gla.shape)],
        out_specs=tok(d),
        out_shape=jax.ShapeDtypeStruct((n, d), F32),
        compiler_params=_cparams(("parallel",)),
        name="mix_out_sample" if sample else "mix_out_prompt",
    )(x, mod.arr, o_mla, o_gla, gg, g_gla_o.reshape(1, GLA_DV), w_lat, w_o_mla, w_o_gla)


def _prep_w_in(w_in):
    d = w_in.shape[0]
    o = 0
    parts = {}
    for name, size in (("qa", MLA_Q_LORA), ("kva", MLA_KV_LORA), ("kr", MLA_ROPE),
                       ("gq", GLA_HEADS * GLA_DK), ("gk", GLA_HEADS * GLA_DK), ("gv", GLA_HEADS * GLA_DV),
                       ("ga", GLA_GATE_RANK), ("gg", GLA_HEADS * GLA_DV)):
        parts[name] = w_in[:, o:o + size]
        o += size
    hr = MLA_ROPE // 2
    kr = parts["kr"]
    z = lambda w: jnp.zeros((d, w), w_in.dtype)
    blk_a = jnp.concatenate([kr, parts["ga"], z(LANE - MLA_ROPE - GLA_GATE_RANK)], axis=1)
    blk_b = jnp.concatenate([kr[:, hr:], kr[:, :hr], z(LANE - MLA_ROPE)], axis=1)
    w = jnp.concatenate([parts["qa"], parts["kva"], parts["gq"] * (GLA_DK ** -0.5), parts["gk"],
                         parts["gv"], parts["gg"], blk_a, blk_b], axis=1)
    return w.astype(BF16)


def _prep_wq(w_qb):
    ql = w_qb.shape[0]
    hr = MLA_ROPE // 2
    nope, r1, r2 = w_qb[..., :MLA_NOPE], w_qb[..., MLA_NOPE:MLA_NOPE + hr], w_qb[..., MLA_NOPE + hr:]
    zp = jnp.zeros((ql, MLA_HEADS, HEAD_PAD - MLA_NOPE - MLA_ROPE), w_qb.dtype)
    zn = jnp.zeros_like(nope)
    a = jnp.concatenate([nope, r1, r2, zp], axis=-1).reshape(ql, MLA_HEADS * HEAD_PAD)
    b = jnp.concatenate([zn, r2, r1, zp], axis=-1).reshape(ql, MLA_HEADS * HEAD_PAD)
    return (jnp.concatenate([a, b], axis=1) * (MLA_SCALE * LOG2_E)).astype(BF16)


def _prep_wkv_prompt(w_kvb):
    c = w_kvb.shape[0]
    rows = c + LANE
    k = jnp.zeros((rows, MLA_HEADS, HEAD_PAD), F32)
    k = k.at[:c, :, :MLA_NOPE].set(w_kvb[..., :MLA_NOPE])
    eye = jnp.broadcast_to(jnp.eye(MLA_ROPE, dtype=F32)[:, None, :], (MLA_ROPE, MLA_HEADS, MLA_ROPE))
    k = k.at[c:c + MLA_ROPE, :, MLA_NOPE:MLA_NOPE + MLA_ROPE].set(eye)
    v = jnp.zeros((rows, MLA_HEADS, MLA_V), F32)
    v = v.at[:c].set(w_kvb[..., MLA_NOPE:])
    return jnp.concatenate([k.reshape(rows, -1), v.reshape(rows, -1)], axis=1).astype(BF16)


def _prep_wq_sample(w_kvb):
    c = w_kvb.shape[0]
    lat = jnp.zeros((MLA_HEADS, HEAD_PAD, MLA_HEADS, c), F32)
    rope = jnp.zeros((MLA_HEADS, HEAD_PAD, MLA_HEADS, MLA_ROPE), F32)
    for hh in range(MLA_HEADS):
        lat = lat.at[hh, :MLA_NOPE, hh, :].set(w_kvb[:, hh, :MLA_NOPE].T)
        rope = rope.at[hh, MLA_NOPE:MLA_NOPE + MLA_ROPE, hh, :].set(jnp.eye(MLA_ROPE, dtype=F32))
    rows = MLA_HEADS * HEAD_PAD
    return jnp.concatenate([lat.reshape(rows, -1), rope.reshape(rows, -1)], axis=1).astype(BF16)


def _prep_w_lat_out(w_kvb):
    c = w_kvb.shape[0]
    w = jnp.zeros((MLA_HEADS, c, MLA_HEADS, MLA_V), F32)
    for hh in range(MLA_HEADS):
        w = w.at[hh, :, hh, :].set(w_kvb[:, hh, MLA_NOPE:])
    return w.reshape(MLA_HEADS * c, MLA_HEADS * MLA_V).astype(BF16)


def _prep_w_o(w_o):
    n_mla = MLA_HEADS * MLA_V
    return w_o[:n_mla].astype(BF16), w_o[n_mla:].astype(BF16)


def _prep_gate(w_gate_b):
    w = jnp.zeros((LANE, w_gate_b.shape[1]), F32)
    return w.at[MLA_ROPE:MLA_ROPE + GLA_GATE_RANK].set(w_gate_b).astype(BF16)


def _rope_table(pos):
    hr = MLA_ROPE // 2
    inv = ROPE_THETA ** (-jnp.arange(0, MLA_ROPE, 2, dtype=F32) / MLA_ROPE)
    ang = pos[:, None] * inv[None, :]
    cos, sin = jnp.cos(ang), jnp.sin(ang)
    t = pos.shape[0]
    z = lambda w: jnp.zeros((t, w), F32)
    pad_q = HEAD_PAD - MLA_NOPE - MLA_ROPE
    cq = jnp.concatenate([jnp.ones((t, MLA_NOPE), F32), cos, cos, z(pad_q)], axis=1)
    sq = jnp.concatenate([z(MLA_NOPE), -sin, sin, z(pad_q)], axis=1)
    ck = jnp.concatenate([cos, cos, z(LANE - 2 * hr)], axis=1)
    sk = jnp.concatenate([-sin, sin, z(LANE - 2 * hr)], axis=1)
    return jnp.concatenate([cq, sq, ck, sk], axis=1)


TM_TOKENS = 512
TQ_PROMPT = 512
PAGES_PER_CHUNK = 32
GLA_SAMPLE_PAD = 16
GLA_SAMPLE_SEQS = 8
GLA_UNROLL = 8


def kernel(x_prompt, x_sample, cache_ckv, cache_krope, state_gla, page_table, c_prompt, c_sample,
           w_ada, b_ada, norm_ffn1, ffn1_w1, ffn1_w3, ffn1_w2, norm_mix, w_in, g_qa, w_qb, g_kva, w_kvb,
           w_gate_b, b_gate, g_gla_o, w_o, norm_ffn2, ffn2_w1, ffn2_w3, ffn2_w2, norm_final):
    bp, tp, d = x_prompt.shape
    bs, ts, _ = x_sample.shape
    depth = w_ada.shape[0]
    n_pages = page_table.shape[1]
    past_len = n_pages * PAGE_SIZE

    tab_p = _rope_table(jnp.arange(tp, dtype=F32))
    tm_s = min(TM_TOKENS, bs * ts)
    tab_s = jnp.tile(_rope_table(past_len + jnp.arange(ts, dtype=F32)), (tm_s // ts, 1))

    hp = x_prompt.reshape(bp * tp, d)
    hs = x_sample.reshape(bs * ts, d)
    outs = [[] for _ in range(6)]
    for l in range(depth):
        m = _ada(jnp.concatenate([c_prompt, c_sample], axis=0), w_ada[l], b_ada[l])
        mod_p = _Mod(m[:bp], False, tp)
        mod_s = _Mod(jnp.repeat(m[bp:], ts, axis=0), True, ts)

        f1 = (norm_ffn1[l], ffn1_w1[l].astype(BF16), ffn1_w3[l].astype(BF16), ffn1_w2[l].astype(BF16))
        f2 = (norm_ffn2[l], ffn2_w1[l].astype(BF16), ffn2_w3[l].astype(BF16), ffn2_w2[l].astype(BF16))
        w_in_r = _prep_w_in(w_in[l])
        wq = _prep_wq(w_qb[l])
        wkv_p = _prep_wkv_prompt(w_kvb[l])
        wq_s = _prep_wq_sample(w_kvb[l])
        w_lat = _prep_w_lat_out(w_kvb[l])
        w_o_mla, w_o_gla = _prep_w_o(w_o[l])
        wgate = _prep_gate(w_gate_b[l])
        last = l == depth - 1

        hp = _ffn(hp, mod_p, 0, *f1, norm_final, False, TM_TOKENS)
        q, k, v, ckv_p, kr_p, gq, gk, gv, la, gg = _mix_in(
            hp, mod_p, norm_mix[l], w_in_r, g_qa[l], wq, g_kva[l], wkv_p, wgate, b_gate[l], tab_p,
            False, TM_TOKENS, bp)
        o_mla = _attn_prompt(q, k, v, TQ_PROMPT).reshape(bp * tp, -1)
        seq3 = lambda a: a.reshape(bp, tp, -1)
        s0_p = jnp.zeros((bp, GLA_HEADS, GLA_DK, GLA_DV), F32)
        o_gla, s_p = _gla(seq3(gq), seq3(gk), seq3(gv), seq3(la), s0_p, GLA_CHUNK, 1)
        hp = _mix_out(hp, mod_p, o_mla, o_gla.reshape(bp * tp, -1), gg, g_gla_o[l], w_lat, w_o_mla,
                      w_o_gla, False, TM_TOKENS)
        hp = _ffn(hp, mod_p, 2, *f2, norm_final, last, TM_TOKENS)

        hs = _ffn(hs, mod_s, 0, *f1, norm_final, False, tm_s)
        ql, ckv_s, kr_s, gq, gk, gv, la, gg = _mix_in(
            hs, mod_s, norm_mix[l], w_in_r, g_qa[l], wq, g_kva[l], wq_s, wgate, b_gate[l], tab_s,
            True, tm_s, bs)
        n_lat = MLA_HEADS * MLA_KV_LORA
        rows = ts * MLA_HEADS
        q_lat = ql[:, :n_lat].reshape(bs, rows, MLA_KV_LORA)
        q_rope = ql[:, n_lat:].reshape(bs, rows, MLA_ROPE)
        o_lat = _attn_sample(page_table, q_lat, q_rope, ckv_s.reshape(bs, ts, -1), kr_s.reshape(bs, ts, -1),
                             cache_ckv, jnp.swapaxes(cache_krope, 2, 3), l, PAGES_PER_CHUNK)
        pad = lambda a: jnp.pad(a.reshape(bs, ts, -1), ((0, 0), (0, GLA_SAMPLE_PAD - ts), (0, 0)))
        o_gla, s_s = _gla(pad(gq), pad(gk), pad(gv), pad(la), state_gla[l], GLA_SAMPLE_PAD, GLA_SAMPLE_SEQS)
        hs = _mix_out(hs, mod_s, o_lat.reshape(bs * ts, n_lat), o_gla[:, :ts].reshape(bs * ts, -1), gg,
                      g_gla_o[l], w_lat, w_o_mla, w_o_gla, True, tm_s)
        hs = _ffn(hs, mod_s, 2, *f2, norm_final, last, tm_s)

        for lst, a in zip(outs, (ckv_p.reshape(bp, tp, -1), kr_p.reshape(bp, tp, -1), s_p,
                                 ckv_s.reshape(bs, ts, -1), kr_s.reshape(bs, ts, -1), s_s)):
            lst.append(a)

    return (hp.reshape(bp, tp, d), hs.reshape(bs, ts, d)) + tuple(jnp.stack(o) for o in outs)
```

```python
import functools
import math

import jax
import jax.numpy as jnp
from jax import lax
from jax.experimental import pallas as pl
from jax.experimental.pallas import tpu as pltpu

F32 = jnp.float32
BF16 = jnp.bfloat16

PAGE_SIZE = 128
MLA_HEADS = 8
MLA_NOPE = 64
MLA_ROPE = 32
MLA_V = 64
MLA_Q_LORA = 384
MLA_KV_LORA = 256
MLA_SCALE = (MLA_NOPE + MLA_ROPE) ** -0.5
LOG2_E = math.log2(math.e)
ROPE_THETA = 10000.0
GLA_HEADS = 4
GLA_DK = 64
GLA_DV = 128
GLA_GATE_RANK = 16
GLA_GATE_TAU = 16.0
GLA_CHUNK = 64
N_ADA = 9
EPS = 1e-6

LANE = 128
HEAD_PAD = 128
VMEM_LIMIT_BYTES = 56 * 1024 * 1024

C_QA = 0
C_KVA = C_QA + MLA_Q_LORA
C_GQ = C_KVA + MLA_KV_LORA
C_GK = C_GQ + GLA_HEADS * GLA_DK
C_GV = C_GK + GLA_HEADS * GLA_DK
C_GG = C_GV + GLA_HEADS * GLA_DV
C_BLKA = C_GG + GLA_HEADS * GLA_DV
C_BLKB = C_BLKA + LANE
C_END = C_BLKB + LANE


def _cparams(sem):
    return pltpu.CompilerParams(dimension_semantics=sem, vmem_limit_bytes=VMEM_LIMIT_BYTES)


def _resident(shape):
    return pl.BlockSpec(shape, lambda *_: (0,) * len(shape), pipeline_mode=pl.Buffered(1))


def _rms(x, g):
    return x * lax.rsqrt(jnp.mean(x * x, axis=-1, keepdims=True) + EPS) * g


def _silu(x):
    return x * jax.nn.sigmoid(x)


def _dot(a, b):
    return jnp.dot(a, b, preferred_element_type=F32)


def _dot_nt(a, b):
    return lax.dot_general(a, b, (((1,), (1,)), ((), ())), preferred_element_type=F32)


def _dot_tn(a, b):
    return lax.dot_general(a, b, (((0,), (0,)), ((), ())), preferred_element_type=F32)


def _ada_body(c_ref, w_ref, b_ref, o_ref):
    c = _silu(c_ref[...]).astype(BF16)
    o_ref[...] = _dot(c, w_ref[...].astype(BF16)) + b_ref[...]


def _ada(c_all, w_ada, b_ada):
    n, d = c_all.shape
    n_out = w_ada.shape[1]
    tn = 1024
    return pl.pallas_call(
        _ada_body,
        grid=(n_out // tn,),
        in_specs=[pl.BlockSpec((n, d), lambda j: (0, 0)),
                  pl.BlockSpec((d, tn), lambda j: (0, j)),
                  pl.BlockSpec((1, tn), lambda j: (0, j))],
        out_specs=pl.BlockSpec((n, tn), lambda j: (0, j)),
        out_shape=jax.ShapeDtypeStruct((n, n_out), F32),
        compiler_params=_cparams(("arbitrary",)),
        name="ada",
    )(c_all, w_ada, b_ada.reshape(1, n_out))


class _Mod:
    def __init__(self, m, per_token, tokens_per_seq):
        self.per_token = per_token
        self.tokens_per_seq = tokens_per_seq
        self.arr = m if per_token else m.reshape(m.shape[0], 1, m.shape[1])

    def spec(self, tm, d, blk, width=3):
        if self.per_token:
            return pl.BlockSpec((tm, width * d), lambda i: (i, blk))
        per = self.tokens_per_seq // tm
        return pl.BlockSpec((None, 1, width * d), lambda i: (i // per, 0, blk))


FF_STEP = 1024


def _ff_chunks(ff):
    return tuple((c, min(c + FF_STEP, ff)) for c in range(0, ff, FF_STEP))


def _ffn_half_step(x, mod_ref, g_ref, w1_ref, w3_ref, w2_ref, d):
    shift, scale, gate = mod_ref[:, 0:d], mod_ref[:, d:2 * d], mod_ref[:, 2 * d:3 * d]
    h = (_rms(x, g_ref[...]) * (1.0 + scale) + shift).astype(BF16)
    y = None
    for c0, c1 in _ff_chunks(w1_ref.shape[1]):
        a = _dot(h, w1_ref[:, c0:c1])
        b = _dot(h, w3_ref[:, c0:c1])
        u = (_silu(a) * b).astype(BF16)
        part = _dot(u, w2_ref[c0:c1, :])
        y = part if y is None else y + part
    return x + 0.5 * gate * y


def _ffn_body(x_ref, mod_ref, g_ref, w1_ref, w3_ref, w2_ref, o_ref, *, d):
    o_ref[...] = _ffn_half_step(x_ref[...], mod_ref, g_ref, w1_ref, w3_ref, w2_ref, d)


def _ffn(x, mod, third, g, w1, w3, w2, tm):
    n, d = x.shape
    ff = w1.shape[1]
    return pl.pallas_call(
        functools.partial(_ffn_body, d=d),
        grid=(n // tm,),
        in_specs=[pl.BlockSpec((tm, d), lambda i: (i, 0)),
                  mod.spec(tm, d, third),
                  _resident((1, d)), _resident((d, ff)), _resident((d, ff)), _resident((ff, d))],
        out_specs=pl.BlockSpec((tm, d), lambda i: (i, 0)),
        out_shape=jax.ShapeDtypeStruct((n, d), F32),
        compiler_params=_cparams(("parallel",)),
        name="ffn",
    )(x, mod.arr, g.reshape(1, d), w1, w3, w2)


def _mix_in_body(x_ref, mod_ref, g_ref, win_ref, gqa_ref, wq_ref, gkva_ref, wkv_ref, wgate_ref,
                 bgate_ref, tab_ref, *out_refs, d, sample):
    if sample:
        (ql_ref, ckv_ref, kr_ref, gq_ref, gk_ref, gv_ref, la_ref, gg_ref) = out_refs
    else:
        (q_ref, k_ref, v_ref, ckv_ref, kr_ref, gq_ref, gk_ref, gv_ref, la_ref, gg_ref) = out_refs
    x = x_ref[...]
    shift, scale = mod_ref[:, 0:d], mod_ref[:, d:2 * d]
    h = (_rms(x, g_ref[...]) * (1.0 + scale) + shift).astype(BF16)
    proj = _dot(h, win_ref[...])

    cq, sq, ck, sk = (tab_ref[:, i * LANE:(i + 1) * LANE] for i in range(4))

    qn = _rms(proj[:, C_QA:C_KVA], gqa_ref[...]).astype(BF16)
    qq = _dot(qn, wq_ref[...])
    half = MLA_HEADS * HEAD_PAD
    q_heads = [qq[:, hh * HEAD_PAD:(hh + 1) * HEAD_PAD] * cq
               + qq[:, half + hh * HEAD_PAD:half + (hh + 1) * HEAD_PAD] * sq
               for hh in range(MLA_HEADS)]

    ckv = _rms(proj[:, C_KVA:C_GQ], gkva_ref[...])
    ckv_ref[...] = ckv
    blk_a = proj[:, C_BLKA:C_BLKB]
    kr_blk = blk_a * ck + proj[:, C_BLKB:C_END] * sk
    kr_ref[...] = kr_blk[:, 0:MLA_ROPE]

    if sample:
        q_all = jnp.concatenate(q_heads, axis=1).astype(BF16)
        ql_ref[...] = _dot(q_all, wkv_ref[...]).astype(BF16)
    else:
        for hh in range(MLA_HEADS):
            q_ref[hh] = q_heads[hh].astype(BF16)
        lhs = jnp.concatenate([ckv.astype(BF16), kr_blk.astype(BF16)], axis=1)
        kv = _dot(lhs, wkv_ref[...])
        for hh in range(MLA_HEADS):
            k_ref[hh] = kv[:, hh * HEAD_PAD:(hh + 1) * HEAD_PAD].astype(BF16)
        for pr in range(MLA_HEADS * MLA_V // LANE):
            v_ref[pr] = kv[:, half + pr * LANE:half + (pr + 1) * LANE].astype(BF16)

    gq_ref[...] = proj[:, C_GQ:C_GK]
    gk_ref[...] = proj[:, C_GK:C_GV]
    gv_ref[...] = proj[:, C_GV:C_GG]
    gg_ref[...] = proj[:, C_GG:C_BLKA]
    z = _dot(blk_a.astype(BF16), wgate_ref[...]) + bgate_ref[...]
    la_ref[...] = (jnp.minimum(z, 0.0) - jnp.log1p(jnp.exp(-jnp.abs(z)))) / GLA_GATE_TAU


def _mix_in(x, mod, g, w_in_r, g_qa, wq, g_kva, wkv, wgate, b_gate, tab, sample, tm, n_seq):
    n, d = x.shape
    t_seq = n // n_seq
    n_pos_tiles = tab.shape[0] // tm
    hk = GLA_HEADS * GLA_DK
    hv = GLA_HEADS * GLA_DV
    tok = lambda w: pl.BlockSpec((tm, w), lambda i: (i, 0))
    tok_shape = lambda w, dt=F32: jax.ShapeDtypeStruct((n, w), dt)
    gla_specs = [tok(hk), tok(hk), tok(hv), tok(hk), tok(hv)]
    gla_shapes = [tok_shape(hk), tok_shape(hk), tok_shape(hv), tok_shape(hk), tok_shape(hv)]
    if sample:
        wl = wkv.shape[1]
        out_specs = [tok(wl), tok(MLA_KV_LORA), tok(MLA_ROPE)] + gla_specs
        out_shape = [tok_shape(wl, BF16), tok_shape(MLA_KV_LORA), tok_shape(MLA_ROPE)] + gla_shapes
    else:
        per = t_seq // tm
        n_vp = MLA_HEADS * MLA_V // LANE
        head_spec = lambda nh: pl.BlockSpec((None, nh, tm, HEAD_PAD), lambda i: (i // per, 0, i % per, 0))
        head_shape = lambda nh: jax.ShapeDtypeStruct((n_seq, nh, t_seq, HEAD_PAD), BF16)
        out_specs = ([head_spec(MLA_HEADS)] * 2 + [head_spec(n_vp), tok(MLA_KV_LORA), tok(MLA_ROPE)]
                     + gla_specs)
        out_shape = ([head_shape(MLA_HEADS)] * 2 + [head_shape(n_vp), tok_shape(MLA_KV_LORA),
                                                    tok_shape(MLA_ROPE)] + gla_shapes)
    body = functools.partial(_mix_in_body, d=d, sample=sample)
    return pl.pallas_call(
        body,
        grid=(n // tm,),
        in_specs=[pl.BlockSpec((tm, d), lambda i: (i, 0)),
                  mod.spec(tm, d, 1),
                  _resident((1, d)), _resident(w_in_r.shape), _resident((1, MLA_Q_LORA)),
                  _resident(wq.shape), _resident((1, MLA_KV_LORA)), _resident(wkv.shape),
                  _resident(wgate.shape), _resident((1, hk)),
                  pl.BlockSpec((tm, 4 * LANE), lambda i: (i % n_pos_tiles, 0))],
        out_specs=out_specs,
        out_shape=out_shape,
        compiler_params=_cparams(("parallel",)),
        name="mix_in_sample" if sample else "mix_in_prompt",
    )(x, mod.arr, g.reshape(1, d), w_in_r, g_qa.reshape(1, -1), wq, g_kva.reshape(1, -1), wkv,
      wgate, b_gate.reshape(1, hk), tab)


def _attn_prompt_body(q_ref, k_ref, v_ref, o_ref, *, tq, n_pairs):
    n_heads = 2 * n_pairs
    i = pl.program_id(2)
    qs = [q_ref[hh] for hh in range(n_heads)]

    def scores(hh, j):
        off = pl.multiple_of(j * tq, tq)
        return _dot_nt(qs[hh], k_ref[hh, pl.ds(off, tq), :]), v_ref[hh // 2, pl.ds(off, tq), :]

    init = []
    for hh in range(n_heads):
        s, v = scores(hh, i)
        row = lax.broadcasted_iota(jnp.int32, s.shape, 0)
        col = lax.broadcasted_iota(jnp.int32, s.shape, 1)
        s = jnp.where(col <= row, s, jnp.finfo(F32).min)
        m = jnp.max(s, axis=-1, keepdims=True)
        p = jnp.exp2(s - m)
        init += [m, jnp.sum(p, axis=-1, keepdims=True), _dot(p.astype(BF16), v)]

    def step(j, carry):
        out = []
        for hh in range(n_heads):
            m, l, acc = carry[3 * hh:3 * hh + 3]
            s, v = scores(hh, j)
            m_new = jnp.maximum(m, jnp.max(s, axis=-1, keepdims=True))
            alpha = jnp.exp2(m - m_new)
            p = jnp.exp2(s - m_new)
            out += [m_new, alpha * l + jnp.sum(p, axis=-1, keepdims=True),
                    alpha * acc + _dot(p.astype(BF16), v)]
        return tuple(out)

    fin = lax.fori_loop(0, i, step, tuple(init))
    for pr in range(n_pairs):
        e, o = 6 * pr, 6 * pr + 3
        o_even, o_odd = fin[e + 2] / fin[e + 1], fin[o + 2] / fin[o + 1]
        lane = lax.broadcasted_iota(jnp.int32, o_even.shape, 1)
        o_ref[:, pr * LANE:(pr + 1) * LANE] = jnp.where(lane < MLA_V, o_even, o_odd).astype(o_ref.dtype)


def _attn_prompt(q, k, v, tq, n_pairs):
    b, hh, t, w = q.shape
    nh = 2 * n_pairs
    body = functools.partial(_attn_prompt_body, tq=tq, n_pairs=n_pairs)
    return pl.pallas_call(
        body,
        grid=(b, hh // nh, t // tq),
        in_specs=[pl.BlockSpec((None, nh, tq, w), lambda bi, hi, i: (bi, hi, i, 0)),
                  pl.BlockSpec((None, nh, t, w), lambda bi, hi, i: (bi, hi, 0, 0)),
                  pl.BlockSpec((None, n_pairs, t, w), lambda bi, hi, i: (bi, hi, 0, 0))],
        out_specs=pl.BlockSpec((None, tq, n_pairs * w), lambda bi, hi, i: (bi, i, hi)),
        out_shape=jax.ShapeDtypeStruct((b, t, hh // 2 * w), BF16),
        compiler_params=_cparams(("parallel", "parallel", "arbitrary")),
        name="attn_prompt",
    )(q, k, v)


def _attn_sample_body(pt_ref, ql_ref, qr_ref, cn_ref, kn_ref, ckv_hbm, kr_hbm, o_ref, ckv_buf, kr_buf, sems,
                      *, layer, n_chunks, pg):
    b = pl.program_id(0)
    n_seq = pl.num_programs(0)

    def copies(seq, c):
        out = []
        for p in range(pg):
            page = pt_ref[seq, c * pg + p]
            rows = pl.ds(p * PAGE_SIZE, PAGE_SIZE)
            out.append(pltpu.make_async_copy(ckv_hbm.at[layer, page], ckv_buf.at[c, rows, :], sems.at[c, 0]))
            out.append(pltpu.make_async_copy(kr_hbm.at[layer, page], kr_buf.at[c, :, rows], sems.at[c, 1]))
        return out

    @pl.when(b == 0)
    def _():
        for c in range(n_chunks):
            for cp in copies(0, c):
                cp.start()

    ql = ql_ref[...]
    qr = qr_ref[...]

    cn = cn_ref[...]
    s = _dot_nt(ql.astype(F32), cn) + _dot_nt(qr.astype(F32), kn_ref[...])
    row_tok = lax.broadcasted_iota(jnp.int32, s.shape, 0) // MLA_HEADS
    col = lax.broadcasted_iota(jnp.int32, s.shape, 1)
    s = jnp.where(col <= row_tok, s, jnp.finfo(F32).min)
    m = jnp.max(s, axis=-1, keepdims=True)
    p = jnp.exp2(s - m)
    l = jnp.sum(p, axis=-1, keepdims=True)
    acc = _dot(p, cn)

    def scores(c):
        for cp in copies(b, c):
            cp.wait()
        ck = ckv_buf[c].astype(BF16)
        return _dot_nt(ql, ck) + _dot(qr, kr_buf[c].astype(BF16)), ck

    nxt = jnp.minimum(b + 1, n_seq - 1)
    s_next, ck_next = scores(0)
    for c in range(n_chunks):
        s, ck = s_next, ck_next
        if c + 1 < n_chunks:
            s_next, ck_next = scores(c + 1)
        m_new = jnp.maximum(m, jnp.max(s, axis=-1, keepdims=True))
        alpha = jnp.exp2(m - m_new)
        p = jnp.exp2(s - m_new)
        l = alpha * l + jnp.sum(p, axis=-1, keepdims=True)
        acc = alpha * acc + _dot(p.astype(BF16), ck)
        m = m_new
        for cp in copies(nxt, c):
            cp.start()
    o_ref[...] = acc / l

    @pl.when(b == n_seq - 1)
    def _():
        for c in range(n_chunks):
            for cp in copies(nxt, c):
                cp.wait()


def _attn_sample(page_table, q_lat, q_rope, ckv_new, kr_new, cache_ckv, cache_krope_t, layer, pg):
    n_seq, rows, c = q_lat.shape
    t_new = ckv_new.shape[1]
    r = q_rope.shape[2]
    n_chunks = page_table.shape[1] // pg
    seq = lambda shape: pl.BlockSpec((None,) + shape, lambda b, pt: (b, 0, 0))
    body = functools.partial(_attn_sample_body, layer=layer, n_chunks=n_chunks, pg=pg)
    grid_spec = pltpu.PrefetchScalarGridSpec(
        num_scalar_prefetch=1,
        grid=(n_seq,),
        in_specs=[seq((rows, c)), seq((rows, r)), seq((t_new, c)), seq((t_new, r)),
                  pl.BlockSpec(memory_space=pl.ANY), pl.BlockSpec(memory_space=pl.ANY)],
        out_specs=seq((rows, c)),
        scratch_shapes=[pltpu.VMEM((n_chunks, pg * PAGE_SIZE, c), F32),
                        pltpu.VMEM((n_chunks, r, pg * PAGE_SIZE), F32),
                        pltpu.SemaphoreType.DMA((n_chunks, 2))],
    )
    return pl.pallas_call(
        body,
        grid_spec=grid_spec,
        out_shape=jax.ShapeDtypeStruct((n_seq, rows, c), F32),
        compiler_params=_cparams(("arbitrary",)),
        name="attn_sample",
    )(page_table, q_lat, q_rope, ckv_new, kr_new, cache_ckv, cache_krope_t)


def _gla_body(q_ref, k_ref, v_ref, la_ref, s0_ref, o_ref, s_ref, st_sc, *, n_seq, n_chunks, chunk):
    for sq in range(n_seq):
        for hh in range(GLA_HEADS):
            st_sc[sq, hh] = s0_ref[sq, hh].T
    r = lax.broadcasted_iota(jnp.int32, (chunk, chunk), 0)
    c = lax.broadcasted_iota(jnp.int32, (chunk, chunk), 1)
    causal = c <= r
    tri = causal.astype(BF16)

    def one_chunk(ci, _):
        off = pl.multiple_of(ci * chunk, chunk)
        for sq in range(n_seq):
            g = la_ref[sq, pl.ds(off, chunk), :]
            g_hi = g.astype(BF16)
            g_lo = (g - g_hi.astype(F32)).astype(BF16)
            b = _dot(tri, g_hi) + _dot(tri, g_lo)
            b_last = b[chunk - 1:chunk, :]
            q = q_ref[sq, pl.ds(off, chunk), :]
            k = k_ref[sq, pl.ds(off, chunk), :]
            v = v_ref[sq, pl.ds(off, chunk), :].astype(BF16)
            qe = (q * jnp.exp(b)).astype(BF16)
            ke = (k * jnp.exp(-b)).astype(BF16)
            kd = (k * jnp.exp(b_last - b)).astype(BF16)
            decay = jnp.exp(b_last)
            outs = []
            for hh in range(GLA_HEADS):
                ks = slice(hh * GLA_DK, (hh + 1) * GLA_DK)
                vs = slice(hh * GLA_DV, (hh + 1) * GLA_DV)
                st = st_sc[sq, hh]
                a = jnp.where(causal, _dot_nt(qe[:, ks], ke[:, ks]), 0.0)
                outs.append(_dot_nt(qe[:, ks], st.astype(BF16)) + _dot(a.astype(BF16), v[:, vs]))
                st_sc[sq, hh] = st * decay[:, ks] + _dot_tn(v[:, vs], kd[:, ks])
            o_ref[sq, pl.ds(off, chunk), :] = jnp.concatenate(outs, axis=1)
        return 0

    lax.fori_loop(0, n_chunks, one_chunk, 0, unroll=min(GLA_UNROLL, n_chunks))
    for sq in range(n_seq):
        for hh in range(GLA_HEADS):
            s_ref[sq, hh] = st_sc[sq, hh].T


def _gla(gq, gk, gv, la, s0, chunk, seqs_per_step):
    b, t, hk = gq.shape
    hv = gv.shape[2]
    g = seqs_per_step
    blk = lambda w: pl.BlockSpec((g, t, w), lambda i: (i, 0, 0))
    st = pl.BlockSpec((g, GLA_HEADS, GLA_DK, GLA_DV), lambda i: (i, 0, 0, 0))
    body = functools.partial(_gla_body, n_seq=g, n_chunks=t // chunk, chunk=chunk)
    return pl.pallas_call(
        body,
        grid=(b // g,),
        in_specs=[blk(hk), blk(hk), blk(hv), blk(hk), st],
        out_specs=[blk(hv), st],
        out_shape=[jax.ShapeDtypeStruct((b, t, hv), F32),
                   jax.ShapeDtypeStruct((b, GLA_HEADS, GLA_DK, GLA_DV), F32)],
        scratch_shapes=[pltpu.VMEM((g, GLA_HEADS, GLA_DV, GLA_DK), F32)],
        compiler_params=_cparams(("parallel",)),
        name="gla",
    )(gq, gk, gv, la, s0)


def _mix_out_ffn_body(x_ref, mod2_ref, mod3_ref, om_ref, og_ref, gg_ref, go_ref, wlat_ref, wom_ref, wog_ref,
                      g_ref, w1_ref, w3_ref, w2_ref, gf_ref, o_ref, *, d, sample, final):
    gate = mod2_ref[...]
    om = om_ref[...]
    if sample:
        om = _dot(om.astype(BF16), wlat_ref[...]).astype(BF16)
    parts = []
    for hh in range(GLA_HEADS):
        vs = slice(hh * GLA_DV, (hh + 1) * GLA_DV)
        parts.append(_rms(og_ref[:, vs], go_ref[...]) * _silu(gg_ref[:, vs]))
    og = jnp.concatenate(parts, axis=1).astype(BF16)
    mix = _dot(om, wom_ref[...]) + _dot(og, wog_ref[...])
    x = x_ref[...] + gate * mix
    out = _ffn_half_step(x, mod3_ref, g_ref, w1_ref, w3_ref, w2_ref, d)
    if final:
        out = _rms(out, gf_ref[...])
    o_ref[...] = out


def _mix_out_ffn(x, mod, o_mla, o_gla, gg, g_gla_o, w_lat, w_o_mla, w_o_gla, g, w1, w3, w2, g_final,
                 sample, final, tm):
    n, d = x.shape
    ff = w1.shape[1]
    hv = GLA_HEADS * GLA_DV
    tok = lambda w: pl.BlockSpec((tm, w), lambda i: (i, 0))
    body = functools.partial(_mix_out_ffn_body, d=d, sample=sample, final=final)
    return pl.pallas_call(
        body,
        grid=(n // tm,),
        in_specs=[tok(d), mod.spec(tm, d, 5, 1), mod.spec(tm, d, 2), tok(o_mla.shape[1]), tok(hv), tok(hv),
                  _resident((1, GLA_DV)), _resident(w_lat.shape), _resident(w_o_mla.shape),
                  _resident(w_o_gla.shape),
                  _resident((1, d)), _resident((d, ff)), _resident((d, ff)), _resident((ff, d)),
                  _resident((1, d))],
        out_specs=tok(d),
        out_shape=jax.ShapeDtypeStruct((n, d), F32),
        compiler_params=_cparams(("parallel",)),
        name="mix_out_ffn_sample" if sample else "mix_out_ffn_prompt",
    )(x, mod.arr, mod.arr, o_mla, o_gla, gg, g_gla_o.reshape(1, GLA_DV), w_lat, w_o_mla, w_o_gla,
      g.reshape(1, d), w1, w3, w2, g_final.reshape(1, d))


def _prep_w_in(w_in):
    d = w_in.shape[0]
    o = 0
    parts = {}
    for name, size in (("qa", MLA_Q_LORA), ("kva", MLA_KV_LORA), ("kr", MLA_ROPE),
                       ("gq", GLA_HEADS * GLA_DK), ("gk", GLA_HEADS * GLA_DK), ("gv", GLA_HEADS * GLA_DV),
                       ("ga", GLA_GATE_RANK), ("gg", GLA_HEADS * GLA_DV)):
        parts[name] = w_in[:, o:o + size]
        o += size
    hr = MLA_ROPE // 2
    kr = parts["kr"]
    z = lambda w: jnp.zeros((d, w), w_in.dtype)
    blk_a = jnp.concatenate([kr, parts["ga"], z(LANE - MLA_ROPE - GLA_GATE_RANK)], axis=1)
    blk_b = jnp.concatenate([kr[:, hr:], kr[:, :hr], z(LANE - MLA_ROPE)], axis=1)
    w = jnp.concatenate([parts["qa"], parts["kva"], parts["gq"] * (GLA_DK ** -0.5), parts["gk"],
                         parts["gv"], parts["gg"], blk_a, blk_b], axis=1)
    return w.astype(BF16)


def _prep_wq(w_qb):
    ql = w_qb.shape[0]
    hr = MLA_ROPE // 2
    nope, r1, r2 = w_qb[..., :MLA_NOPE], w_qb[..., MLA_NOPE:MLA_NOPE + hr], w_qb[..., MLA_NOPE + hr:]
    zp = jnp.zeros((ql, MLA_HEADS, HEAD_PAD - MLA_NOPE - MLA_ROPE), w_qb.dtype)
    zn = jnp.zeros_like(nope)
    a = jnp.concatenate([nope, r1, r2, zp], axis=-1).reshape(ql, MLA_HEADS * HEAD_PAD)
    b = jnp.concatenate([zn, r2, r1, zp], axis=-1).reshape(ql, MLA_HEADS * HEAD_PAD)
    return (jnp.concatenate([a, b], axis=1) * (MLA_SCALE * LOG2_E)).astype(BF16)


def _prep_wkv_prompt(w_kvb):
    rows = w_kvb.shape[0] + LANE
    k_top = jnp.pad(w_kvb[..., :MLA_NOPE], ((0, 0), (0, 0), (0, HEAD_PAD - MLA_NOPE)))
    sel = jnp.eye(LANE, HEAD_PAD, k=MLA_NOPE, dtype=F32)
    sel = jnp.where(jnp.arange(LANE)[:, None] < MLA_ROPE, sel, 0.0)
    k = jnp.concatenate([k_top, jnp.broadcast_to(sel[:, None, :], (LANE, MLA_HEADS, HEAD_PAD))], axis=0)
    v = jnp.pad(w_kvb[..., MLA_NOPE:], ((0, LANE), (0, 0), (0, 0)))
    return jnp.concatenate([k.reshape(rows, -1), v.reshape(rows, -1)], axis=1).astype(BF16)


def _prep_wq_sample(w_kvb):
    same_head = jnp.eye(MLA_HEADS, dtype=F32)[:, None, :, None]
    pad_n = ((0, 0), (0, HEAD_PAD - MLA_NOPE), (0, 0), (0, 0))
    lat = jnp.pad(jnp.transpose(w_kvb[..., :MLA_NOPE], (1, 2, 0))[:, :, None, :] * same_head, pad_n)
    sel = jnp.eye(HEAD_PAD, MLA_ROPE, k=-MLA_NOPE, dtype=F32)
    rope = sel[None, :, None, :] * same_head
    rows = MLA_HEADS * HEAD_PAD
    return jnp.concatenate([lat.reshape(rows, -1), rope.reshape(rows, -1)], axis=1).astype(BF16)


def _prep_w_lat_out(w_kvb):
    c = w_kvb.shape[0]
    same_head = jnp.eye(MLA_HEADS, dtype=F32)[:, None, :, None]
    w = jnp.transpose(w_kvb[..., MLA_NOPE:], (1, 0, 2))[:, :, None, :] * same_head
    return w.reshape(MLA_HEADS * c, MLA_HEADS * MLA_V).astype(BF16)


def _prep_w_o(w_o):
    n_mla = MLA_HEADS * MLA_V
    return w_o[:n_mla].astype(BF16), w_o[n_mla:].astype(BF16)


def _prep_gate(w_gate_b):
    return jnp.pad(w_gate_b, ((MLA_ROPE, LANE - MLA_ROPE - GLA_GATE_RANK), (0, 0))).astype(BF16)


def _rope_table(pos):
    hr = MLA_ROPE // 2
    inv = ROPE_THETA ** (-jnp.arange(0, MLA_ROPE, 2, dtype=F32) / MLA_ROPE)
    ang = pos[:, None] * inv[None, :]
    cos, sin = jnp.cos(ang), jnp.sin(ang)
    t = pos.shape[0]
    z = lambda w: jnp.zeros((t, w), F32)
    pad_q = HEAD_PAD - MLA_NOPE - MLA_ROPE
    cq = jnp.concatenate([jnp.ones((t, MLA_NOPE), F32), cos, cos, z(pad_q)], axis=1)
    sq = jnp.concatenate([z(MLA_NOPE), -sin, sin, z(pad_q)], axis=1)
    ck = jnp.concatenate([cos, cos, z(LANE - 2 * hr)], axis=1)
    sk = jnp.concatenate([-sin, sin, z(LANE - 2 * hr)], axis=1)
    return jnp.concatenate([cq, sq, ck, sk], axis=1)


TM_TOKENS = 512
TM_SAMPLE = 256
TQ_PROMPT = 512
HEAD_PAIRS_PER_STEP = 2
PAGES_PER_CHUNK = 32
GLA_SAMPLE_PAD = 16
GLA_SAMPLE_SEQS = 8
GLA_UNROLL = 8


def kernel(x_prompt, x_sample, cache_ckv, cache_krope, state_gla, page_table, c_prompt, c_sample,
           w_ada, b_ada, norm_ffn1, ffn1_w1, ffn1_w3, ffn1_w2, norm_mix, w_in, g_qa, w_qb, g_kva, w_kvb,
           w_gate_b, b_gate, g_gla_o, w_o, norm_ffn2, ffn2_w1, ffn2_w3, ffn2_w2, norm_final):
    bp, tp, d = x_prompt.shape
    bs, ts, _ = x_sample.shape
    depth = w_ada.shape[0]
    n_pages = page_table.shape[1]
    past_len = n_pages * PAGE_SIZE

    tab_p = _rope_table(jnp.arange(tp, dtype=F32))
    tm_s = min(TM_SAMPLE, bs * ts)
    tab_s = jnp.tile(_rope_table(past_len + jnp.arange(ts, dtype=F32)), (tm_s // ts, 1))

    hp = x_prompt.reshape(bp * tp, d)
    hs = x_sample.reshape(bs * ts, d)
    outs = [[] for _ in range(6)]
    for l in range(depth):
        m = _ada(jnp.concatenate([c_prompt, c_sample], axis=0), w_ada[l], b_ada[l])
        mod_p = _Mod(m[:bp], False, tp)
        mod_s = _Mod(jnp.repeat(m[bp:], ts, axis=0), True, ts)

        f1 = (norm_ffn1[l], ffn1_w1[l].astype(BF16), ffn1_w3[l].astype(BF16), ffn1_w2[l].astype(BF16))
        f2 = (norm_ffn2[l], ffn2_w1[l].astype(BF16), ffn2_w3[l].astype(BF16), ffn2_w2[l].astype(BF16))
        w_in_r = _prep_w_in(w_in[l])
        wq = _prep_wq(w_qb[l])
        wkv_p = _prep_wkv_prompt(w_kvb[l])
        wq_s = _prep_wq_sample(w_kvb[l])
        w_lat = _prep_w_lat_out(w_kvb[l])
        w_o_mla, w_o_gla = _prep_w_o(w_o[l])
        wgate = _prep_gate(w_gate_b[l])
        last = l == depth - 1

        hp = _ffn(hp, mod_p, 0, *f1, TM_TOKENS)
        q, k, v, ckv_p, kr_p, gq, gk, gv, la, gg = _mix_in(
            hp, mod_p, norm_mix[l], w_in_r, g_qa[l], wq, g_kva[l], wkv_p, wgate, b_gate[l], tab_p,
            False, TM_TOKENS, bp)
        o_mla = _attn_prompt(q, k, v, TQ_PROMPT, HEAD_PAIRS_PER_STEP).reshape(bp * tp, -1)
        seq3 = lambda a: a.reshape(bp, tp, -1)
        s0_p = jnp.zeros((bp, GLA_HEADS, GLA_DK, GLA_DV), F32)
        o_gla, s_p = _gla(seq3(gq), seq3(gk), seq3(gv), seq3(la), s0_p, GLA_CHUNK, 1)
        hp = _mix_out_ffn(hp, mod_p, o_mla, o_gla.reshape(bp * tp, -1), gg, g_gla_o[l], w_lat, w_o_mla,
                          w_o_gla, *f2, norm_final, False, last, TM_TOKENS)

        hs = _ffn(hs, mod_s, 0, *f1, tm_s)
        ql, ckv_s, kr_s, gq, gk, gv, la, gg = _mix_in(
            hs, mod_s, norm_mix[l], w_in_r, g_qa[l], wq, g_kva[l], wq_s, wgate, b_gate[l], tab_s,
            True, tm_s, bs)
        n_lat = MLA_HEADS * MLA_KV_LORA
        rows = ts * MLA_HEADS
        q_lat = ql[:, :n_lat].reshape(bs, rows, MLA_KV_LORA)
        q_rope = ql[:, n_lat:].reshape(bs, rows, MLA_ROPE)
        o_lat = _attn_sample(page_table, q_lat, q_rope, ckv_s.reshape(bs, ts, -1), kr_s.reshape(bs, ts, -1),
                             cache_ckv, jnp.swapaxes(cache_krope, 2, 3), l, PAGES_PER_CHUNK)
        pad = lambda a: jnp.pad(a.reshape(bs, ts, -1), ((0, 0), (0, GLA_SAMPLE_PAD - ts), (0, 0)))
        o_gla, s_s = _gla(pad(gq), pad(gk), pad(gv), pad(la), state_gla[l], GLA_SAMPLE_PAD, GLA_SAMPLE_SEQS)
        hs = _mix_out_ffn(hs, mod_s, o_lat.reshape(bs * ts, n_lat), o_gla[:, :ts].reshape(bs * ts, -1), gg,
                          g_gla_o[l], w_lat, w_o_mla, w_o_gla, *f2, norm_final, True, last, tm_s)

        for lst, a in zip(outs, (ckv_p.reshape(bp, tp, -1), kr_p.reshape(bp, tp, -1), s_p,
                                 ckv_s.reshape(bs, ts, -1), kr_s.reshape(bs, ts, -1), s_s)):
            lst.append(a)

    return (hp.reshape(bp, tp, d), hs.reshape(bs, ts, d)) + tuple(jnp.stack(o) for o in outs)
```

```python
import functools
import math

import jax
import jax.numpy as jnp
from jax import lax
from jax.experimental import pallas as pl
from jax.experimental.pallas import tpu as pltpu

F32 = jnp.float32
BF16 = jnp.bfloat16

PAGE_SIZE = 128
MLA_HEADS = 8
MLA_NOPE = 64
MLA_ROPE = 32
MLA_V = 64
MLA_Q_LORA = 384
MLA_KV_LORA = 256
MLA_SCALE = (MLA_NOPE + MLA_ROPE) ** -0.5
LOG2_E = math.log2(math.e)
ROPE_THETA = 10000.0
GLA_HEADS = 4
GLA_DK = 64
GLA_DV = 128
GLA_GATE_RANK = 16
GLA_GATE_TAU = 16.0
GLA_CHUNK = 64
N_ADA = 9
EPS = 1e-6

LANE = 128
HEAD_PAD = 128
VMEM_LIMIT_BYTES = 56 * 1024 * 1024

C_QA = 0
C_KVA = C_QA + MLA_Q_LORA
C_GQ = C_KVA + MLA_KV_LORA
C_GK = C_GQ + GLA_HEADS * GLA_DK
C_GV = C_GK + GLA_HEADS * GLA_DK
C_GG = C_GV + GLA_HEADS * GLA_DV
C_BLKA = C_GG + GLA_HEADS * GLA_DV
C_END = C_BLKA + LANE
N_ROPE_TABS = 6


def _cparams(sem):
    return pltpu.CompilerParams(dimension_semantics=sem, vmem_limit_bytes=VMEM_LIMIT_BYTES)


def _resident(shape):
    return pl.BlockSpec(shape, lambda *_: (0,) * len(shape), pipeline_mode=pl.Buffered(1))


def _rms(x, g):
    return x * lax.rsqrt(jnp.mean(x * x, axis=-1, keepdims=True) + EPS) * g


def _silu(x):
    return x * jax.nn.sigmoid(x)


def _dot(a, b):
    return jnp.dot(a, b, preferred_element_type=F32)


def _dot_nt(a, b):
    return lax.dot_general(a, b, (((1,), (1,)), ((), ())), preferred_element_type=F32)


def _dot_tn(a, b):
    return lax.dot_general(a, b, (((0,), (0,)), ((), ())), preferred_element_type=F32)


def _ada_body(c_ref, w_ref, b_ref, o_ref):
    c = _silu(c_ref[...]).astype(BF16)
    o_ref[...] = _dot(c, w_ref[...].astype(BF16)) + b_ref[...]


def _ada(c_all, w_ada, b_ada):
    n, d = c_all.shape
    n_out = w_ada.shape[1]
    tn = 1024
    return pl.pallas_call(
        _ada_body,
        grid=(n_out // tn,),
        in_specs=[pl.BlockSpec((n, d), lambda j: (0, 0)),
                  pl.BlockSpec((d, tn), lambda j: (0, j)),
                  pl.BlockSpec((1, tn), lambda j: (0, j))],
        out_specs=pl.BlockSpec((n, tn), lambda j: (0, j)),
        out_shape=jax.ShapeDtypeStruct((n, n_out), F32),
        compiler_params=_cparams(("arbitrary",)),
        name="ada",
    )(c_all, w_ada, b_ada.reshape(1, n_out))


class _Mod:
    def __init__(self, m, per_token, tokens_per_seq):
        self.per_token = per_token
        self.tokens_per_seq = tokens_per_seq
        self.arr = m if per_token else m.reshape(m.shape[0], 1, m.shape[1])

    def spec(self, tm, d, blk, width=3):
        if self.per_token:
            return pl.BlockSpec((tm, width * d), lambda i: (i, blk))
        per = self.tokens_per_seq // tm
        return pl.BlockSpec((None, 1, width * d), lambda i: (i // per, 0, blk))


FF_STEP = 1024


def _ff_chunks(ff):
    return tuple((c, min(c + FF_STEP, ff)) for c in range(0, ff, FF_STEP))


def _ffn_half_step(x, mod_ref, g_ref, w1_ref, w3_ref, w2_ref, d):
    shift, scale, gate = mod_ref[:, 0:d], mod_ref[:, d:2 * d], mod_ref[:, 2 * d:3 * d]
    h = (_rms(x, g_ref[...]) * (1.0 + scale) + shift).astype(BF16)
    y = None
    for c0, c1 in _ff_chunks(w1_ref.shape[1]):
        a = _dot(h, w1_ref[:, c0:c1])
        b = _dot(h, w3_ref[:, c0:c1])
        u = (_silu(a) * b).astype(BF16)
        part = _dot(u, w2_ref[c0:c1, :])
        y = part if y is None else y + part
    return x + 0.5 * gate * y


def _ffn_body(x_ref, mod_ref, g_ref, w1_ref, w3_ref, w2_ref, o_ref, *, d):
    o_ref[...] = _ffn_half_step(x_ref[...], mod_ref, g_ref, w1_ref, w3_ref, w2_ref, d)


def _ffn(x, mod, third, g, w1, w3, w2, tm):
    n, d = x.shape
    ff = w1.shape[1]
    return pl.pallas_call(
        functools.partial(_ffn_body, d=d),
        grid=(n // tm,),
        in_specs=[pl.BlockSpec((tm, d), lambda i: (i, 0)),
                  mod.spec(tm, d, third),
                  _resident((1, d)), _resident((d, ff)), _resident((d, ff)), _resident((ff, d))],
        out_specs=pl.BlockSpec((tm, d), lambda i: (i, 0)),
        out_shape=jax.ShapeDtypeStruct((n, d), F32),
        compiler_params=_cparams(("parallel",)),
        name="ffn",
    )(x, mod.arr, g.reshape(1, d), w1, w3, w2)


def _mix_in_body(x_ref, mod_ref, g_ref, win_ref, gqa_ref, wq_ref, gkva_ref, wkv_ref, wgate_ref,
                 bgate_ref, tab_ref, *out_refs, d, sample):
    if sample:
        (ql_ref, ckv_ref, kr_ref, gq_ref, gk_ref, gv_ref, la_ref, gg_ref) = out_refs
    else:
        (q_ref, k_ref, v_ref, ckv_ref, kr_ref, gq_ref, gk_ref, gv_ref, la_ref, gg_ref) = out_refs
    x = x_ref[...]
    shift, scale = mod_ref[:, 0:d], mod_ref[:, d:2 * d]
    h = (_rms(x, g_ref[...]) * (1.0 + scale) + shift).astype(BF16)
    proj = _dot(h, win_ref[...])

    q_tabs = tuple(tab_ref[:, i * LANE:(i + 1) * LANE] for i in range(3))
    k_tabs = tuple(tab_ref[:, i * LANE:(i + 1) * LANE] for i in range(3, 6))
    hr = MLA_ROPE // 2

    def rotate(x, tabs):
        c, s_from_below, s_from_above = tabs
        return x * c + pltpu.roll(x, hr, 1) * s_from_below + pltpu.roll(x, LANE - hr, 1) * s_from_above

    qn = _rms(proj[:, C_QA:C_KVA], gqa_ref[...]).astype(BF16)
    qq = _dot(qn, wq_ref[...])
    half = MLA_HEADS * HEAD_PAD
    q_heads = [rotate(qq[:, hh * HEAD_PAD:(hh + 1) * HEAD_PAD], q_tabs) for hh in range(MLA_HEADS)]

    ckv = _rms(proj[:, C_KVA:C_GQ], gkva_ref[...])
    ckv_ref[...] = ckv
    blk_a = proj[:, C_BLKA:C_END]
    kr_blk = rotate(blk_a, k_tabs)
    kr_ref[...] = kr_blk[:, 0:MLA_ROPE]

    if sample:
        q_all = jnp.concatenate(q_heads, axis=1).astype(BF16)
        ql_ref[...] = _dot(q_all, wkv_ref[...]).astype(BF16)
    else:
        for hh in range(MLA_HEADS):
            q_ref[hh] = q_heads[hh].astype(BF16)
        lhs = jnp.concatenate([ckv.astype(BF16), kr_blk.astype(BF16)], axis=1)
        kv = _dot(lhs, wkv_ref[...])
        for hh in range(MLA_HEADS):
            k_ref[hh] = kv[:, hh * HEAD_PAD:(hh + 1) * HEAD_PAD].astype(BF16)
        for pr in range(MLA_HEADS * MLA_V // LANE):
            v_ref[pr] = kv[:, half + pr * LANE:half + (pr + 1) * LANE].astype(BF16)

    gq_ref[...] = proj[:, C_GQ:C_GK]
    gk_ref[...] = proj[:, C_GK:C_GV]
    gv_ref[...] = proj[:, C_GV:C_GG]
    gg_ref[...] = proj[:, C_GG:C_BLKA]
    z = _dot(blk_a.astype(BF16), wgate_ref[...]) + bgate_ref[...]
    la_ref[...] = (jnp.minimum(z, 0.0) - jnp.log1p(jnp.exp(-jnp.abs(z)))) / GLA_GATE_TAU


def _mix_in(x, mod, g, w_in_r, g_qa, wq, g_kva, wkv, wgate, b_gate, tab, sample, tm, n_seq):
    n, d = x.shape
    t_seq = n // n_seq
    n_pos_tiles = tab.shape[0] // tm
    hk = GLA_HEADS * GLA_DK
    hv = GLA_HEADS * GLA_DV
    tok = lambda w: pl.BlockSpec((tm, w), lambda i: (i, 0))
    tok_shape = lambda w, dt=F32: jax.ShapeDtypeStruct((n, w), dt)
    gla_specs = [tok(hk), tok(hk), tok(hv), tok(hk), tok(hv)]
    gla_shapes = [tok_shape(hk), tok_shape(hk), tok_shape(hv), tok_shape(hk), tok_shape(hv)]
    if sample:
        wl = wkv.shape[1]
        out_specs = [tok(wl), tok(MLA_KV_LORA), tok(MLA_ROPE)] + gla_specs
        out_shape = [tok_shape(wl, BF16), tok_shape(MLA_KV_LORA), tok_shape(MLA_ROPE)] + gla_shapes
    else:
        per = t_seq // tm
        n_vp = MLA_HEADS * MLA_V // LANE
        head_spec = lambda nh: pl.BlockSpec((None, nh, tm, HEAD_PAD), lambda i: (i // per, 0, i % per, 0))
        head_shape = lambda nh: jax.ShapeDtypeStruct((n_seq, nh, t_seq, HEAD_PAD), BF16)
        out_specs = ([head_spec(MLA_HEADS)] * 2 + [head_spec(n_vp), tok(MLA_KV_LORA), tok(MLA_ROPE)]
                     + gla_specs)
        out_shape = ([head_shape(MLA_HEADS)] * 2 + [head_shape(n_vp), tok_shape(MLA_KV_LORA),
                                                    tok_shape(MLA_ROPE)] + gla_shapes)
    body = functools.partial(_mix_in_body, d=d, sample=sample)
    return pl.pallas_call(
        body,
        grid=(n // tm,),
        in_specs=[pl.BlockSpec((tm, d), lambda i: (i, 0)),
                  mod.spec(tm, d, 1),
                  _resident((1, d)), _resident(w_in_r.shape), _resident((1, MLA_Q_LORA)),
                  _resident(wq.shape), _resident((1, MLA_KV_LORA)), _resident(wkv.shape),
                  _resident(wgate.shape), _resident((1, hk)),
                  pl.BlockSpec((tm, N_ROPE_TABS * LANE), lambda i: (i % n_pos_tiles, 0))],
        out_specs=out_specs,
        out_shape=out_shape,
        compiler_params=_cparams(("parallel",)),
        name="mix_in_sample" if sample else "mix_in_prompt",
    )(x, mod.arr, g.reshape(1, d), w_in_r, g_qa.reshape(1, -1), wq, g_kva.reshape(1, -1), wkv,
      wgate, b_gate.reshape(1, hk), tab)


def _attn_prompt_body(q_ref, k_ref, v_ref, o_ref, *, tq, n_pairs):
    n_heads = 2 * n_pairs
    i = pl.program_id(2)
    qs = [q_ref[hh] for hh in range(n_heads)]

    def scores(hh, j):
        off = pl.multiple_of(j * tq, tq)
        return _dot_nt(qs[hh], k_ref[hh, pl.ds(off, tq), :]), v_ref[hh // 2, pl.ds(off, tq), :]

    init = []
    for hh in range(n_heads):
        s, v = scores(hh, i)
        row = lax.broadcasted_iota(jnp.int32, s.shape, 0)
        col = lax.broadcasted_iota(jnp.int32, s.shape, 1)
        s = jnp.where(col <= row, s, jnp.finfo(F32).min)
        m = jnp.max(s, axis=-1, keepdims=True)
        p = jnp.exp2(s - m)
        init += [m, jnp.sum(p, axis=-1, keepdims=True), _dot(p.astype(BF16), v)]

    def step(j, carry):
        out = []
        for hh in range(n_heads):
            m, l, acc = carry[3 * hh:3 * hh + 3]
            s, v = scores(hh, j)
            m_new = jnp.maximum(m, jnp.max(s, axis=-1, keepdims=True))
            alpha = jnp.exp2(m - m_new)
            p = jnp.exp2(s - m_new)
            out += [m_new, alpha * l + jnp.sum(p, axis=-1, keepdims=True),
                    alpha * acc + _dot(p.astype(BF16), v)]
        return tuple(out)

    fin = lax.fori_loop(0, i, step, tuple(init))
    for pr in range(n_pairs):
        e, o = 6 * pr, 6 * pr + 3
        o_even, o_odd = fin[e + 2] / fin[e + 1], fin[o + 2] / fin[o + 1]
        lane = lax.broadcasted_iota(jnp.int32, o_even.shape, 1)
        o_ref[:, pr * LANE:(pr + 1) * LANE] = jnp.where(lane < MLA_V, o_even, o_odd).astype(o_ref.dtype)


def _attn_prompt(q, k, v, tq, n_pairs):
    b, hh, t, w = q.shape
    nh = 2 * n_pairs
    body = functools.partial(_attn_prompt_body, tq=tq, n_pairs=n_pairs)
    return pl.pallas_call(
        body,
        grid=(b, hh // nh, t // tq),
        in_specs=[pl.BlockSpec((None, nh, tq, w), lambda bi, hi, i: (bi, hi, i, 0)),
                  pl.BlockSpec((None, nh, t, w), lambda bi, hi, i: (bi, hi, 0, 0)),
                  pl.BlockSpec((None, n_pairs, t, w), lambda bi, hi, i: (bi, hi, 0, 0))],
        out_specs=pl.BlockSpec((None, tq, n_pairs * w), lambda bi, hi, i: (bi, i, hi)),
        out_shape=jax.ShapeDtypeStruct((b, t, hh // 2 * w), BF16),
        compiler_params=_cparams(("parallel", "parallel", "arbitrary")),
        name="attn_prompt",
    )(q, k, v)


def _attn_sample_body(pt_ref, ql_ref, qr_ref, cn_ref, kn_ref, ckv_hbm, kr_hbm, o_ref, ckv_buf, kr_buf, sems,
                      *, layer, n_chunks, pg):
    b = pl.program_id(0)
    n_seq = pl.num_programs(0)

    def copies(seq, c):
        out = []
        for p in range(pg):
            page = pt_ref[seq, c * pg + p]
            rows = pl.ds(p * PAGE_SIZE, PAGE_SIZE)
            out.append(pltpu.make_async_copy(ckv_hbm.at[layer, page], ckv_buf.at[c, rows, :], sems.at[c, 0]))
            out.append(pltpu.make_async_copy(kr_hbm.at[layer, page], kr_buf.at[c, :, rows], sems.at[c, 1]))
        return out

    @pl.when(b == 0)
    def _():
        for c in range(n_chunks):
            for cp in copies(0, c):
                cp.start()

    ql = ql_ref[...]
    qr = qr_ref[...]

    cn = cn_ref[...]
    s = _dot_nt(ql.astype(F32), cn) + _dot_nt(qr.astype(F32), kn_ref[...])
    row_tok = lax.broadcasted_iota(jnp.int32, s.shape, 0) // MLA_HEADS
    col = lax.broadcasted_iota(jnp.int32, s.shape, 1)
    s = jnp.where(col <= row_tok, s, jnp.finfo(F32).min)
    m = jnp.max(s, axis=-1, keepdims=True)
    p = jnp.exp2(s - m)
    l = jnp.sum(p, axis=-1, keepdims=True)
    acc = _dot(p, cn)

    nxt = jnp.minimum(b + 1, n_seq - 1)

    def scores(c):
        for cp in copies(b, c):
            cp.wait()
        ck = ckv_buf[c].astype(BF16)
        return _dot_nt(ql, ck) + _dot(qr, kr_buf[c].astype(BF16)), ck

    s_next, ck_next = scores(0)
    for c in range(n_chunks):
        s, ck = s_next, ck_next
        if c + 1 < n_chunks:
            s_next, ck_next = scores(c + 1)
        m_new = jnp.maximum(m, jnp.max(s, axis=-1, keepdims=True))
        alpha = jnp.exp2(m - m_new)
        p = jnp.exp2(s - m_new)
        l = alpha * l + jnp.sum(p, axis=-1, keepdims=True)
        acc = alpha * acc + _dot(p.astype(BF16), ck)
        m = m_new
        for cp in copies(nxt, c):
            cp.start()
    o_ref[...] = acc / l

    @pl.when(b == n_seq - 1)
    def _():
        for c in range(n_chunks):
            for cp in copies(nxt, c):
                cp.wait()


def _attn_sample(page_table, q_lat, q_rope, ckv_new, kr_new, cache_ckv, cache_krope_t, layer, pg):
    n_seq, rows, c = q_lat.shape
    t_new = ckv_new.shape[1]
    r = q_rope.shape[2]
    n_chunks = page_table.shape[1] // pg
    seq = lambda shape: pl.BlockSpec((None,) + shape, lambda b, pt: (b, 0, 0))
    body = functools.partial(_attn_sample_body, layer=layer, n_chunks=n_chunks, pg=pg)
    grid_spec = pltpu.PrefetchScalarGridSpec(
        num_scalar_prefetch=1,
        grid=(n_seq,),
        in_specs=[seq((rows, c)), seq((rows, r)), seq((t_new, c)), seq((t_new, r)),
                  pl.BlockSpec(memory_space=pl.ANY), pl.BlockSpec(memory_space=pl.ANY)],
        out_specs=seq((rows, c)),
        scratch_shapes=[pltpu.VMEM((n_chunks, pg * PAGE_SIZE, c), F32),
                        pltpu.VMEM((n_chunks, r, pg * PAGE_SIZE), F32),
                        pltpu.SemaphoreType.DMA((n_chunks, 2))],
    )
    return pl.pallas_call(
        body,
        grid_spec=grid_spec,
        out_shape=jax.ShapeDtypeStruct((n_seq, rows, c), F32),
        compiler_params=_cparams(("arbitrary",)),
        name="attn_sample",
    )(page_table, q_lat, q_rope, ckv_new, kr_new, cache_ckv, cache_krope_t)


def _gla_body(q_ref, k_ref, v_ref, la_ref, s0_ref, o_ref, s_ref, st_sc, *, n_seq, n_chunks, chunk):
    hk = GLA_HEADS * GLA_DK
    for sq in range(n_seq):
        st_sc[sq] = s0_ref[sq].reshape(hk, GLA_DV).T
    rows = GLA_HEADS * chunk
    r = lax.broadcasted_iota(jnp.int32, (chunk, chunk), 0)
    c = lax.broadcasted_iota(jnp.int32, (chunk, chunk), 1)
    tri = (c <= r).astype(BF16)
    rs = lax.broadcasted_iota(jnp.int32, (rows, chunk), 0)
    cs = lax.broadcasted_iota(jnp.int32, (rows, chunk), 1)
    causal_stacked = cs <= rs % chunk
    lane_head = lax.broadcasted_iota(jnp.int32, (chunk, hk), 1) // GLA_DK
    zero = jnp.zeros((chunk, hk), BF16)

    def stack_heads(x):
        return jnp.concatenate([jnp.where(lane_head == hh, x, zero) for hh in range(GLA_HEADS)], axis=0)

    def one_chunk(ci, _):
        off = pl.multiple_of(ci * chunk, chunk)
        for sq in range(n_seq):
            g = la_ref[sq, pl.ds(off, chunk), :]
            g_hi = g.astype(BF16)
            g_lo = (g - g_hi.astype(F32)).astype(BF16)
            b = _dot(tri, g_hi) + _dot(tri, g_lo)
            b_last = b[chunk - 1:chunk, :]
            q = q_ref[sq, pl.ds(off, chunk), :]
            k = k_ref[sq, pl.ds(off, chunk), :]
            v = v_ref[sq, pl.ds(off, chunk), :].astype(BF16)
            qe = stack_heads((q * jnp.exp(b)).astype(BF16))
            ke = (k * jnp.exp(-b)).astype(BF16)
            kd = stack_heads((k * jnp.exp(b_last - b)).astype(BF16))
            st = st_sc[sq]
            a = jnp.where(causal_stacked, _dot_nt(qe, ke), 0.0).astype(BF16)
            intra = _dot(a, v)
            inter = _dot_nt(qe, st.astype(BF16))
            o_ref[sq, pl.ds(off, chunk), :] = jnp.concatenate(
                [intra[hh * chunk:(hh + 1) * chunk, hh * GLA_DV:(hh + 1) * GLA_DV]
                 + inter[hh * chunk:(hh + 1) * chunk] for hh in range(GLA_HEADS)], axis=1)
            v_stacked = jnp.concatenate([v[:, hh * GLA_DV:(hh + 1) * GLA_DV] for hh in range(GLA_HEADS)], axis=0)
            st_sc[sq] = st * jnp.exp(b_last) + _dot_tn(v_stacked, kd)
        return 0

    lax.fori_loop(0, n_chunks, one_chunk, 0, unroll=min(GLA_UNROLL, n_chunks))
    for sq in range(n_seq):
        s_ref[sq] = st_sc[sq].T.reshape(GLA_HEADS, GLA_DK, GLA_DV)


def _gla(gq, gk, gv, la, s0, chunk, seqs_per_step):
    b, t, hk = gq.shape
    hv = gv.shape[2]
    g = seqs_per_step
    blk = lambda w: pl.BlockSpec((g, t, w), lambda i: (i, 0, 0))
    st = pl.BlockSpec((g, GLA_HEADS, GLA_DK, GLA_DV), lambda i: (i, 0, 0, 0))
    body = functools.partial(_gla_body, n_seq=g, n_chunks=t // chunk, chunk=chunk)
    return pl.pallas_call(
        body,
        grid=(b // g,),
        in_specs=[blk(hk), blk(hk), blk(hv), blk(hk), st],
        out_specs=[blk(hv), st],
        out_shape=[jax.ShapeDtypeStruct((b, t, hv), F32),
                   jax.ShapeDtypeStruct((b, GLA_HEADS, GLA_DK, GLA_DV), F32)],
        scratch_shapes=[pltpu.VMEM((g, GLA_DV, GLA_HEADS * GLA_DK), F32)],
        compiler_params=_cparams(("parallel",)),
        name="gla",
    )(gq, gk, gv, la, s0)


def _mix_out_ffn_body(x_ref, mod2_ref, mod3_ref, om_ref, og_ref, gg_ref, go_ref, wlat_ref, wom_ref, wog_ref,
                      g_ref, w1_ref, w3_ref, w2_ref, gf_ref, o_ref, *, d, sample, final):
    gate = mod2_ref[...]
    om = om_ref[...]
    if sample:
        om = _dot(om.astype(BF16), wlat_ref[...]).astype(BF16)
    parts = []
    for hh in range(GLA_HEADS):
        vs = slice(hh * GLA_DV, (hh + 1) * GLA_DV)
        parts.append(_rms(og_ref[:, vs], go_ref[...]) * _silu(gg_ref[:, vs]))
    og = jnp.concatenate(parts, axis=1).astype(BF16)
    mix = _dot(om, wom_ref[...]) + _dot(og, wog_ref[...])
    x = x_ref[...] + gate * mix
    out = _ffn_half_step(x, mod3_ref, g_ref, w1_ref, w3_ref, w2_ref, d)
    if final:
        out = _rms(out, gf_ref[...])
    o_ref[...] = out


def _mix_out_ffn(x, mod, o_mla, o_gla, gg, g_gla_o, w_lat, w_o_mla, w_o_gla, g, w1, w3, w2, g_final,
                 sample, final, tm):
    n, d = x.shape
    ff = w1.shape[1]
    hv = GLA_HEADS * GLA_DV
    tok = lambda w: pl.BlockSpec((tm, w), lambda i: (i, 0))
    body = functools.partial(_mix_out_ffn_body, d=d, sample=sample, final=final)
    return pl.pallas_call(
        body,
        grid=(n // tm,),
        in_specs=[tok(d), mod.spec(tm, d, 5, 1), mod.spec(tm, d, 2), tok(o_mla.shape[1]), tok(hv), tok(hv),
                  _resident((1, GLA_DV)), _resident(w_lat.shape), _resident(w_o_mla.shape),
                  _resident(w_o_gla.shape),
                  _resident((1, d)), _resident((d, ff)), _resident((d, ff)), _resident((ff, d)),
                  _resident((1, d))],
        out_specs=tok(d),
        out_shape=jax.ShapeDtypeStruct((n, d), F32),
        compiler_params=_cparams(("parallel",)),
        name="mix_out_ffn_sample" if sample else "mix_out_ffn_prompt",
    )(x, mod.arr, mod.arr, o_mla, o_gla, gg, g_gla_o.reshape(1, GLA_DV), w_lat, w_o_mla, w_o_gla,
      g.reshape(1, d), w1, w3, w2, g_final.reshape(1, d))


def _prep_w_in(w_in):
    d = w_in.shape[0]
    o = 0
    parts = {}
    for name, size in (("qa", MLA_Q_LORA), ("kva", MLA_KV_LORA), ("kr", MLA_ROPE),
                       ("gq", GLA_HEADS * GLA_DK), ("gk", GLA_HEADS * GLA_DK), ("gv", GLA_HEADS * GLA_DV),
                       ("ga", GLA_GATE_RANK), ("gg", GLA_HEADS * GLA_DV)):
        parts[name] = w_in[:, o:o + size]
        o += size
    blk_a = jnp.pad(jnp.concatenate([parts["kr"], parts["ga"]], axis=1),
                    ((0, 0), (0, LANE - MLA_ROPE - GLA_GATE_RANK)))
    w = jnp.concatenate([parts["qa"], parts["kva"], parts["gq"] * (GLA_DK ** -0.5), parts["gk"],
                         parts["gv"], parts["gg"], blk_a], axis=1)
    return w.astype(BF16)


def _prep_wq(w_qb):
    ql = w_qb.shape[0]
    w = jnp.pad(w_qb, ((0, 0), (0, 0), (0, HEAD_PAD - MLA_NOPE - MLA_ROPE)))
    return (w.reshape(ql, MLA_HEADS * HEAD_PAD) * (MLA_SCALE * LOG2_E)).astype(BF16)


def _prep_wkv_prompt(w_kvb):
    rows = w_kvb.shape[0] + LANE
    k_top = jnp.pad(w_kvb[..., :MLA_NOPE], ((0, 0), (0, 0), (0, HEAD_PAD - MLA_NOPE)))
    sel = jnp.eye(LANE, HEAD_PAD, k=MLA_NOPE, dtype=F32)
    sel = jnp.where(jnp.arange(LANE)[:, None] < MLA_ROPE, sel, 0.0)
    k = jnp.concatenate([k_top, jnp.broadcast_to(sel[:, None, :], (LANE, MLA_HEADS, HEAD_PAD))], axis=0)
    v = jnp.pad(w_kvb[..., MLA_NOPE:], ((0, LANE), (0, 0), (0, 0)))
    return jnp.concatenate([k.reshape(rows, -1), v.reshape(rows, -1)], axis=1).astype(BF16)


def _prep_wq_sample(w_kvb):
    same_head = jnp.eye(MLA_HEADS, dtype=F32)[:, None, :, None]
    pad_n = ((0, 0), (0, HEAD_PAD - MLA_NOPE), (0, 0), (0, 0))
    lat = jnp.pad(jnp.transpose(w_kvb[..., :MLA_NOPE], (1, 2, 0))[:, :, None, :] * same_head, pad_n)
    sel = jnp.eye(HEAD_PAD, MLA_ROPE, k=-MLA_NOPE, dtype=F32)
    rope = sel[None, :, None, :] * same_head
    rows = MLA_HEADS * HEAD_PAD
    return jnp.concatenate([lat.reshape(rows, -1), rope.reshape(rows, -1)], axis=1).astype(BF16)


def _prep_w_lat_out(w_kvb):
    c = w_kvb.shape[0]
    same_head = jnp.eye(MLA_HEADS, dtype=F32)[:, None, :, None]
    w = jnp.transpose(w_kvb[..., MLA_NOPE:], (1, 0, 2))[:, :, None, :] * same_head
    return w.reshape(MLA_HEADS * c, MLA_HEADS * MLA_V).astype(BF16)


def _prep_w_o(w_o):
    n_mla = MLA_HEADS * MLA_V
    return w_o[:n_mla].astype(BF16), w_o[n_mla:].astype(BF16)


def _prep_gate(w_gate_b):
    return jnp.pad(w_gate_b, ((MLA_ROPE, LANE - MLA_ROPE - GLA_GATE_RANK), (0, 0))).astype(BF16)


def _rope_table(pos):
    hr = MLA_ROPE // 2
    inv = ROPE_THETA ** (-jnp.arange(0, MLA_ROPE, 2, dtype=F32) / MLA_ROPE)
    ang = pos[:, None] * inv[None, :]
    cos, sin = jnp.cos(ang), jnp.sin(ang)
    t = pos.shape[0]
    z = lambda w: jnp.zeros((t, w), F32)

    def tabs(start):
        after = LANE - start - 2 * hr
        c = jnp.concatenate([jnp.ones((t, start), F32), cos, cos, z(after)], axis=1)
        s_from_below = jnp.concatenate([z(start + hr), sin, z(after)], axis=1)
        s_from_above = jnp.concatenate([z(start), -sin, z(hr + after)], axis=1)
        return [c, s_from_below, s_from_above]

    return jnp.concatenate(tabs(MLA_NOPE) + tabs(0), axis=1)


TM_TOKENS = 512
TM_SAMPLE = 256
TQ_PROMPT = 512
HEAD_PAIRS_PER_STEP = 4
PAGES_PER_CHUNK = 32
GLA_SAMPLE_PAD = 16
GLA_SAMPLE_SEQS = 8
GLA_UNROLL = 8


def kernel(x_prompt, x_sample, cache_ckv, cache_krope, state_gla, page_table, c_prompt, c_sample,
           w_ada, b_ada, norm_ffn1, ffn1_w1, ffn1_w3, ffn1_w2, norm_mix, w_in, g_qa, w_qb, g_kva, w_kvb,
           w_gate_b, b_gate, g_gla_o, w_o, norm_ffn2, ffn2_w1, ffn2_w3, ffn2_w2, norm_final):
    bp, tp, d = x_prompt.shape
    bs, ts, _ = x_sample.shape
    depth = w_ada.shape[0]
    n_pages = page_table.shape[1]
    past_len = n_pages * PAGE_SIZE

    tab_p = _rope_table(jnp.arange(tp, dtype=F32))
    tm_s = min(TM_SAMPLE, bs * ts)
    tab_s = jnp.tile(_rope_table(past_len + jnp.arange(ts, dtype=F32)), (tm_s // ts, 1))

    hp = x_prompt.reshape(bp * tp, d)
    hs = x_sample.reshape(bs * ts, d)
    outs = [[] for _ in range(6)]
    for l in range(depth):
        m = _ada(jnp.concatenate([c_prompt, c_sample], axis=0), w_ada[l], b_ada[l])
        mod_p = _Mod(m[:bp], False, tp)
        mod_s = _Mod(jnp.repeat(m[bp:], ts, axis=0), True, ts)

        f1 = (norm_ffn1[l], ffn1_w1[l].astype(BF16), ffn1_w3[l].astype(BF16), ffn1_w2[l].astype(BF16))
        f2 = (norm_ffn2[l], ffn2_w1[l].astype(BF16), ffn2_w3[l].astype(BF16), ffn2_w2[l].astype(BF16))
        w_in_r = _prep_w_in(w_in[l])
        wq = _prep_wq(w_qb[l])
        wkv_p = _prep_wkv_prompt(w_kvb[l])
        wq_s = _prep_wq_sample(w_kvb[l])
        w_lat = _prep_w_lat_out(w_kvb[l])
        w_o_mla, w_o_gla = _prep_w_o(w_o[l])
        wgate = _prep_gate(w_gate_b[l])
        last = l == depth - 1

        hp = _ffn(hp, mod_p, 0, *f1, TM_TOKENS)
        q, k, v, ckv_p, kr_p, gq, gk, gv, la, gg = _mix_in(
            hp, mod_p, norm_mix[l], w_in_r, g_qa[l], wq, g_kva[l], wkv_p, wgate, b_gate[l], tab_p,
            False, TM_TOKENS, bp)
        o_mla = _attn_prompt(q, k, v, TQ_PROMPT, HEAD_PAIRS_PER_STEP).reshape(bp * tp, -1)
        seq3 = lambda a: a.reshape(bp, tp, -1)
        s0_p = jnp.zeros((bp, GLA_HEADS, GLA_DK, GLA_DV), F32)
        o_gla, s_p = _gla(seq3(gq), seq3(gk), seq3(gv), seq3(la), s0_p, GLA_CHUNK, 1)
        hp = _mix_out_ffn(hp, mod_p, o_mla, o_gla.reshape(bp * tp, -1), gg, g_gla_o[l], w_lat, w_o_mla,
                          w_o_gla, *f2, norm_final, False, last, TM_TOKENS)

        hs = _ffn(hs, mod_s, 0, *f1, tm_s)
        ql, ckv_s, kr_s, gq, gk, gv, la, gg = _mix_in(
            hs, mod_s, norm_mix[l], w_in_r, g_qa[l], wq, g_kva[l], wq_s, wgate, b_gate[l], tab_s,
            True, tm_s, bs)
        n_lat = MLA_HEADS * MLA_KV_LORA
        rows = ts * MLA_HEADS
        q_lat = ql[:, :n_lat].reshape(bs, rows, MLA_KV_LORA)
        q_rope = ql[:, n_lat:].reshape(bs, rows, MLA_ROPE)
        o_lat = _attn_sample(page_table, q_lat, q_rope, ckv_s.reshape(bs, ts, -1), kr_s.reshape(bs, ts, -1),
                             cache_ckv, jnp.swapaxes(cache_krope, 2, 3), l, PAGES_PER_CHUNK)
        pad = lambda a: jnp.pad(a.reshape(bs, ts, -1), ((0, 0), (0, GLA_SAMPLE_PAD - ts), (0, 0)))
        o_gla, s_s = _gla(pad(gq), pad(gk), pad(gv), pad(la), state_gla[l], GLA_SAMPLE_PAD, GLA_SAMPLE_SEQS)
        hs = _mix_out_ffn(hs, mod_s, o_lat.reshape(bs * ts, n_lat), o_gla[:, :ts].reshape(bs * ts, -1), gg,
                          g_gla_o[l], w_lat, w_o_mla, w_o_gla, *f2, norm_final, True, last, tm_s)

        for lst, a in zip(outs, (ckv_p.reshape(bp, tp, -1), kr_p.reshape(bp, tp, -1), s_p,
                                 ckv_s.reshape(bs, ts, -1), kr_s.reshape(bs, ts, -1), s_s)):
            lst.append(a)

    return (hp.reshape(bp, tp, d), hs.reshape(bs, ts, d)) + tuple(jnp.stack(o) for o in outs)
```

```python
import functools
import math

import jax
import jax.numpy as jnp
from jax import lax
from jax.experimental import pallas as pl
from jax.experimental.pallas import tpu as pltpu

F32 = jnp.float32
BF16 = jnp.bfloat16

PAGE_SIZE = 128
MLA_HEADS = 8
MLA_NOPE = 64
MLA_ROPE = 32
MLA_V = 64
MLA_Q_LORA = 384
MLA_KV_LORA = 256
MLA_SCALE = (MLA_NOPE + MLA_ROPE) ** -0.5
LOG2_E = math.log2(math.e)
ROPE_THETA = 10000.0
GLA_HEADS = 4
GLA_DK = 64
GLA_DV = 128
GLA_GATE_RANK = 16
GLA_GATE_TAU = 16.0
GLA_CHUNK = 64
N_ADA = 9
EPS = 1e-6

LANE = 128
HEAD_PAD = 128
VMEM_LIMIT_BYTES = 56 * 1024 * 1024

C_QA = 0
C_KVA = C_QA + MLA_Q_LORA
C_GQ = C_KVA + MLA_KV_LORA
C_GK = C_GQ + GLA_HEADS * GLA_DK
C_GV = C_GK + GLA_HEADS * GLA_DK
C_GG = C_GV + GLA_HEADS * GLA_DV
C_BLKA = C_GG + GLA_HEADS * GLA_DV
C_END = C_BLKA + LANE
N_ROPE_TABS = 6


def _cparams(sem):
    return pltpu.CompilerParams(dimension_semantics=sem, vmem_limit_bytes=VMEM_LIMIT_BYTES)


def _resident(shape):
    return pl.BlockSpec(shape, lambda *_: (0,) * len(shape), pipeline_mode=pl.Buffered(1))


def _rms(x, g):
    return x * lax.rsqrt(jnp.mean(x * x, axis=-1, keepdims=True) + EPS) * g


def _silu(x):
    return x * jax.nn.sigmoid(x)


def _dot(a, b):
    return jnp.dot(a, b, preferred_element_type=F32)


def _dot_nt(a, b):
    return lax.dot_general(a, b, (((1,), (1,)), ((), ())), preferred_element_type=F32)


def _dot_tn(a, b):
    return lax.dot_general(a, b, (((0,), (0,)), ((), ())), preferred_element_type=F32)


def _ada_body(c_ref, w_ref, b_ref, o_ref):
    c = _silu(c_ref[...]).astype(BF16)
    o_ref[...] = _dot(c, w_ref[...].astype(BF16)) + b_ref[...]


def _ada(c_all, w_ada, b_ada):
    n, d = c_all.shape
    n_out = w_ada.shape[1]
    tn = 1024
    return pl.pallas_call(
        _ada_body,
        grid=(n_out // tn,),
        in_specs=[pl.BlockSpec((n, d), lambda j: (0, 0)),
                  pl.BlockSpec((d, tn), lambda j: (0, j)),
                  pl.BlockSpec((1, tn), lambda j: (0, j))],
        out_specs=pl.BlockSpec((n, tn), lambda j: (0, j)),
        out_shape=jax.ShapeDtypeStruct((n, n_out), F32),
        compiler_params=_cparams(("arbitrary",)),
        name="ada",
    )(c_all, w_ada, b_ada.reshape(1, n_out))


class _Mod:
    def __init__(self, m, per_token, tokens_per_seq):
        self.per_token = per_token
        self.tokens_per_seq = tokens_per_seq
        self.arr = m if per_token else m.reshape(m.shape[0], 1, m.shape[1])

    def spec(self, tm, d, blk, width=3):
        if self.per_token:
            return pl.BlockSpec((tm, width * d), lambda i: (i, blk))
        per = self.tokens_per_seq // tm
        return pl.BlockSpec((None, 1, width * d), lambda i: (i // per, 0, blk))


FF_STEP = 1024


def _ff_chunks(ff):
    return tuple((c, min(c + FF_STEP, ff)) for c in range(0, ff, FF_STEP))


def _ffn_half_step(x, mod_ref, g_ref, w1_ref, w3_ref, w2_ref, d):
    shift, scale, gate = mod_ref[:, 0:d], mod_ref[:, d:2 * d], mod_ref[:, 2 * d:3 * d]
    h = (_rms(x, g_ref[...]) * (1.0 + scale) + shift).astype(BF16)
    y = None
    for c0, c1 in _ff_chunks(w1_ref.shape[1]):
        a = _dot(h, w1_ref[:, c0:c1])
        b = _dot(h, w3_ref[:, c0:c1])
        u = (_silu(a) * b).astype(BF16)
        part = _dot(u, w2_ref[c0:c1, :])
        y = part if y is None else y + part
    return x + 0.5 * gate * y


def _ffn_body(x_ref, mod_ref, g_ref, w1_ref, w3_ref, w2_ref, o_ref, *, d):
    o_ref[...] = _ffn_half_step(x_ref[...], mod_ref, g_ref, w1_ref, w3_ref, w2_ref, d)


def _ffn(x, mod, third, g, w1, w3, w2, tm):
    n, d = x.shape
    ff = w1.shape[1]
    return pl.pallas_call(
        functools.partial(_ffn_body, d=d),
        grid=(n // tm,),
        in_specs=[pl.BlockSpec((tm, d), lambda i: (i, 0)),
                  mod.spec(tm, d, third),
                  _resident((1, d)), _resident((d, ff)), _resident((d, ff)), _resident((ff, d))],
        out_specs=pl.BlockSpec((tm, d), lambda i: (i, 0)),
        out_shape=jax.ShapeDtypeStruct((n, d), F32),
        compiler_params=_cparams(("parallel",)),
        name="ffn",
    )(x, mod.arr, g.reshape(1, d), w1, w3, w2)


def _mix_in_body(x_ref, mod_ref, g_ref, win_ref, gqa_ref, wq_ref, gkva_ref, wkv_ref, wgate_ref,
                 bgate_ref, tab_ref, *out_refs, d, sample):
    if sample:
        (ql_ref, ckv_ref, kr_ref, gq_ref, gk_ref, gv_ref, la_ref, gg_ref) = out_refs
    else:
        (q_ref, k_ref, v_ref, ckv_ref, kr_ref, gq_ref, gk_ref, gv_ref, la_ref, gg_ref) = out_refs
    x = x_ref[...]
    shift, scale = mod_ref[:, 0:d], mod_ref[:, d:2 * d]
    h = (_rms(x, g_ref[...]) * (1.0 + scale) + shift).astype(BF16)
    proj = _dot(h, win_ref[...])

    q_tabs = tuple(tab_ref[:, i * LANE:(i + 1) * LANE] for i in range(3))
    k_tabs = tuple(tab_ref[:, i * LANE:(i + 1) * LANE] for i in range(3, 6))
    hr = MLA_ROPE // 2

    def rotate(x, tabs):
        c, s_from_below, s_from_above = tabs
        return x * c + pltpu.roll(x, hr, 1) * s_from_below + pltpu.roll(x, LANE - hr, 1) * s_from_above

    qn = _rms(proj[:, C_QA:C_KVA], gqa_ref[...]).astype(BF16)
    qq = _dot(qn, wq_ref[...])
    half = MLA_HEADS * HEAD_PAD
    q_heads = [rotate(qq[:, hh * HEAD_PAD:(hh + 1) * HEAD_PAD], q_tabs) for hh in range(MLA_HEADS)]

    ckv = _rms(proj[:, C_KVA:C_GQ], gkva_ref[...])
    ckv_ref[...] = ckv
    blk_a = proj[:, C_BLKA:C_END]
    kr_blk = rotate(blk_a, k_tabs)
    kr_ref[...] = kr_blk[:, 0:MLA_ROPE]

    if sample:
        q_all = jnp.concatenate(q_heads, axis=1).astype(BF16)
        ql_ref[...] = _dot(q_all, wkv_ref[...]).astype(BF16)
    else:
        for hh in range(MLA_HEADS):
            q_ref[hh] = q_heads[hh].astype(BF16)
        lhs = jnp.concatenate([ckv.astype(BF16), kr_blk.astype(BF16)], axis=1)
        kv = _dot(lhs, wkv_ref[...])
        for hh in range(MLA_HEADS):
            k_ref[hh] = kv[:, hh * HEAD_PAD:(hh + 1) * HEAD_PAD].astype(BF16)
        for pr in range(MLA_HEADS * MLA_V // LANE):
            v_ref[pr] = kv[:, half + pr * LANE:half + (pr + 1) * LANE].astype(BF16)

    gq_ref[...] = proj[:, C_GQ:C_GK]
    gk_ref[...] = proj[:, C_GK:C_GV]
    gv_ref[...] = proj[:, C_GV:C_GG]
    gg_ref[...] = proj[:, C_GG:C_BLKA]
    z = _dot(blk_a.astype(BF16), wgate_ref[...]) + bgate_ref[...]
    la_ref[...] = (jnp.minimum(z, 0.0) - jnp.log1p(jnp.exp(-jnp.abs(z)))) / GLA_GATE_TAU


def _mix_in(x, mod, g, w_in_r, g_qa, wq, g_kva, wkv, wgate, b_gate, tab, sample, tm, n_seq):
    n, d = x.shape
    t_seq = n // n_seq
    n_pos_tiles = tab.shape[0] // tm
    hk = GLA_HEADS * GLA_DK
    hv = GLA_HEADS * GLA_DV
    tok = lambda w: pl.BlockSpec((tm, w), lambda i: (i, 0))
    tok_shape = lambda w, dt=F32: jax.ShapeDtypeStruct((n, w), dt)
    gla_specs = [tok(hk), tok(hk), tok(hv), tok(hk), tok(hv)]
    gla_shapes = [tok_shape(hk), tok_shape(hk), tok_shape(hv), tok_shape(hk), tok_shape(hv)]
    if sample:
        wl = wkv.shape[1]
        out_specs = [tok(wl), tok(MLA_KV_LORA), tok(MLA_ROPE)] + gla_specs
        out_shape = [tok_shape(wl, BF16), tok_shape(MLA_KV_LORA), tok_shape(MLA_ROPE)] + gla_shapes
    else:
        per = t_seq // tm
        n_vp = MLA_HEADS * MLA_V // LANE
        head_spec = lambda nh: pl.BlockSpec((None, nh, tm, HEAD_PAD), lambda i: (i // per, 0, i % per, 0))
        head_shape = lambda nh: jax.ShapeDtypeStruct((n_seq, nh, t_seq, HEAD_PAD), BF16)
        out_specs = ([head_spec(MLA_HEADS)] * 2 + [head_spec(n_vp), tok(MLA_KV_LORA), tok(MLA_ROPE)]
                     + gla_specs)
        out_shape = ([head_shape(MLA_HEADS)] * 2 + [head_shape(n_vp), tok_shape(MLA_KV_LORA),
                                                    tok_shape(MLA_ROPE)] + gla_shapes)
    body = functools.partial(_mix_in_body, d=d, sample=sample)
    return pl.pallas_call(
        body,
        grid=(n // tm,),
        in_specs=[pl.BlockSpec((tm, d), lambda i: (i, 0)),
                  mod.spec(tm, d, 1),
                  _resident((1, d)), _resident(w_in_r.shape), _resident((1, MLA_Q_LORA)),
                  _resident(wq.shape), _resident((1, MLA_KV_LORA)), _resident(wkv.shape),
                  _resident(wgate.shape), _resident((1, hk)),
                  pl.BlockSpec((tm, N_ROPE_TABS * LANE), lambda i: (i % n_pos_tiles, 0))],
        out_specs=out_specs,
        out_shape=out_shape,
        compiler_params=_cparams(("parallel",)),
        name="mix_in_sample" if sample else "mix_in_prompt",
    )(x, mod.arr, g.reshape(1, d), w_in_r, g_qa.reshape(1, -1), wq, g_kva.reshape(1, -1), wkv,
      wgate, b_gate.reshape(1, hk), tab)


def _attn_prompt_body(q_ref, k_ref, v_ref, o_ref, *, tq, n_pairs):
    n_heads = 2 * n_pairs
    i = pl.program_id(2)
    qs = [q_ref[hh] for hh in range(n_heads)]

    def scores(hh, j):
        off = pl.multiple_of(j * tq, tq)
        return _dot_nt(qs[hh], k_ref[hh, pl.ds(off, tq), :]), v_ref[hh // 2, pl.ds(off, tq), :]

    init = []
    for hh in range(n_heads):
        s, v = scores(hh, i)
        row = lax.broadcasted_iota(jnp.int32, s.shape, 0)
        col = lax.broadcasted_iota(jnp.int32, s.shape, 1)
        s = jnp.where(col <= row, s, jnp.finfo(F32).min)
        m = jnp.max(s, axis=-1, keepdims=True)
        p = jnp.exp2(s - m)
        init += [m, jnp.sum(p, axis=-1, keepdims=True), _dot(p.astype(BF16), v)]

    def step(j, carry):
        out = []
        for hh in range(n_heads):
            m, l, acc = carry[3 * hh:3 * hh + 3]
            s, v = scores(hh, j)
            m_new = jnp.maximum(m, jnp.max(s, axis=-1, keepdims=True))
            alpha = jnp.exp2(m - m_new)
            p = jnp.exp2(s - m_new)
            out += [m_new, alpha * l + jnp.sum(p, axis=-1, keepdims=True),
                    alpha * acc + _dot(p.astype(BF16), v)]
        return tuple(out)

    fin = lax.fori_loop(0, i, step, tuple(init))
    for pr in range(n_pairs):
        e, o = 6 * pr, 6 * pr + 3
        o_even, o_odd = fin[e + 2] / fin[e + 1], fin[o + 2] / fin[o + 1]
        lane = lax.broadcasted_iota(jnp.int32, o_even.shape, 1)
        o_ref[:, pr * LANE:(pr + 1) * LANE] = jnp.where(lane < MLA_V, o_even, o_odd).astype(o_ref.dtype)


def _attn_prompt(q, k, v, tq, n_pairs):
    b, hh, t, w = q.shape
    nh = 2 * n_pairs
    body = functools.partial(_attn_prompt_body, tq=tq, n_pairs=n_pairs)
    return pl.pallas_call(
        body,
        grid=(b, hh // nh, t // tq),
        in_specs=[pl.BlockSpec((None, nh, tq, w), lambda bi, hi, i: (bi, hi, i, 0)),
                  pl.BlockSpec((None, nh, t, w), lambda bi, hi, i: (bi, hi, 0, 0)),
                  pl.BlockSpec((None, n_pairs, t, w), lambda bi, hi, i: (bi, hi, 0, 0))],
        out_specs=pl.BlockSpec((None, tq, n_pairs * w), lambda bi, hi, i: (bi, i, hi)),
        out_shape=jax.ShapeDtypeStruct((b, t, hh // 2 * w), BF16),
        compiler_params=_cparams(("parallel", "parallel", "arbitrary")),
        name="attn_prompt",
    )(q, k, v)


def _attn_sample_body(pt_ref, ql_ref, qr_ref, cn_ref, kn_ref, ckv_hbm, kr_hbm, o_ref, ckv_buf, kr_buf, sems,
                      *, layer, n_chunks, pg):
    b = pl.program_id(0)
    n_seq = pl.num_programs(0)

    def copies(seq, c):
        out = []
        for p in range(pg):
            page = pt_ref[seq, c * pg + p]
            rows = pl.ds(p * PAGE_SIZE, PAGE_SIZE)
            out.append(pltpu.make_async_copy(ckv_hbm.at[layer, page], ckv_buf.at[c, rows, :], sems.at[c, 0]))
            out.append(pltpu.make_async_copy(kr_hbm.at[layer, page], kr_buf.at[c, :, rows], sems.at[c, 1]))
        return out

    @pl.when(b == 0)
    def _():
        for c in range(n_chunks):
            for cp in copies(0, c):
                cp.start()

    ql = ql_ref[...]
    qr = qr_ref[...]

    cn = cn_ref[...]
    s = _dot_nt(ql.astype(F32), cn) + _dot_nt(qr.astype(F32), kn_ref[...])
    row_tok = lax.broadcasted_iota(jnp.int32, s.shape, 0) // MLA_HEADS
    col = lax.broadcasted_iota(jnp.int32, s.shape, 1)
    s = jnp.where(col <= row_tok, s, jnp.finfo(F32).min)
    m = jnp.max(s, axis=-1, keepdims=True)
    p = jnp.exp2(s - m)
    l = jnp.sum(p, axis=-1, keepdims=True)
    acc = _dot(p, cn)

    nxt = jnp.minimum(b + 1, n_seq - 1)

    def slot_filled(c):
        return (pltpu.make_async_copy(ckv_buf.at[c], ckv_buf.at[c], sems.at[c, 0]),
                pltpu.make_async_copy(kr_buf.at[c], kr_buf.at[c], sems.at[c, 1]))

    def scores(c):
        for cp in slot_filled(c):
            cp.wait()
        ck = ckv_buf[c].astype(BF16)
        return _dot_nt(ql, ck) + _dot(qr, kr_buf[c].astype(BF16)), ck

    s_next, ck_next = scores(0)
    for c in range(n_chunks):
        s, ck = s_next, ck_next
        if c + 1 < n_chunks:
            s_next, ck_next = scores(c + 1)
        m_new = jnp.maximum(m, jnp.max(s, axis=-1, keepdims=True))
        alpha = jnp.exp2(m - m_new)
        p = jnp.exp2(s - m_new)
        l = alpha * l + jnp.sum(p, axis=-1, keepdims=True)
        acc = alpha * acc + _dot(p.astype(BF16), ck)
        m = m_new
        for cp in copies(nxt, c):
            cp.start()
    o_ref[...] = acc / l

    @pl.when(b == n_seq - 1)
    def _():
        for c in range(n_chunks):
            for cp in slot_filled(c):
                cp.wait()


def _attn_sample(page_table, q_lat, q_rope, ckv_new, kr_new, cache_ckv, cache_krope_t, layer, pg):
    n_seq, rows, c = q_lat.shape
    t_new = ckv_new.shape[1]
    r = q_rope.shape[2]
    n_chunks = page_table.shape[1] // pg
    seq = lambda shape: pl.BlockSpec((None,) + shape, lambda b, pt: (b, 0, 0))
    body = functools.partial(_attn_sample_body, layer=layer, n_chunks=n_chunks, pg=pg)
    grid_spec = pltpu.PrefetchScalarGridSpec(
        num_scalar_prefetch=1,
        grid=(n_seq,),
        in_specs=[seq((rows, c)), seq((rows, r)), seq((t_new, c)), seq((t_new, r)),
                  pl.BlockSpec(memory_space=pl.ANY), pl.BlockSpec(memory_space=pl.ANY)],
        out_specs=seq((rows, c)),
        scratch_shapes=[pltpu.VMEM((n_chunks, pg * PAGE_SIZE, c), F32),
                        pltpu.VMEM((n_chunks, r, pg * PAGE_SIZE), F32),
                        pltpu.SemaphoreType.DMA((n_chunks, 2))],
    )
    return pl.pallas_call(
        body,
        grid_spec=grid_spec,
        out_shape=jax.ShapeDtypeStruct((n_seq, rows, c), F32),
        compiler_params=_cparams(("arbitrary",)),
        name="attn_sample",
    )(page_table, q_lat, q_rope, ckv_new, kr_new, cache_ckv, cache_krope_t)


def _gla_body(q_ref, k_ref, v_ref, la_ref, s0_ref, o_ref, s_ref, st_sc, *, n_seq, n_chunks, chunk):
    hk = GLA_HEADS * GLA_DK
    for sq in range(n_seq):
        st_sc[sq] = s0_ref[sq].reshape(hk, GLA_DV).T
    rows = GLA_HEADS * chunk
    r = lax.broadcasted_iota(jnp.int32, (chunk, chunk), 0)
    c = lax.broadcasted_iota(jnp.int32, (chunk, chunk), 1)
    tri = (c <= r).astype(BF16)
    rs = lax.broadcasted_iota(jnp.int32, (rows, chunk), 0)
    cs = lax.broadcasted_iota(jnp.int32, (rows, chunk), 1)
    causal_stacked = cs <= rs % chunk
    lane_head = lax.broadcasted_iota(jnp.int32, (chunk, hk), 1) // GLA_DK
    zero = jnp.zeros((chunk, hk), BF16)

    def stack_heads(x):
        return jnp.concatenate([jnp.where(lane_head == hh, x, zero) for hh in range(GLA_HEADS)], axis=0)

    def one_chunk(ci, _):
        off = pl.multiple_of(ci * chunk, chunk)
        for sq in range(n_seq):
            g = la_ref[sq, pl.ds(off, chunk), :]
            g_hi = g.astype(BF16)
            g_lo = (g - g_hi.astype(F32)).astype(BF16)
            b = _dot(tri, g_hi) + _dot(tri, g_lo)
            b_last = b[chunk - 1:chunk, :]
            q = q_ref[sq, pl.ds(off, chunk), :]
            k = k_ref[sq, pl.ds(off, chunk), :]
            v = v_ref[sq, pl.ds(off, chunk), :].astype(BF16)
            qe = stack_heads((q * jnp.exp(b)).astype(BF16))
            ke = (k * jnp.exp(-b)).astype(BF16)
            kd = stack_heads((k * jnp.exp(b_last - b)).astype(BF16))
            st = st_sc[sq]
            a = jnp.where(causal_stacked, _dot_nt(qe, ke), 0.0).astype(BF16)
            intra = _dot(a, v)
            inter = _dot_nt(qe, st.astype(BF16))
            o_ref[sq, pl.ds(off, chunk), :] = jnp.concatenate(
                [intra[hh * chunk:(hh + 1) * chunk, hh * GLA_DV:(hh + 1) * GLA_DV]
                 + inter[hh * chunk:(hh + 1) * chunk] for hh in range(GLA_HEADS)], axis=1)
            v_stacked = jnp.concatenate([v[:, hh * GLA_DV:(hh + 1) * GLA_DV] for hh in range(GLA_HEADS)], axis=0)
            st_sc[sq] = st * jnp.exp(b_last) + _dot_tn(v_stacked, kd)
        return 0

    lax.fori_loop(0, n_chunks, one_chunk, 0, unroll=min(GLA_UNROLL, n_chunks))
    for sq in range(n_seq):
        s_ref[sq] = st_sc[sq].T.reshape(GLA_HEADS, GLA_DK, GLA_DV)


def _gla(gq, gk, gv, la, s0, chunk, seqs_per_step):
    b, t, hk = gq.shape
    hv = gv.shape[2]
    g = seqs_per_step
    blk = lambda w: pl.BlockSpec((g, t, w), lambda i: (i, 0, 0))
    st = pl.BlockSpec((g, GLA_HEADS, GLA_DK, GLA_DV), lambda i: (i, 0, 0, 0))
    body = functools.partial(_gla_body, n_seq=g, n_chunks=t // chunk, chunk=chunk)
    return pl.pallas_call(
        body,
        grid=(b // g,),
        in_specs=[blk(hk), blk(hk), blk(hv), blk(hk), st],
        out_specs=[blk(hv), st],
        out_shape=[jax.ShapeDtypeStruct((b, t, hv), F32),
                   jax.ShapeDtypeStruct((b, GLA_HEADS, GLA_DK, GLA_DV), F32)],
        scratch_shapes=[pltpu.VMEM((g, GLA_DV, GLA_HEADS * GLA_DK), F32)],
        compiler_params=_cparams(("parallel",)),
        name="gla",
    )(gq, gk, gv, la, s0)


def _mix_out_ffn_body(x_ref, mod2_ref, mod3_ref, om_ref, og_ref, gg_ref, go_ref, wlat_ref, wom_ref, wog_ref,
                      g_ref, w1_ref, w3_ref, w2_ref, gf_ref, o_ref, *, d, sample, final):
    gate = mod2_ref[...]
    om = om_ref[...]
    if sample:
        om = _dot(om.astype(BF16), wlat_ref[...]).astype(BF16)
    parts = []
    for hh in range(GLA_HEADS):
        vs = slice(hh * GLA_DV, (hh + 1) * GLA_DV)
        parts.append(_rms(og_ref[:, vs], go_ref[...]) * _silu(gg_ref[:, vs]))
    og = jnp.concatenate(parts, axis=1).astype(BF16)
    mix = _dot(om, wom_ref[...]) + _dot(og, wog_ref[...])
    x = x_ref[...] + gate * mix
    out = _ffn_half_step(x, mod3_ref, g_ref, w1_ref, w3_ref, w2_ref, d)
    if final:
        out = _rms(out, gf_ref[...])
    o_ref[...] = out


def _mix_out_ffn(x, mod, o_mla, o_gla, gg, g_gla_o, w_lat, w_o_mla, w_o_gla, g, w1, w3, w2, g_final,
                 sample, final, tm):
    n, d = x.shape
    ff = w1.shape[1]
    hv = GLA_HEADS * GLA_DV
    tok = lambda w: pl.BlockSpec((tm, w), lambda i: (i, 0))
    body = functools.partial(_mix_out_ffn_body, d=d, sample=sample, final=final)
    return pl.pallas_call(
        body,
        grid=(n // tm,),
        in_specs=[tok(d), mod.spec(tm, d, 5, 1), mod.spec(tm, d, 2), tok(o_mla.shape[1]), tok(hv), tok(hv),
                  _resident((1, GLA_DV)), _resident(w_lat.shape), _resident(w_o_mla.shape),
                  _resident(w_o_gla.shape),
                  _resident((1, d)), _resident((d, ff)), _resident((d, ff)), _resident((ff, d)),
                  _resident((1, d))],
        out_specs=tok(d),
        out_shape=jax.ShapeDtypeStruct((n, d), F32),
        compiler_params=_cparams(("parallel",)),
        name="mix_out_ffn_sample" if sample else "mix_out_ffn_prompt",
    )(x, mod.arr, mod.arr, o_mla, o_gla, gg, g_gla_o.reshape(1, GLA_DV), w_lat, w_o_mla, w_o_gla,
      g.reshape(1, d), w1, w3, w2, g_final.reshape(1, d))


def _prep_w_in(w_in):
    d = w_in.shape[0]
    o = 0
    parts = {}
    for name, size in (("qa", MLA_Q_LORA), ("kva", MLA_KV_LORA), ("kr", MLA_ROPE),
                       ("gq", GLA_HEADS * GLA_DK), ("gk", GLA_HEADS * GLA_DK), ("gv", GLA_HEADS * GLA_DV),
                       ("ga", GLA_GATE_RANK), ("gg", GLA_HEADS * GLA_DV)):
        parts[name] = w_in[:, o:o + size]
        o += size
    blk_a = jnp.pad(jnp.concatenate([parts["kr"], parts["ga"]], axis=1),
                    ((0, 0), (0, LANE - MLA_ROPE - GLA_GATE_RANK)))
    w = jnp.concatenate([parts["qa"], parts["kva"], parts["gq"] * (GLA_DK ** -0.5), parts["gk"],
                         parts["gv"], parts["gg"], blk_a], axis=1)
    return w.astype(BF16)


def _prep_wq(w_qb):
    ql = w_qb.shape[0]
    w = jnp.pad(w_qb, ((0, 0), (0, 0), (0, HEAD_PAD - MLA_NOPE - MLA_ROPE)))
    return (w.reshape(ql, MLA_HEADS * HEAD_PAD) * (MLA_SCALE * LOG2_E)).astype(BF16)


def _prep_wkv_prompt(w_kvb):
    rows = w_kvb.shape[0] + LANE
    k_top = jnp.pad(w_kvb[..., :MLA_NOPE], ((0, 0), (0, 0), (0, HEAD_PAD - MLA_NOPE)))
    sel = jnp.eye(LANE, HEAD_PAD, k=MLA_NOPE, dtype=F32)
    sel = jnp.where(jnp.arange(LANE)[:, None] < MLA_ROPE, sel, 0.0)
    k = jnp.concatenate([k_top, jnp.broadcast_to(sel[:, None, :], (LANE, MLA_HEADS, HEAD_PAD))], axis=0)
    v = jnp.pad(w_kvb[..., MLA_NOPE:], ((0, LANE), (0, 0), (0, 0)))
    return jnp.concatenate([k.reshape(rows, -1), v.reshape(rows, -1)], axis=1).astype(BF16)


def _prep_wq_sample(w_kvb):
    same_head = jnp.eye(MLA_HEADS, dtype=F32)[:, None, :, None]
    pad_n = ((0, 0), (0, HEAD_PAD - MLA_NOPE), (0, 0), (0, 0))
    lat = jnp.pad(jnp.transpose(w_kvb[..., :MLA_NOPE], (1, 2, 0))[:, :, None, :] * same_head, pad_n)
    sel = jnp.eye(HEAD_PAD, MLA_ROPE, k=-MLA_NOPE, dtype=F32)
    rope = sel[None, :, None, :] * same_head
    rows = MLA_HEADS * HEAD_PAD
    return jnp.concatenate([lat.reshape(rows, -1), rope.reshape(rows, -1)], axis=1).astype(BF16)


def _prep_w_lat_out(w_kvb):
    c = w_kvb.shape[0]
    same_head = jnp.eye(MLA_HEADS, dtype=F32)[:, None, :, None]
    w = jnp.transpose(w_kvb[..., MLA_NOPE:], (1, 0, 2))[:, :, None, :] * same_head
    return w.reshape(MLA_HEADS * c, MLA_HEADS * MLA_V).astype(BF16)


def _prep_w_o(w_o):
    n_mla = MLA_HEADS * MLA_V
    return w_o[:n_mla].astype(BF16), w_o[n_mla:].astype(BF16)


def _prep_gate(w_gate_b):
    return jnp.pad(w_gate_b, ((MLA_ROPE, LANE - MLA_ROPE - GLA_GATE_RANK), (0, 0))).astype(BF16)


def _rope_table(pos):
    hr = MLA_ROPE // 2
    inv = ROPE_THETA ** (-jnp.arange(0, MLA_ROPE, 2, dtype=F32) / MLA_ROPE)
    ang = pos[:, None] * inv[None, :]
    cos, sin = jnp.cos(ang), jnp.sin(ang)
    t = pos.shape[0]
    z = lambda w: jnp.zeros((t, w), F32)

    def tabs(start):
        after = LANE - start - 2 * hr
        c = jnp.concatenate([jnp.ones((t, start), F32), cos, cos, z(after)], axis=1)
        s_from_below = jnp.concatenate([z(start + hr), sin, z(after)], axis=1)
        s_from_above = jnp.concatenate([z(start), -sin, z(hr + after)], axis=1)
        return [c, s_from_below, s_from_above]

    return jnp.concatenate(tabs(MLA_NOPE) + tabs(0), axis=1)


TM_TOKENS = 512
TM_FFN = 1024
TM_SAMPLE = 256
TQ_PROMPT = 512
HEAD_PAIRS_PER_STEP = 4
PAGES_PER_CHUNK = 64
GLA_SAMPLE_PAD = 16
GLA_SAMPLE_SEQS = 8
GLA_UNROLL = 8


def kernel(x_prompt, x_sample, cache_ckv, cache_krope, state_gla, page_table, c_prompt, c_sample,
           w_ada, b_ada, norm_ffn1, ffn1_w1, ffn1_w3, ffn1_w2, norm_mix, w_in, g_qa, w_qb, g_kva, w_kvb,
           w_gate_b, b_gate, g_gla_o, w_o, norm_ffn2, ffn2_w1, ffn2_w3, ffn2_w2, norm_final):
    bp, tp, d = x_prompt.shape
    bs, ts, _ = x_sample.shape
    depth = w_ada.shape[0]
    n_pages = page_table.shape[1]
    past_len = n_pages * PAGE_SIZE

    tab_p = _rope_table(jnp.arange(tp, dtype=F32))
    tm_s = min(TM_SAMPLE, bs * ts)
    tab_s = jnp.tile(_rope_table(past_len + jnp.arange(ts, dtype=F32)), (tm_s // ts, 1))

    hp = x_prompt.reshape(bp * tp, d)
    hs = x_sample.reshape(bs * ts, d)
    outs = [[] for _ in range(6)]
    for l in range(depth):
        m = _ada(jnp.concatenate([c_prompt, c_sample], axis=0), w_ada[l], b_ada[l])
        mod_p = _Mod(m[:bp], False, tp)
        mod_s = _Mod(jnp.repeat(m[bp:], ts, axis=0), True, ts)

        f1 = (norm_ffn1[l], ffn1_w1[l].astype(BF16), ffn1_w3[l].astype(BF16), ffn1_w2[l].astype(BF16))
        f2 = (norm_ffn2[l], ffn2_w1[l].astype(BF16), ffn2_w3[l].astype(BF16), ffn2_w2[l].astype(BF16))
        w_in_r = _prep_w_in(w_in[l])
        wq = _prep_wq(w_qb[l])
        wkv_p = _prep_wkv_prompt(w_kvb[l])
        wq_s = _prep_wq_sample(w_kvb[l])
        w_lat = _prep_w_lat_out(w_kvb[l])
        w_o_mla, w_o_gla = _prep_w_o(w_o[l])
        wgate = _prep_gate(w_gate_b[l])
        last = l == depth - 1

        hp = _ffn(hp, mod_p, 0, *f1, TM_FFN)
        q, k, v, ckv_p, kr_p, gq, gk, gv, la, gg = _mix_in(
            hp, mod_p, norm_mix[l], w_in_r, g_qa[l], wq, g_kva[l], wkv_p, wgate, b_gate[l], tab_p,
            False, TM_TOKENS, bp)
        o_mla = _attn_prompt(q, k, v, TQ_PROMPT, HEAD_PAIRS_PER_STEP).reshape(bp * tp, -1)
        seq3 = lambda a: a.reshape(bp, tp, -1)
        s0_p = jnp.zeros((bp, GLA_HEADS, GLA_DK, GLA_DV), F32)
        o_gla, s_p = _gla(seq3(gq), seq3(gk), seq3(gv), seq3(la), s0_p, GLA_CHUNK, 1)
        hp = _mix_out_ffn(hp, mod_p, o_mla, o_gla.reshape(bp * tp, -1), gg, g_gla_o[l], w_lat, w_o_mla,
                          w_o_gla, *f2, norm_final, False, last, TM_TOKENS)

        hs = _ffn(hs, mod_s, 0, *f1, tm_s)
        ql, ckv_s, kr_s, gq, gk, gv, la, gg = _mix_in(
            hs, mod_s, norm_mix[l], w_in_r, g_qa[l], wq, g_kva[l], wq_s, wgate, b_gate[l], tab_s,
            True, tm_s, bs)
        n_lat = MLA_HEADS * MLA_KV_LORA
        rows = ts * MLA_HEADS
        q_lat = ql[:, :n_lat].reshape(bs, rows, MLA_KV_LORA)
        q_rope = ql[:, n_lat:].reshape(bs, rows, MLA_ROPE)
        o_lat = _attn_sample(page_table, q_lat, q_rope, ckv_s.reshape(bs, ts, -1), kr_s.reshape(bs, ts, -1),
                             cache_ckv, jnp.swapaxes(cache_krope, 2, 3), l, PAGES_PER_CHUNK)
        pad = lambda a: jnp.pad(a.reshape(bs, ts, -1), ((0, 0), (0, GLA_SAMPLE_PAD - ts), (0, 0)))
        o_gla, s_s = _gla(pad(gq), pad(gk), pad(gv), pad(la), state_gla[l], GLA_SAMPLE_PAD, GLA_SAMPLE_SEQS)
        hs = _mix_out_ffn(hs, mod_s, o_lat.reshape(bs * ts, n_lat), o_gla[:, :ts].reshape(bs * ts, -1), gg,
                          g_gla_o[l], w_lat, w_o_mla, w_o_gla, *f2, norm_final, True, last, tm_s)

        for lst, a in zip(outs, (ckv_p.reshape(bp, tp, -1), kr_p.reshape(bp, tp, -1), s_p,
                                 ckv_s.reshape(bs, ts, -1), kr_s.reshape(bs, ts, -1), s_s)):
            lst.append(a)

    return (hp.reshape(bp, tp, d), hs.reshape(bs, ts, d)) + tuple(jnp.stack(o) for o in outs)
```

```python
import functools
import math

import jax
import jax.numpy as jnp
from jax import lax
from jax.experimental import pallas as pl
from jax.experimental.pallas import tpu as pltpu

F32 = jnp.float32
BF16 = jnp.bfloat16

PAGE_SIZE = 128
MLA_HEADS = 8
MLA_NOPE = 64
MLA_ROPE = 32
MLA_V = 64
MLA_Q_LORA = 384
MLA_KV_LORA = 256
MLA_SCALE = (MLA_NOPE + MLA_ROPE) ** -0.5
LOG2_E = math.log2(math.e)
ROPE_THETA = 10000.0
GLA_HEADS = 4
GLA_DK = 64
GLA_DV = 128
GLA_GATE_RANK = 16
GLA_GATE_TAU = 16.0
GLA_CHUNK = 64
N_ADA = 9
EPS = 1e-6

LANE = 128
HEAD_PAD = 128
VMEM_LIMIT_BYTES = 56 * 1024 * 1024

C_QA = 0
C_KVA = C_QA + MLA_Q_LORA
C_GQ = C_KVA + MLA_KV_LORA
C_GK = C_GQ + GLA_HEADS * GLA_DK
C_GV = C_GK + GLA_HEADS * GLA_DK
C_GG = C_GV + GLA_HEADS * GLA_DV
C_BLKA = C_GG + GLA_HEADS * GLA_DV
C_END = C_BLKA + LANE
N_ROPE_TABS = 6


def _cparams(sem):
    return pltpu.CompilerParams(dimension_semantics=sem, vmem_limit_bytes=VMEM_LIMIT_BYTES)


def _resident(shape):
    return pl.BlockSpec(shape, lambda *_: (0,) * len(shape), pipeline_mode=pl.Buffered(1))


def _rms(x, g):
    return x * lax.rsqrt(jnp.mean(x * x, axis=-1, keepdims=True) + EPS) * g


def _silu(x):
    return x * jax.nn.sigmoid(x)


def _dot(a, b):
    return jnp.dot(a, b, preferred_element_type=F32)


def _dot_nt(a, b):
    return lax.dot_general(a, b, (((1,), (1,)), ((), ())), preferred_element_type=F32)


def _dot_tn(a, b):
    return lax.dot_general(a, b, (((0,), (0,)), ((), ())), preferred_element_type=F32)


def _ada_body(c_ref, w_ref, b_ref, o_ref):
    c = _silu(c_ref[...]).astype(BF16)
    o_ref[...] = _dot(c, w_ref[...].astype(BF16)) + b_ref[...]


def _ada(c_all, w_ada, b_ada):
    n, d = c_all.shape
    n_out = w_ada.shape[1]
    tn = 1024
    return pl.pallas_call(
        _ada_body,
        grid=(n_out // tn,),
        in_specs=[pl.BlockSpec((n, d), lambda j: (0, 0)),
                  pl.BlockSpec((d, tn), lambda j: (0, j)),
                  pl.BlockSpec((1, tn), lambda j: (0, j))],
        out_specs=pl.BlockSpec((n, tn), lambda j: (0, j)),
        out_shape=jax.ShapeDtypeStruct((n, n_out), F32),
        compiler_params=_cparams(("arbitrary",)),
        name="ada",
    )(c_all, w_ada, b_ada.reshape(1, n_out))


class _Mod:
    def __init__(self, m, per_token, tokens_per_seq):
        self.per_token = per_token
        self.tokens_per_seq = tokens_per_seq
        self.arr = m if per_token else m.reshape(m.shape[0], 1, m.shape[1])

    def spec(self, tm, d, blk, width=3):
        if self.per_token:
            return pl.BlockSpec((tm, width * d), lambda i: (i, blk))
        per = self.tokens_per_seq // tm
        return pl.BlockSpec((None, 1, width * d), lambda i: (i // per, 0, blk))


FF_STEP = 1024


def _ff_chunks(ff):
    return tuple((c, min(c + FF_STEP, ff)) for c in range(0, ff, FF_STEP))


def _ffn_half_step(x, mod_ref, g_ref, w1_ref, w3_ref, w2_ref, d):
    shift, scale, gate = mod_ref[:, 0:d], mod_ref[:, d:2 * d], mod_ref[:, 2 * d:3 * d]
    h = (_rms(x, g_ref[...]) * (1.0 + scale) + shift).astype(BF16)
    y = None
    for c0, c1 in _ff_chunks(w1_ref.shape[1]):
        a = _dot(h, w1_ref[:, c0:c1])
        b = _dot(h, w3_ref[:, c0:c1])
        u = (_silu(a) * b).astype(BF16)
        part = _dot(u, w2_ref[c0:c1, :])
        y = part if y is None else y + part
    return x + 0.5 * gate * y


def _ffn_body(x_ref, mod_ref, g_ref, w1_ref, w3_ref, w2_ref, o_ref, *, d):
    o_ref[...] = _ffn_half_step(x_ref[...], mod_ref, g_ref, w1_ref, w3_ref, w2_ref, d)


def _ffn(x, mod, third, g, w1, w3, w2, tm):
    n, d = x.shape
    ff = w1.shape[1]
    return pl.pallas_call(
        functools.partial(_ffn_body, d=d),
        grid=(n // tm,),
        in_specs=[pl.BlockSpec((tm, d), lambda i: (i, 0)),
                  mod.spec(tm, d, third),
                  _resident((1, d)), _resident((d, ff)), _resident((d, ff)), _resident((ff, d))],
        out_specs=pl.BlockSpec((tm, d), lambda i: (i, 0)),
        out_shape=jax.ShapeDtypeStruct((n, d), F32),
        compiler_params=_cparams(("parallel",)),
        name="ffn",
    )(x, mod.arr, g.reshape(1, d), w1, w3, w2)


def _mix_in_body(x_ref, mod_ref, g_ref, win_ref, gqa_ref, wq_ref, gkva_ref, wkv_ref, wgate_ref,
                 bgate_ref, tab_ref, *out_refs, d, sample):
    if sample:
        (ql_ref, ckv_ref, kr_ref, gq_ref, gk_ref, gv_ref, la_ref, gg_ref) = out_refs
    else:
        (q_ref, k_ref, v_ref, ckv_ref, kr_ref, gq_ref, gk_ref, gv_ref, la_ref, gg_ref) = out_refs
    x = x_ref[...]
    shift, scale = mod_ref[:, 0:d], mod_ref[:, d:2 * d]
    h = (_rms(x, g_ref[...]) * (1.0 + scale) + shift).astype(BF16)
    proj = _dot(h, win_ref[...])

    q_tabs = tuple(tab_ref[:, i * LANE:(i + 1) * LANE] for i in range(3))
    k_tabs = tuple(tab_ref[:, i * LANE:(i + 1) * LANE] for i in range(3, 6))
    hr = MLA_ROPE // 2

    def rotate(x, tabs):
        c, s_from_below, s_from_above = tabs
        return x * c + pltpu.roll(x, hr, 1) * s_from_below + pltpu.roll(x, LANE - hr, 1) * s_from_above

    qn = _rms(proj[:, C_QA:C_KVA], gqa_ref[...]).astype(BF16)
    qq = _dot(qn, wq_ref[...])
    half = MLA_HEADS * HEAD_PAD
    q_heads = [rotate(qq[:, hh * HEAD_PAD:(hh + 1) * HEAD_PAD], q_tabs) for hh in range(MLA_HEADS)]

    ckv = _rms(proj[:, C_KVA:C_GQ], gkva_ref[...])
    ckv_ref[...] = ckv
    blk_a = proj[:, C_BLKA:C_END]
    kr_blk = rotate(blk_a, k_tabs)
    kr_ref[...] = kr_blk[:, 0:MLA_ROPE]

    if sample:
        q_all = jnp.concatenate(q_heads, axis=1).astype(BF16)
        ql_ref[...] = _dot(q_all, wkv_ref[...]).astype(BF16)
    else:
        for hh in range(MLA_HEADS):
            q_ref[hh] = q_heads[hh].astype(BF16)
        lhs = jnp.concatenate([ckv.astype(BF16), kr_blk.astype(BF16)], axis=1)
        kv = _dot(lhs, wkv_ref[...])
        for hh in range(MLA_HEADS):
            k_ref[hh] = kv[:, hh * HEAD_PAD:(hh + 1) * HEAD_PAD].astype(BF16)
        for pr in range(MLA_HEADS * MLA_V // LANE):
            v_ref[pr] = kv[:, half + pr * LANE:half + (pr + 1) * LANE].astype(BF16)

    gq_ref[...] = proj[:, C_GQ:C_GK]
    gk_ref[...] = proj[:, C_GK:C_GV]
    gv_ref[...] = proj[:, C_GV:C_GG]
    gg_ref[...] = proj[:, C_GG:C_BLKA]
    z = _dot(blk_a.astype(BF16), wgate_ref[...]) + bgate_ref[...]
    la_ref[...] = (jnp.minimum(z, 0.0) - jnp.log1p(jnp.exp(-jnp.abs(z)))) / GLA_GATE_TAU


def _mix_in(x, mod, g, w_in_r, g_qa, wq, g_kva, wkv, wgate, b_gate, tab, sample, tm, n_seq):
    n, d = x.shape
    t_seq = n // n_seq
    n_pos_tiles = tab.shape[0] // tm
    hk = GLA_HEADS * GLA_DK
    hv = GLA_HEADS * GLA_DV
    tok = lambda w: pl.BlockSpec((tm, w), lambda i: (i, 0))
    tok_shape = lambda w, dt=F32: jax.ShapeDtypeStruct((n, w), dt)
    gla_specs = [tok(hk), tok(hk), tok(hv), tok(hk), tok(hv)]
    gla_shapes = [tok_shape(hk), tok_shape(hk), tok_shape(hv), tok_shape(hk), tok_shape(hv)]
    if sample:
        wl = wkv.shape[1]
        out_specs = [tok(wl), tok(MLA_KV_LORA), tok(MLA_ROPE)] + gla_specs
        out_shape = [tok_shape(wl, BF16), tok_shape(MLA_KV_LORA), tok_shape(MLA_ROPE)] + gla_shapes
    else:
        per = t_seq // tm
        n_vp = MLA_HEADS * MLA_V // LANE
        head_spec = lambda nh: pl.BlockSpec((None, nh, tm, HEAD_PAD), lambda i: (i // per, 0, i % per, 0))
        head_shape = lambda nh: jax.ShapeDtypeStruct((n_seq, nh, t_seq, HEAD_PAD), BF16)
        out_specs = ([head_spec(MLA_HEADS)] * 2 + [head_spec(n_vp), tok(MLA_KV_LORA), tok(MLA_ROPE)]
                     + gla_specs)
        out_shape = ([head_shape(MLA_HEADS)] * 2 + [head_shape(n_vp), tok_shape(MLA_KV_LORA),
                                                    tok_shape(MLA_ROPE)] + gla_shapes)
    body = functools.partial(_mix_in_body, d=d, sample=sample)
    return pl.pallas_call(
        body,
        grid=(n // tm,),
        in_specs=[pl.BlockSpec((tm, d), lambda i: (i, 0)),
                  mod.spec(tm, d, 1),
                  _resident((1, d)), _resident(w_in_r.shape), _resident((1, MLA_Q_LORA)),
                  _resident(wq.shape), _resident((1, MLA_KV_LORA)), _resident(wkv.shape),
                  _resident(wgate.shape), _resident((1, hk)),
                  pl.BlockSpec((tm, N_ROPE_TABS * LANE), lambda i: (i % n_pos_tiles, 0))],
        out_specs=out_specs,
        out_shape=out_shape,
        compiler_params=_cparams(("parallel",)),
        name="mix_in_sample" if sample else "mix_in_prompt",
    )(x, mod.arr, g.reshape(1, d), w_in_r, g_qa.reshape(1, -1), wq, g_kva.reshape(1, -1), wkv,
      wgate, b_gate.reshape(1, hk), tab)


def _attn_prompt_body(q_ref, k_ref, v_ref, o_ref, *, tq, n_pairs):
    n_heads = 2 * n_pairs
    i = pl.program_id(2)
    qs = [q_ref[hh] for hh in range(n_heads)]

    def scores(hh, j):
        off = pl.multiple_of(j * tq, tq)
        return _dot_nt(qs[hh], k_ref[hh, pl.ds(off, tq), :]), v_ref[hh // 2, pl.ds(off, tq), :]

    init = []
    for hh in range(n_heads):
        s, v = scores(hh, i)
        row = lax.broadcasted_iota(jnp.int32, s.shape, 0)
        col = lax.broadcasted_iota(jnp.int32, s.shape, 1)
        s = jnp.where(col <= row, s, jnp.finfo(F32).min)
        m = jnp.max(s, axis=-1, keepdims=True)
        p = jnp.exp2(s - m)
        init += [m, jnp.sum(p, axis=-1, keepdims=True), _dot(p.astype(BF16), v)]

    def step(j, carry):
        out = []
        for hh in range(n_heads):
            m, l, acc = carry[3 * hh:3 * hh + 3]
            s, v = scores(hh, j)
            m_new = jnp.maximum(m, jnp.max(s, axis=-1, keepdims=True))
            alpha = jnp.exp2(m - m_new)
            p = jnp.exp2(s - m_new)
            out += [m_new, alpha * l + jnp.sum(p, axis=-1, keepdims=True),
                    alpha * acc + _dot(p.astype(BF16), v)]
        return tuple(out)

    fin = lax.fori_loop(0, i, step, tuple(init))
    for pr in range(n_pairs):
        e, o = 6 * pr, 6 * pr + 3
        o_even, o_odd = fin[e + 2] / fin[e + 1], fin[o + 2] / fin[o + 1]
        lane = lax.broadcasted_iota(jnp.int32, o_even.shape, 1)
        o_ref[:, pr * LANE:(pr + 1) * LANE] = jnp.where(lane < MLA_V, o_even, o_odd).astype(o_ref.dtype)


def _attn_prompt(q, k, v, tq, n_pairs):
    b, hh, t, w = q.shape
    nh = 2 * n_pairs
    body = functools.partial(_attn_prompt_body, tq=tq, n_pairs=n_pairs)
    return pl.pallas_call(
        body,
        grid=(b, hh // nh, t // tq),
        in_specs=[pl.BlockSpec((None, nh, tq, w), lambda bi, hi, i: (bi, hi, i, 0)),
                  pl.BlockSpec((None, nh, t, w), lambda bi, hi, i: (bi, hi, 0, 0)),
                  pl.BlockSpec((None, n_pairs, t, w), lambda bi, hi, i: (bi, hi, 0, 0))],
        out_specs=pl.BlockSpec((None, tq, n_pairs * w), lambda bi, hi, i: (bi, i, hi)),
        out_shape=jax.ShapeDtypeStruct((b, t, hh // 2 * w), BF16),
        compiler_params=_cparams(("parallel", "parallel", "arbitrary")),
        name="attn_prompt",
    )(q, k, v)


def _attn_sample_body(pt_ref, ql_ref, qr_ref, cn_ref, kn_ref, ckv_hbm, kr_hbm, o_ref, ckv_buf, kr_buf,
                      ckb_buf, krb_buf, sems, *, layer, n_chunks, pg):
    b = pl.program_id(0)
    n_seq = pl.num_programs(0)

    def copies(seq, c):
        out = []
        for p in range(pg):
            page = pt_ref[seq, c * pg + p]
            rows = pl.ds(p * PAGE_SIZE, PAGE_SIZE)
            out.append(pltpu.make_async_copy(ckv_hbm.at[layer, page], ckv_buf.at[c, rows, :], sems.at[c, 0]))
            out.append(pltpu.make_async_copy(kr_hbm.at[layer, page], kr_buf.at[c, :, rows], sems.at[c, 1]))
        return out

    @pl.when(b == 0)
    def _():
        for c in range(n_chunks):
            for cp in copies(0, c):
                cp.start()

    ql = ql_ref[...]
    qr = qr_ref[...]

    cn = cn_ref[...]
    s = _dot_nt(ql.astype(F32), cn) + _dot_nt(qr.astype(F32), kn_ref[...])
    row_tok = lax.broadcasted_iota(jnp.int32, s.shape, 0) // MLA_HEADS
    col = lax.broadcasted_iota(jnp.int32, s.shape, 1)
    s = jnp.where(col <= row_tok, s, jnp.finfo(F32).min)
    m = jnp.max(s, axis=-1, keepdims=True)
    p = jnp.exp2(s - m)
    l = jnp.sum(p, axis=-1, keepdims=True)
    acc = _dot(p, cn)

    nxt = jnp.minimum(b + 1, n_seq - 1)

    def slot_filled(c):
        return (pltpu.make_async_copy(ckv_buf.at[c], ckv_buf.at[c], sems.at[c, 0]),
                pltpu.make_async_copy(kr_buf.at[c], kr_buf.at[c], sems.at[c, 1]))

    def arrive(c):
        for cp in slot_filled(c):
            cp.wait()

    def stage(c):
        ckb_buf[c] = ckv_buf[c].astype(BF16)
        krb_buf[c] = kr_buf[c].astype(BF16)
        for cp in copies(nxt, c):
            cp.start()

    arrive(0)
    stage(0)
    pending = None
    for c in range(n_chunks):
        s = _dot_nt(ql, ckb_buf[c]) + _dot(qr, krb_buf[c])
        if c + 1 < n_chunks:
            arrive(c + 1)
            stage(c + 1)
        if pending is not None:
            p_prev, alpha_prev = pending
            acc = alpha_prev * acc + _dot(p_prev, ckb_buf[c - 1])
        m_new = jnp.maximum(m, jnp.max(s, axis=-1, keepdims=True))
        alpha = jnp.exp2(m - m_new)
        p = jnp.exp2(s - m_new)
        l = alpha * l + jnp.sum(p, axis=-1, keepdims=True)
        pending = (p.astype(BF16), alpha)
        m = m_new
    p_prev, alpha_prev = pending
    acc = alpha_prev * acc + _dot(p_prev, ckb_buf[n_chunks - 1])
    o_ref[...] = acc / l

    @pl.when(b == n_seq - 1)
    def _():
        for c in range(n_chunks):
            for cp in slot_filled(c):
                cp.wait()


def _attn_sample(page_table, q_lat, q_rope, ckv_new, kr_new, cache_ckv, cache_krope_t, layer, pg):
    n_seq, rows, c = q_lat.shape
    t_new = ckv_new.shape[1]
    r = q_rope.shape[2]
    n_chunks = page_table.shape[1] // pg
    seq = lambda shape: pl.BlockSpec((None,) + shape, lambda b, pt: (b, 0, 0))
    body = functools.partial(_attn_sample_body, layer=layer, n_chunks=n_chunks, pg=pg)
    grid_spec = pltpu.PrefetchScalarGridSpec(
        num_scalar_prefetch=1,
        grid=(n_seq,),
        in_specs=[seq((rows, c)), seq((rows, r)), seq((t_new, c)), seq((t_new, r)),
                  pl.BlockSpec(memory_space=pl.ANY), pl.BlockSpec(memory_space=pl.ANY)],
        out_specs=seq((rows, c)),
        scratch_shapes=[pltpu.VMEM((n_chunks, pg * PAGE_SIZE, c), F32),
                        pltpu.VMEM((n_chunks, r, pg * PAGE_SIZE), F32),
                        pltpu.VMEM((n_chunks, pg * PAGE_SIZE, c), BF16),
                        pltpu.VMEM((n_chunks, r, pg * PAGE_SIZE), BF16),
                        pltpu.SemaphoreType.DMA((n_chunks, 2))],
    )
    return pl.pallas_call(
        body,
        grid_spec=grid_spec,
        out_shape=jax.ShapeDtypeStruct((n_seq, rows, c), F32),
        compiler_params=_cparams(("arbitrary",)),
        name="attn_sample",
    )(page_table, q_lat, q_rope, ckv_new, kr_new, cache_ckv, cache_krope_t)


def _gla_body(q_ref, k_ref, v_ref, la_ref, s0_ref, o_ref, s_ref, st_sc, *, n_seq, n_chunks, chunk):
    hk = GLA_HEADS * GLA_DK
    for sq in range(n_seq):
        st_sc[sq] = s0_ref[sq].reshape(hk, GLA_DV).T
    rows = GLA_HEADS * chunk
    r = lax.broadcasted_iota(jnp.int32, (chunk, chunk), 0)
    c = lax.broadcasted_iota(jnp.int32, (chunk, chunk), 1)
    tri = (c <= r).astype(BF16)
    rs = lax.broadcasted_iota(jnp.int32, (rows, chunk), 0)
    cs = lax.broadcasted_iota(jnp.int32, (rows, chunk), 1)
    causal_stacked = cs <= rs % chunk
    lane_head = lax.broadcasted_iota(jnp.int32, (chunk, hk), 1) // GLA_DK
    zero = jnp.zeros((chunk, hk), BF16)

    def stack_heads(x):
        return jnp.concatenate([jnp.where(lane_head == hh, x, zero) for hh in range(GLA_HEADS)], axis=0)

    def one_chunk(ci, _):
        off = pl.multiple_of(ci * chunk, chunk)
        for sq in range(n_seq):
            g = la_ref[sq, pl.ds(off, chunk), :]
            g_hi = g.astype(BF16)
            g_lo = (g - g_hi.astype(F32)).astype(BF16)
            b = _dot(tri, g_hi) + _dot(tri, g_lo)
            b_last = b[chunk - 1:chunk, :]
            q = q_ref[sq, pl.ds(off, chunk), :]
            k = k_ref[sq, pl.ds(off, chunk), :]
            v = v_ref[sq, pl.ds(off, chunk), :].astype(BF16)
            qe = stack_heads((q * jnp.exp(b)).astype(BF16))
            ke = (k * jnp.exp(-b)).astype(BF16)
            kd = stack_heads((k * jnp.exp(b_last - b)).astype(BF16))
            st = st_sc[sq]
            a = jnp.where(causal_stacked, _dot_nt(qe, ke), 0.0).astype(BF16)
            intra = _dot(a, v)
            inter = _dot_nt(qe, st.astype(BF16))
            o_ref[sq, pl.ds(off, chunk), :] = jnp.concatenate(
                [intra[hh * chunk:(hh + 1) * chunk, hh * GLA_DV:(hh + 1) * GLA_DV]
                 + inter[hh * chunk:(hh + 1) * chunk] for hh in range(GLA_HEADS)], axis=1)
            v_stacked = jnp.concatenate([v[:, hh * GLA_DV:(hh + 1) * GLA_DV] for hh in range(GLA_HEADS)], axis=0)
            st_sc[sq] = st * jnp.exp(b_last) + _dot_tn(v_stacked, kd)
        return 0

    lax.fori_loop(0, n_chunks, one_chunk, 0, unroll=min(GLA_UNROLL, n_chunks))
    for sq in range(n_seq):
        s_ref[sq] = st_sc[sq].T.reshape(GLA_HEADS, GLA_DK, GLA_DV)


def _gla(gq, gk, gv, la, s0, chunk, seqs_per_step):
    b, t, hk = gq.shape
    hv = gv.shape[2]
    g = seqs_per_step
    blk = lambda w: pl.BlockSpec((g, t, w), lambda i: (i, 0, 0))
    st = pl.BlockSpec((g, GLA_HEADS, GLA_DK, GLA_DV), lambda i: (i, 0, 0, 0))
    body = functools.partial(_gla_body, n_seq=g, n_chunks=t // chunk, chunk=chunk)
    return pl.pallas_call(
        body,
        grid=(b // g,),
        in_specs=[blk(hk), blk(hk), blk(hv), blk(hk), st],
        out_specs=[blk(hv), st],
        out_shape=[jax.ShapeDtypeStruct((b, t, hv), F32),
                   jax.ShapeDtypeStruct((b, GLA_HEADS, GLA_DK, GLA_DV), F32)],
        scratch_shapes=[pltpu.VMEM((g, GLA_DV, GLA_HEADS * GLA_DK), F32)],
        compiler_params=_cparams(("parallel",)),
        name="gla",
    )(gq, gk, gv, la, s0)


def _mix_out_ffn_body(x_ref, mod2_ref, mod3_ref, om_ref, og_ref, gg_ref, go_ref, wlat_ref, wom_ref, wog_ref,
                      g_ref, w1_ref, w3_ref, w2_ref, gf_ref, o_ref, *, d, sample, final):
    gate = mod2_ref[...]
    om = om_ref[...]
    if sample:
        om = _dot(om.astype(BF16), wlat_ref[...]).astype(BF16)
    parts = []
    for hh in range(GLA_HEADS):
        vs = slice(hh * GLA_DV, (hh + 1) * GLA_DV)
        parts.append(_rms(og_ref[:, vs], go_ref[...]) * _silu(gg_ref[:, vs]))
    og = jnp.concatenate(parts, axis=1).astype(BF16)
    mix = _dot(om, wom_ref[...]) + _dot(og, wog_ref[...])
    x = x_ref[...] + gate * mix
    out = _ffn_half_step(x, mod3_ref, g_ref, w1_ref, w3_ref, w2_ref, d)
    if final:
        out = _rms(out, gf_ref[...])
    o_ref[...] = out


def _mix_out_ffn(x, mod, o_mla, o_gla, gg, g_gla_o, w_lat, w_o_mla, w_o_gla, g, w1, w3, w2, g_final,
                 sample, final, tm):
    n, d = x.shape
    ff = w1.shape[1]
    hv = GLA_HEADS * GLA_DV
    tok = lambda w: pl.BlockSpec((tm, w), lambda i: (i, 0))
    body = functools.partial(_mix_out_ffn_body, d=d, sample=sample, final=final)
    return pl.pallas_call(
        body,
        grid=(n // tm,),
        in_specs=[tok(d), mod.spec(tm, d, 5, 1), mod.spec(tm, d, 2), tok(o_mla.shape[1]), tok(hv), tok(hv),
                  _resident((1, GLA_DV)), _resident(w_lat.shape), _resident(w_o_mla.shape),
                  _resident(w_o_gla.shape),
                  _resident((1, d)), _resident((d, ff)), _resident((d, ff)), _resident((ff, d)),
                  _resident((1, d))],
        out_specs=tok(d),
        out_shape=jax.ShapeDtypeStruct((n, d), F32),
        compiler_params=_cparams(("parallel",)),
        name="mix_out_ffn_sample" if sample else "mix_out_ffn_prompt",
    )(x, mod.arr, mod.arr, o_mla, o_gla, gg, g_gla_o.reshape(1, GLA_DV), w_lat, w_o_mla, w_o_gla,
      g.reshape(1, d), w1, w3, w2, g_final.reshape(1, d))


def _prep_w_in(w_in):
    d = w_in.shape[0]
    o = 0
    parts = {}
    for name, size in (("qa", MLA_Q_LORA), ("kva", MLA_KV_LORA), ("kr", MLA_ROPE),
                       ("gq", GLA_HEADS * GLA_DK), ("gk", GLA_HEADS * GLA_DK), ("gv", GLA_HEADS * GLA_DV),
                       ("ga", GLA_GATE_RANK), ("gg", GLA_HEADS * GLA_DV)):
        parts[name] = w_in[:, o:o + size]
        o += size
    blk_a = jnp.pad(jnp.concatenate([parts["kr"], parts["ga"]], axis=1),
                    ((0, 0), (0, LANE - MLA_ROPE - GLA_GATE_RANK)))
    w = jnp.concatenate([parts["qa"], parts["kva"], parts["gq"] * (GLA_DK ** -0.5), parts["gk"],
                         parts["gv"], parts["gg"], blk_a], axis=1)
    return w.astype(BF16)


def _prep_wq(w_qb):
    ql = w_qb.shape[0]
    w = jnp.pad(w_qb, ((0, 0), (0, 0), (0, HEAD_PAD - MLA_NOPE - MLA_ROPE)))
    return (w.reshape(ql, MLA_HEADS * HEAD_PAD) * (MLA_SCALE * LOG2_E)).astype(BF16)


def _prep_wkv_prompt(w_kvb):
    rows = w_kvb.shape[0] + LANE
    k_top = jnp.pad(w_kvb[..., :MLA_NOPE], ((0, 0), (0, 0), (0, HEAD_PAD - MLA_NOPE)))
    sel = jnp.eye(LANE, HEAD_PAD, k=MLA_NOPE, dtype=F32)
    sel = jnp.where(jnp.arange(LANE)[:, None] < MLA_ROPE, sel, 0.0)
    k = jnp.concatenate([k_top, jnp.broadcast_to(sel[:, None, :], (LANE, MLA_HEADS, HEAD_PAD))], axis=0)
    v = jnp.pad(w_kvb[..., MLA_NOPE:], ((0, LANE), (0, 0), (0, 0)))
    return jnp.concatenate([k.reshape(rows, -1), v.reshape(rows, -1)], axis=1).astype(BF16)


def _prep_wq_sample(w_kvb):
    same_head = jnp.eye(MLA_HEADS, dtype=F32)[:, None, :, None]
    pad_n = ((0, 0), (0, HEAD_PAD - MLA_NOPE), (0, 0), (0, 0))
    lat = jnp.pad(jnp.transpose(w_kvb[..., :MLA_NOPE], (1, 2, 0))[:, :, None, :] * same_head, pad_n)
    sel = jnp.eye(HEAD_PAD, MLA_ROPE, k=-MLA_NOPE, dtype=F32)
    rope = sel[None, :, None, :] * same_head
    rows = MLA_HEADS * HEAD_PAD
    return jnp.concatenate([lat.reshape(rows, -1), rope.reshape(rows, -1)], axis=1).astype(BF16)


def _prep_w_lat_out(w_kvb):
    c = w_kvb.shape[0]
    same_head = jnp.eye(MLA_HEADS, dtype=F32)[:, None, :, None]
    w = jnp.transpose(w_kvb[..., MLA_NOPE:], (1, 0, 2))[:, :, None, :] * same_head
    return w.reshape(MLA_HEADS * c, MLA_HEADS * MLA_V).astype(BF16)


def _prep_w_o(w_o):
    n_mla = MLA_HEADS * MLA_V
    return w_o[:n_mla].astype(BF16), w_o[n_mla:].astype(BF16)


def _prep_gate(w_gate_b):
    return jnp.pad(w_gate_b, ((MLA_ROPE, LANE - MLA_ROPE - GLA_GATE_RANK), (0, 0))).astype(BF16)


def _rope_table(pos):
    hr = MLA_ROPE // 2
    inv = ROPE_THETA ** (-jnp.arange(0, MLA_ROPE, 2, dtype=F32) / MLA_ROPE)
    ang = pos[:, None] * inv[None, :]
    cos, sin = jnp.cos(ang), jnp.sin(ang)
    t = pos.shape[0]
    z = lambda w: jnp.zeros((t, w), F32)

    def tabs(start):
        after = LANE - start - 2 * hr
        c = jnp.concatenate([jnp.ones((t, start), F32), cos, cos, z(after)], axis=1)
        s_from_below = jnp.concatenate([z(start + hr), sin, z(after)], axis=1)
        s_from_above = jnp.concatenate([z(start), -sin, z(hr + after)], axis=1)
        return [c, s_from_below, s_from_above]

    return jnp.concatenate(tabs(MLA_NOPE) + tabs(0), axis=1)


TM_TOKENS = 512
TM_FFN = 1024
TM_SAMPLE = 256
TQ_PROMPT = 512
HEAD_PAIRS_PER_STEP = 4
PAGES_PER_CHUNK = 64
GLA_SAMPLE_PAD = 16
GLA_SAMPLE_SEQS = 8
GLA_UNROLL = 8


def kernel(x_prompt, x_sample, cache_ckv, cache_krope, state_gla, page_table, c_prompt, c_sample,
           w_ada, b_ada, norm_ffn1, ffn1_w1, ffn1_w3, ffn1_w2, norm_mix, w_in, g_qa, w_qb, g_kva, w_kvb,
           w_gate_b, b_gate, g_gla_o, w_o, norm_ffn2, ffn2_w1, ffn2_w3, ffn2_w2, norm_final):
    bp, tp, d = x_prompt.shape
    bs, ts, _ = x_sample.shape
    depth = w_ada.shape[0]
    n_pages = page_table.shape[1]
    past_len = n_pages * PAGE_SIZE

    tab_p = _rope_table(jnp.arange(tp, dtype=F32))
    tm_s = min(TM_SAMPLE, bs * ts)
    tab_s = jnp.tile(_rope_table(past_len + jnp.arange(ts, dtype=F32)), (tm_s // ts, 1))

    hp = x_prompt.reshape(bp * tp, d)
    hs = x_sample.reshape(bs * ts, d)
    outs = [[] for _ in range(6)]
    for l in range(depth):
        m = _ada(jnp.concatenate([c_prompt, c_sample], axis=0), w_ada[l], b_ada[l])
        mod_p = _Mod(m[:bp], False, tp)
        mod_s = _Mod(jnp.repeat(m[bp:], ts, axis=0), True, ts)

        f1 = (norm_ffn1[l], ffn1_w1[l].astype(BF16), ffn1_w3[l].astype(BF16), ffn1_w2[l].astype(BF16))
        f2 = (norm_ffn2[l], ffn2_w1[l].astype(BF16), ffn2_w3[l].astype(BF16), ffn2_w2[l].astype(BF16))
        w_in_r = _prep_w_in(w_in[l])
        wq = _prep_wq(w_qb[l])
        wkv_p = _prep_wkv_prompt(w_kvb[l])
        wq_s = _prep_wq_sample(w_kvb[l])
        w_lat = _prep_w_lat_out(w_kvb[l])
        w_o_mla, w_o_gla = _prep_w_o(w_o[l])
        wgate = _prep_gate(w_gate_b[l])
        last = l == depth - 1

        hp = _ffn(hp, mod_p, 0, *f1, TM_FFN)
        q, k, v, ckv_p, kr_p, gq, gk, gv, la, gg = _mix_in(
            hp, mod_p, norm_mix[l], w_in_r, g_qa[l], wq, g_kva[l], wkv_p, wgate, b_gate[l], tab_p,
            False, TM_TOKENS, bp)
        o_mla = _attn_prompt(q, k, v, TQ_PROMPT, HEAD_PAIRS_PER_STEP).reshape(bp * tp, -1)
        seq3 = lambda a: a.reshape(bp, tp, -1)
        s0_p = jnp.zeros((bp, GLA_HEADS, GLA_DK, GLA_DV), F32)
        o_gla, s_p = _gla(seq3(gq), seq3(gk), seq3(gv), seq3(la), s0_p, GLA_CHUNK, 1)
        hp = _mix_out_ffn(hp, mod_p, o_mla, o_gla.reshape(bp * tp, -1), gg, g_gla_o[l], w_lat, w_o_mla,
                          w_o_gla, *f2, norm_final, False, last, TM_TOKENS)

        hs = _ffn(hs, mod_s, 0, *f1, tm_s)
        ql, ckv_s, kr_s, gq, gk, gv, la, gg = _mix_in(
            hs, mod_s, norm_mix[l], w_in_r, g_qa[l], wq, g_kva[l], wq_s, wgate, b_gate[l], tab_s,
            True, tm_s, bs)
        n_lat = MLA_HEADS * MLA_KV_LORA
        rows = ts * MLA_HEADS
        q_lat = ql[:, :n_lat].reshape(bs, rows, MLA_KV_LORA)
        q_rope = ql[:, n_lat:].reshape(bs, rows, MLA_ROPE)
        o_lat = _attn_sample(page_table, q_lat, q_rope, ckv_s.reshape(bs, ts, -1), kr_s.reshape(bs, ts, -1),
                             cache_ckv, jnp.swapaxes(cache_krope, 2, 3), l, PAGES_PER_CHUNK)
        pad = lambda a: jnp.pad(a.reshape(bs, ts, -1), ((0, 0), (0, GLA_SAMPLE_PAD - ts), (0, 0)))
        o_gla, s_s = _gla(pad(gq), pad(gk), pad(gv), pad(la), state_gla[l], GLA_SAMPLE_PAD, GLA_SAMPLE_SEQS)
        hs = _mix_out_ffn(hs, mod_s, o_lat.reshape(bs * ts, n_lat), o_gla[:, :ts].reshape(bs * ts, -1), gg,
                          g_gla_o[l], w_lat, w_o_mla, w_o_gla, *f2, norm_final, True, last, tm_s)

        for lst, a in zip(outs, (ckv_p.reshape(bp, tp, -1), kr_p.reshape(bp, tp, -1), s_p,
                                 ckv_s.reshape(bs, ts, -1), kr_s.reshape(bs, ts, -1), s_s)):
            lst.append(a)

    return (hp.reshape(bp, tp, d), hs.reshape(bs, ts, d)) + tuple(jnp.stack(o) for o in outs)
```

```python
import functools
import math

import jax
import jax.numpy as jnp
from jax import lax
from jax.experimental import pallas as pl
from jax.experimental.pallas import tpu as pltpu

F32 = jnp.float32
BF16 = jnp.bfloat16

PAGE_SIZE = 128
MLA_HEADS = 8
MLA_NOPE = 64
MLA_ROPE = 32
MLA_V = 64
MLA_Q_LORA = 384
MLA_KV_LORA = 256
MLA_SCALE = (MLA_NOPE + MLA_ROPE) ** -0.5
LOG2_E = math.log2(math.e)
ROPE_THETA = 10000.0
GLA_HEADS = 4
GLA_DK = 64
GLA_DV = 128
GLA_GATE_RANK = 16
GLA_GATE_TAU = 16.0
GLA_CHUNK = 64
N_ADA = 9
EPS = 1e-6

LANE = 128
HEAD_PAD = 128
VMEM_LIMIT_BYTES = 56 * 1024 * 1024

C_QA = 0
C_KVA = C_QA + MLA_Q_LORA
C_GQ = C_KVA + MLA_KV_LORA
C_GK = C_GQ + GLA_HEADS * GLA_DK
C_GV = C_GK + GLA_HEADS * GLA_DK
C_GG = C_GV + GLA_HEADS * GLA_DV
C_BLKA = C_GG + GLA_HEADS * GLA_DV
C_END = C_BLKA + LANE
N_ROPE_TABS = 6


def _cparams(sem):
    return pltpu.CompilerParams(dimension_semantics=sem, vmem_limit_bytes=VMEM_LIMIT_BYTES)


def _resident(shape):
    return pl.BlockSpec(shape, lambda *_: (0,) * len(shape), pipeline_mode=pl.Buffered(1))


def _rms(x, g):
    return x * lax.rsqrt(jnp.mean(x * x, axis=-1, keepdims=True) + EPS) * g


def _silu(x):
    return x * jax.nn.sigmoid(x)


def _dot(a, b):
    return jnp.dot(a, b, preferred_element_type=F32)


def _dot_nt(a, b):
    return lax.dot_general(a, b, (((1,), (1,)), ((), ())), preferred_element_type=F32)


def _dot_tn(a, b):
    return lax.dot_general(a, b, (((0,), (0,)), ((), ())), preferred_element_type=F32)


def _ada_body(c_ref, w_ref, b_ref, o_ref):
    c = _silu(c_ref[...]).astype(BF16)
    o_ref[...] = _dot(c, w_ref[...].astype(BF16)) + b_ref[...]


def _ada(c_all, w_ada, b_ada):
    n, d = c_all.shape
    n_out = w_ada.shape[1]
    tn = 1024
    return pl.pallas_call(
        _ada_body,
        grid=(n_out // tn,),
        in_specs=[pl.BlockSpec((n, d), lambda j: (0, 0)),
                  pl.BlockSpec((d, tn), lambda j: (0, j)),
                  pl.BlockSpec((1, tn), lambda j: (0, j))],
        out_specs=pl.BlockSpec((n, tn), lambda j: (0, j)),
        out_shape=jax.ShapeDtypeStruct((n, n_out), F32),
        compiler_params=_cparams(("arbitrary",)),
        name="ada",
    )(c_all, w_ada, b_ada.reshape(1, n_out))


class _Mod:
    def __init__(self, m, per_token, tokens_per_seq):
        self.per_token = per_token
        self.tokens_per_seq = tokens_per_seq
        self.arr = m if per_token else m.reshape(m.shape[0], 1, m.shape[1])

    def spec(self, tm, d, blk, width=3):
        if self.per_token:
            return pl.BlockSpec((tm, width * d), lambda i: (i, blk))
        per = self.tokens_per_seq // tm
        return pl.BlockSpec((None, 1, width * d), lambda i: (i // per, 0, blk))


FF_STEP = 1024


def _ff_chunks(ff):
    return tuple((c, min(c + FF_STEP, ff)) for c in range(0, ff, FF_STEP))


def _ffn_half_step(x, mod_ref, g_ref, w1_ref, w3_ref, w2_ref, d):
    shift, scale, gate = mod_ref[:, 0:d], mod_ref[:, d:2 * d], mod_ref[:, 2 * d:3 * d]
    h = (_rms(x, g_ref[...]) * (1.0 + scale) + shift).astype(BF16)
    y = None
    for c0, c1 in _ff_chunks(w1_ref.shape[1]):
        a = _dot(h, w1_ref[:, c0:c1])
        b = _dot(h, w3_ref[:, c0:c1])
        u = (_silu(a) * b).astype(BF16)
        part = _dot(u, w2_ref[c0:c1, :])
        y = part if y is None else y + part
    return x + 0.5 * gate * y


def _ffn_body(x_ref, mod_ref, g_ref, w1_ref, w3_ref, w2_ref, o_ref, *, d):
    o_ref[...] = _ffn_half_step(x_ref[...], mod_ref, g_ref, w1_ref, w3_ref, w2_ref, d)


def _ffn(x, mod, third, g, w1, w3, w2, tm):
    n, d = x.shape
    ff = w1.shape[1]
    return pl.pallas_call(
        functools.partial(_ffn_body, d=d),
        grid=(n // tm,),
        in_specs=[pl.BlockSpec((tm, d), lambda i: (i, 0)),
                  mod.spec(tm, d, third),
                  _resident((1, d)), _resident((d, ff)), _resident((d, ff)), _resident((ff, d))],
        out_specs=pl.BlockSpec((tm, d), lambda i: (i, 0)),
        out_shape=jax.ShapeDtypeStruct((n, d), F32),
        compiler_params=_cparams(("parallel",)),
        name="ffn",
    )(x, mod.arr, g.reshape(1, d), w1, w3, w2)


def _mix_in_body(x_ref, mod_ref, g_ref, win_ref, gqa_ref, wq_ref, gkva_ref, wkv_ref, wgate_ref,
                 bgate_ref, tab_ref, *out_refs, d, sample):
    if sample:
        (ql_ref, ckv_ref, kr_ref, gq_ref, gk_ref, gv_ref, la_ref, gg_ref) = out_refs
    else:
        (q_ref, k_ref, v_ref, ckv_ref, kr_ref, gq_ref, gk_ref, gv_ref, la_ref, gg_ref) = out_refs
    x = x_ref[...]
    shift, scale = mod_ref[:, 0:d], mod_ref[:, d:2 * d]
    h = (_rms(x, g_ref[...]) * (1.0 + scale) + shift).astype(BF16)
    proj = _dot(h, win_ref[...])

    q_tabs = tuple(tab_ref[:, i * LANE:(i + 1) * LANE] for i in range(3))
    k_tabs = tuple(tab_ref[:, i * LANE:(i + 1) * LANE] for i in range(3, 6))
    hr = MLA_ROPE // 2

    def rotate(x, tabs):
        c, s_from_below, s_from_above = tabs
        return x * c + pltpu.roll(x, hr, 1) * s_from_below + pltpu.roll(x, LANE - hr, 1) * s_from_above

    qn = _rms(proj[:, C_QA:C_KVA], gqa_ref[...]).astype(BF16)
    qq = _dot(qn, wq_ref[...])
    half = MLA_HEADS * HEAD_PAD
    q_heads = [rotate(qq[:, hh * HEAD_PAD:(hh + 1) * HEAD_PAD], q_tabs) for hh in range(MLA_HEADS)]

    ckv = _rms(proj[:, C_KVA:C_GQ], gkva_ref[...])
    ckv_ref[...] = ckv
    blk_a = proj[:, C_BLKA:C_END]
    kr_blk = rotate(blk_a, k_tabs)
    kr_ref[...] = kr_blk[:, 0:MLA_ROPE]

    if sample:
        q_all = jnp.concatenate(q_heads, axis=1).astype(BF16)
        ql_ref[...] = _dot(q_all, wkv_ref[...]).astype(BF16)
    else:
        for hh in range(MLA_HEADS):
            q_ref[hh] = q_heads[hh].astype(BF16)
        kv = _dot(ckv.astype(BF16), wkv_ref[...])
        kr_in_head = pltpu.roll(kr_blk, MLA_NOPE, 1)
        for hh in range(MLA_HEADS):
            k_ref[hh] = (kv[:, hh * HEAD_PAD:(hh + 1) * HEAD_PAD] + kr_in_head).astype(BF16)
        for pr in range(MLA_HEADS * MLA_V // LANE):
            v_ref[pr] = kv[:, half + pr * LANE:half + (pr + 1) * LANE].astype(BF16)

    gq_ref[...] = proj[:, C_GQ:C_GK]
    gk_ref[...] = proj[:, C_GK:C_GV]
    gv_ref[...] = proj[:, C_GV:C_GG]
    gg_ref[...] = proj[:, C_GG:C_BLKA]
    z = _dot(blk_a.astype(BF16), wgate_ref[...]) + bgate_ref[...]
    la_ref[...] = (jnp.minimum(z, 0.0) - jnp.log1p(jnp.exp(-jnp.abs(z)))) / GLA_GATE_TAU


def _mix_in(x, mod, g, w_in_r, g_qa, wq, g_kva, wkv, wgate, b_gate, tab, sample, tm, n_seq):
    n, d = x.shape
    t_seq = n // n_seq
    n_pos_tiles = tab.shape[0] // tm
    hk = GLA_HEADS * GLA_DK
    hv = GLA_HEADS * GLA_DV
    tok = lambda w: pl.BlockSpec((tm, w), lambda i: (i, 0))
    tok_shape = lambda w, dt=F32: jax.ShapeDtypeStruct((n, w), dt)
    gla_specs = [tok(hk), tok(hk), tok(hv), tok(hk), tok(hv)]
    gla_shapes = [tok_shape(hk), tok_shape(hk), tok_shape(hv), tok_shape(hk), tok_shape(hv)]
    if sample:
        wl = wkv.shape[1]
        out_specs = [tok(wl), tok(MLA_KV_LORA), tok(MLA_ROPE)] + gla_specs
        out_shape = [tok_shape(wl, BF16), tok_shape(MLA_KV_LORA), tok_shape(MLA_ROPE)] + gla_shapes
    else:
        per = t_seq // tm
        n_vp = MLA_HEADS * MLA_V // LANE
        head_spec = lambda nh: pl.BlockSpec((None, nh, tm, HEAD_PAD), lambda i: (i // per, 0, i % per, 0))
        head_shape = lambda nh: jax.ShapeDtypeStruct((n_seq, nh, t_seq, HEAD_PAD), BF16)
        out_specs = ([head_spec(MLA_HEADS)] * 2 + [head_spec(n_vp), tok(MLA_KV_LORA), tok(MLA_ROPE)]
                     + gla_specs)
        out_shape = ([head_shape(MLA_HEADS)] * 2 + [head_shape(n_vp), tok_shape(MLA_KV_LORA),
                                                    tok_shape(MLA_ROPE)] + gla_shapes)
    body = functools.partial(_mix_in_body, d=d, sample=sample)
    return pl.pallas_call(
        body,
        grid=(n // tm,),
        in_specs=[pl.BlockSpec((tm, d), lambda i: (i, 0)),
                  mod.spec(tm, d, 1),
                  _resident((1, d)), _resident(w_in_r.shape), _resident((1, MLA_Q_LORA)),
                  _resident(wq.shape), _resident((1, MLA_KV_LORA)), _resident(wkv.shape),
                  _resident(wgate.shape), _resident((1, hk)),
                  pl.BlockSpec((tm, N_ROPE_TABS * LANE), lambda i: (i % n_pos_tiles, 0))],
        out_specs=out_specs,
        out_shape=out_shape,
        compiler_params=_cparams(("parallel",)),
        name="mix_in_sample" if sample else "mix_in_prompt",
    )(x, mod.arr, g.reshape(1, d), w_in_r, g_qa.reshape(1, -1), wq, g_kva.reshape(1, -1), wkv,
      wgate, b_gate.reshape(1, hk), tab)


def _attn_prompt_body(q_ref, k_ref, v_ref, o_ref, *, tq, n_pairs):
    n_heads = 2 * n_pairs
    i = pl.program_id(2)
    qs = [q_ref[hh] for hh in range(n_heads)]

    def scores(hh, j):
        off = pl.multiple_of(j * tq, tq)
        return _dot_nt(qs[hh], k_ref[hh, pl.ds(off, tq), :]), v_ref[hh // 2, pl.ds(off, tq), :]

    init = []
    for hh in range(n_heads):
        s, v = scores(hh, i)
        row = lax.broadcasted_iota(jnp.int32, s.shape, 0)
        col = lax.broadcasted_iota(jnp.int32, s.shape, 1)
        s = jnp.where(col <= row, s, jnp.finfo(F32).min)
        m = jnp.max(s, axis=-1, keepdims=True)
        p = jnp.exp2(s - m)
        init += [m, jnp.sum(p, axis=-1, keepdims=True), _dot(p.astype(BF16), v)]

    def step(j, carry):
        out = []
        for hh in range(n_heads):
            m, l, acc = carry[3 * hh:3 * hh + 3]
            s, v = scores(hh, j)
            m_new = jnp.maximum(m, jnp.max(s, axis=-1, keepdims=True))
            alpha = jnp.exp2(m - m_new)
            p = jnp.exp2(s - m_new)
            out += [m_new, alpha * l + jnp.sum(p, axis=-1, keepdims=True),
                    alpha * acc + _dot(p.astype(BF16), v)]
        return tuple(out)

    fin = lax.fori_loop(0, i, step, tuple(init))
    for pr in range(n_pairs):
        e, o = 6 * pr, 6 * pr + 3
        o_even, o_odd = fin[e + 2] / fin[e + 1], fin[o + 2] / fin[o + 1]
        lane = lax.broadcasted_iota(jnp.int32, o_even.shape, 1)
        o_ref[:, pr * LANE:(pr + 1) * LANE] = jnp.where(lane < MLA_V, o_even, o_odd).astype(o_ref.dtype)


def _attn_prompt(q, k, v, tq, n_pairs):
    b, hh, t, w = q.shape
    nh = 2 * n_pairs
    body = functools.partial(_attn_prompt_body, tq=tq, n_pairs=n_pairs)
    return pl.pallas_call(
        body,
        grid=(b, hh // nh, t // tq),
        in_specs=[pl.BlockSpec((None, nh, tq, w), lambda bi, hi, i: (bi, hi, i, 0)),
                  pl.BlockSpec((None, nh, t, w), lambda bi, hi, i: (bi, hi, 0, 0)),
                  pl.BlockSpec((None, n_pairs, t, w), lambda bi, hi, i: (bi, hi, 0, 0))],
        out_specs=pl.BlockSpec((None, tq, n_pairs * w), lambda bi, hi, i: (bi, i, hi)),
        out_shape=jax.ShapeDtypeStruct((b, t, hh // 2 * w), BF16),
        compiler_params=_cparams(("parallel", "parallel", "arbitrary")),
        name="attn_prompt",
    )(q, k, v)


def _attn_sample_body(pt_ref, ql_ref, qr_ref, cn_ref, kn_ref, ckv_hbm, kr_hbm, o_ref, ckv_buf, kr_buf,
                      ckb_buf, krb_buf, sems, *, layer, n_chunks, pg):
    b = pl.program_id(0)
    n_seq = pl.num_programs(0)

    def copies(seq, c):
        out = []
        for p in range(pg):
            page = pt_ref[seq, c * pg + p]
            rows = pl.ds(p * PAGE_SIZE, PAGE_SIZE)
            out.append(pltpu.make_async_copy(ckv_hbm.at[layer, page], ckv_buf.at[c, rows, :], sems.at[c, 0]))
            out.append(pltpu.make_async_copy(kr_hbm.at[layer, page], kr_buf.at[c, :, rows], sems.at[c, 1]))
        return out

    @pl.when(b == 0)
    def _():
        for c in range(n_chunks):
            for cp in copies(0, c):
                cp.start()

    ql = ql_ref[...]
    qr = qr_ref[...]

    cn = cn_ref[...]
    s = _dot_nt(ql.astype(F32), cn) + _dot_nt(qr.astype(F32), kn_ref[...])
    row_tok = lax.broadcasted_iota(jnp.int32, s.shape, 0) // MLA_HEADS
    col = lax.broadcasted_iota(jnp.int32, s.shape, 1)
    s = jnp.where(col <= row_tok, s, jnp.finfo(F32).min)
    m = jnp.max(s, axis=-1, keepdims=True)
    p = jnp.exp2(s - m)
    l = jnp.sum(p, axis=-1, keepdims=True)
    acc = _dot(p, cn)

    nxt = jnp.minimum(b + 1, n_seq - 1)

    def slot_filled(c):
        return (pltpu.make_async_copy(ckv_buf.at[c], ckv_buf.at[c], sems.at[c, 0]),
                pltpu.make_async_copy(kr_buf.at[c], kr_buf.at[c], sems.at[c, 1]))

    def arrive(c):
        for cp in slot_filled(c):
            cp.wait()

    def stage(c):
        ckb_buf[c] = ckv_buf[c].astype(BF16)
        krb_buf[c] = kr_buf[c].astype(BF16)
        for cp in copies(nxt, c):
            cp.start()

    arrive(0)
    stage(0)
    pending = None
    for c in range(n_chunks):
        s = _dot_nt(ql, ckb_buf[c]) + _dot(qr, krb_buf[c])
        if c + 1 < n_chunks:
            arrive(c + 1)
            stage(c + 1)
        if pending is not None:
            p_prev, alpha_prev = pending
            acc = alpha_prev * acc + _dot(p_prev, ckb_buf[c - 1])
        m_new = jnp.maximum(m, jnp.max(s, axis=-1, keepdims=True))
        alpha = jnp.exp2(m - m_new)
        p = jnp.exp2(s - m_new)
        l = alpha * l + jnp.sum(p, axis=-1, keepdims=True)
        pending = (p.astype(BF16), alpha)
        m = m_new
    p_prev, alpha_prev = pending
    acc = alpha_prev * acc + _dot(p_prev, ckb_buf[n_chunks - 1])
    o_ref[...] = acc / l

    @pl.when(b == n_seq - 1)
    def _():
        for c in range(n_chunks):
            for cp in slot_filled(c):
                cp.wait()


def _attn_sample(page_table, q_lat, q_rope, ckv_new, kr_new, cache_ckv, cache_krope_t, layer, pg):
    n_seq, rows, c = q_lat.shape
    t_new = ckv_new.shape[1]
    r = q_rope.shape[2]
    n_chunks = page_table.shape[1] // pg
    seq = lambda shape: pl.BlockSpec((None,) + shape, lambda b, pt: (b, 0, 0))
    body = functools.partial(_attn_sample_body, layer=layer, n_chunks=n_chunks, pg=pg)
    grid_spec = pltpu.PrefetchScalarGridSpec(
        num_scalar_prefetch=1,
        grid=(n_seq,),
        in_specs=[seq((rows, c)), seq((rows, r)), seq((t_new, c)), seq((t_new, r)),
                  pl.BlockSpec(memory_space=pl.ANY), pl.BlockSpec(memory_space=pl.ANY)],
        out_specs=seq((rows, c)),
        scratch_shapes=[pltpu.VMEM((n_chunks, pg * PAGE_SIZE, c), F32),
                        pltpu.VMEM((n_chunks, r, pg * PAGE_SIZE), F32),
                        pltpu.VMEM((n_chunks, pg * PAGE_SIZE, c), BF16),
                        pltpu.VMEM((n_chunks, r, pg * PAGE_SIZE), BF16),
                        pltpu.SemaphoreType.DMA((n_chunks, 2))],
    )
    return pl.pallas_call(
        body,
        grid_spec=grid_spec,
        out_shape=jax.ShapeDtypeStruct((n_seq, rows, c), F32),
        compiler_params=_cparams(("arbitrary",)),
        name="attn_sample",
    )(page_table, q_lat, q_rope, ckv_new, kr_new, cache_ckv, cache_krope_t)


def _gla_body(q_ref, k_ref, v_ref, la_ref, s0_ref, o_ref, s_ref, st_sc, *, n_seq, n_chunks, chunk):
    hk = GLA_HEADS * GLA_DK
    for sq in range(n_seq):
        st_sc[sq] = s0_ref[sq].reshape(hk, GLA_DV).T
    rows = GLA_HEADS * chunk
    r = lax.broadcasted_iota(jnp.int32, (chunk, chunk), 0)
    c = lax.broadcasted_iota(jnp.int32, (chunk, chunk), 1)
    tri = (c <= r).astype(BF16)
    rs = lax.broadcasted_iota(jnp.int32, (rows, chunk), 0)
    cs = lax.broadcasted_iota(jnp.int32, (rows, chunk), 1)
    causal_stacked = cs <= rs % chunk
    lane_head = lax.broadcasted_iota(jnp.int32, (chunk, hk), 1) // GLA_DK
    zero = jnp.zeros((chunk, hk), BF16)

    def stack_heads(x):
        return jnp.concatenate([jnp.where(lane_head == hh, x, zero) for hh in range(GLA_HEADS)], axis=0)

    def one_chunk(ci, _):
        off = pl.multiple_of(ci * chunk, chunk)
        for sq in range(n_seq):
            g = la_ref[sq, pl.ds(off, chunk), :]
            g_hi = g.astype(BF16)
            g_lo = (g - g_hi.astype(F32)).astype(BF16)
            b = _dot(tri, g_hi) + _dot(tri, g_lo)
            b_last = b[chunk - 1:chunk, :]
            q = q_ref[sq, pl.ds(off, chunk), :]
            k = k_ref[sq, pl.ds(off, chunk), :]
            v = v_ref[sq, pl.ds(off, chunk), :].astype(BF16)
            qe = stack_heads((q * jnp.exp(b)).astype(BF16))
            ke = (k * jnp.exp(-b)).astype(BF16)
            kd = stack_heads((k * jnp.exp(b_last - b)).astype(BF16))
            st = st_sc[sq]
            a = jnp.where(causal_stacked, _dot_nt(qe, ke), 0.0).astype(BF16)
            intra = _dot(a, v)
            inter = _dot_nt(qe, st.astype(BF16))
            o_ref[sq, pl.ds(off, chunk), :] = jnp.concatenate(
                [intra[hh * chunk:(hh + 1) * chunk, hh * GLA_DV:(hh + 1) * GLA_DV]
                 + inter[hh * chunk:(hh + 1) * chunk] for hh in range(GLA_HEADS)], axis=1)
            v_stacked = jnp.concatenate([v[:, hh * GLA_DV:(hh + 1) * GLA_DV] for hh in range(GLA_HEADS)], axis=0)
            st_sc[sq] = st * jnp.exp(b_last) + _dot_tn(v_stacked, kd)
        return 0

    lax.fori_loop(0, n_chunks, one_chunk, 0, unroll=min(GLA_UNROLL, n_chunks))
    for sq in range(n_seq):
        s_ref[sq] = st_sc[sq].T.reshape(GLA_HEADS, GLA_DK, GLA_DV)


def _gla(gq, gk, gv, la, s0, chunk, seqs_per_step):
    b, t, hk = gq.shape
    hv = gv.shape[2]
    g = seqs_per_step
    blk = lambda w: pl.BlockSpec((g, t, w), lambda i: (i, 0, 0))
    st = pl.BlockSpec((g, GLA_HEADS, GLA_DK, GLA_DV), lambda i: (i, 0, 0, 0))
    body = functools.partial(_gla_body, n_seq=g, n_chunks=t // chunk, chunk=chunk)
    return pl.pallas_call(
        body,
        grid=(b // g,),
        in_specs=[blk(hk), blk(hk), blk(hv), blk(hk), st],
        out_specs=[blk(hv), st],
        out_shape=[jax.ShapeDtypeStruct((b, t, hv), F32),
                   jax.ShapeDtypeStruct((b, GLA_HEADS, GLA_DK, GLA_DV), F32)],
        scratch_shapes=[pltpu.VMEM((g, GLA_DV, GLA_HEADS * GLA_DK), F32)],
        compiler_params=_cparams(("parallel",)),
        name="gla",
    )(gq, gk, gv, la, s0)


def _mix_out_ffn_body(x_ref, mod2_ref, mod3_ref, om_ref, og_ref, gg_ref, go_ref, wlat_ref, wom_ref, wog_ref,
                      g_ref, w1_ref, w3_ref, w2_ref, gf_ref, o_ref, *, d, sample, final):
    gate = mod2_ref[...]
    om = om_ref[...]
    if sample:
        om = _dot(om.astype(BF16), wlat_ref[...]).astype(BF16)
    parts = []
    for hh in range(GLA_HEADS):
        vs = slice(hh * GLA_DV, (hh + 1) * GLA_DV)
        parts.append(_rms(og_ref[:, vs], go_ref[...]) * _silu(gg_ref[:, vs]))
    og = jnp.concatenate(parts, axis=1).astype(BF16)
    mix = _dot(om, wom_ref[...]) + _dot(og, wog_ref[...])
    x = x_ref[...] + gate * mix
    out = _ffn_half_step(x, mod3_ref, g_ref, w1_ref, w3_ref, w2_ref, d)
    if final:
        out = _rms(out, gf_ref[...])
    o_ref[...] = out


def _mix_out_ffn(x, mod, o_mla, o_gla, gg, g_gla_o, w_lat, w_o_mla, w_o_gla, g, w1, w3, w2, g_final,
                 sample, final, tm):
    n, d = x.shape
    ff = w1.shape[1]
    hv = GLA_HEADS * GLA_DV
    tok = lambda w: pl.BlockSpec((tm, w), lambda i: (i, 0))
    body = functools.partial(_mix_out_ffn_body, d=d, sample=sample, final=final)
    return pl.pallas_call(
        body,
        grid=(n // tm,),
        in_specs=[tok(d), mod.spec(tm, d, 5, 1), mod.spec(tm, d, 2), tok(o_mla.shape[1]), tok(hv), tok(hv),
                  _resident((1, GLA_DV)), _resident(w_lat.shape), _resident(w_o_mla.shape),
                  _resident(w_o_gla.shape),
                  _resident((1, d)), _resident((d, ff)), _resident((d, ff)), _resident((ff, d)),
                  _resident((1, d))],
        out_specs=tok(d),
        out_shape=jax.ShapeDtypeStruct((n, d), F32),
        compiler_params=_cparams(("parallel",)),
        name="mix_out_ffn_sample" if sample else "mix_out_ffn_prompt",
    )(x, mod.arr, mod.arr, o_mla, o_gla, gg, g_gla_o.reshape(1, GLA_DV), w_lat, w_o_mla, w_o_gla,
      g.reshape(1, d), w1, w3, w2, g_final.reshape(1, d))


def _prep_w_in(w_in):
    d = w_in.shape[0]
    o = 0
    parts = {}
    for name, size in (("qa", MLA_Q_LORA), ("kva", MLA_KV_LORA), ("kr", MLA_ROPE),
                       ("gq", GLA_HEADS * GLA_DK), ("gk", GLA_HEADS * GLA_DK), ("gv", GLA_HEADS * GLA_DV),
                       ("ga", GLA_GATE_RANK), ("gg", GLA_HEADS * GLA_DV)):
        parts[name] = w_in[:, o:o + size]
        o += size
    blk_a = jnp.pad(jnp.concatenate([parts["kr"], parts["ga"]], axis=1),
                    ((0, 0), (0, LANE - MLA_ROPE - GLA_GATE_RANK)))
    w = jnp.concatenate([parts["qa"], parts["kva"], parts["gq"] * (GLA_DK ** -0.5), parts["gk"],
                         parts["gv"], parts["gg"], blk_a], axis=1)
    return w.astype(BF16)


def _prep_wq(w_qb):
    ql = w_qb.shape[0]
    w = jnp.pad(w_qb, ((0, 0), (0, 0), (0, HEAD_PAD - MLA_NOPE - MLA_ROPE)))
    return (w.reshape(ql, MLA_HEADS * HEAD_PAD) * (MLA_SCALE * LOG2_E)).astype(BF16)


def _prep_wkv_prompt(w_kvb):
    rows = w_kvb.shape[0]
    k = jnp.pad(w_kvb[..., :MLA_NOPE], ((0, 0), (0, 0), (0, HEAD_PAD - MLA_NOPE)))
    v = w_kvb[..., MLA_NOPE:]
    return jnp.concatenate([k.reshape(rows, -1), v.reshape(rows, -1)], axis=1).astype(BF16)


def _prep_wq_sample(w_kvb):
    same_head = jnp.eye(MLA_HEADS, dtype=F32)[:, None, :, None]
    pad_n = ((0, 0), (0, HEAD_PAD - MLA_NOPE), (0, 0), (0, 0))
    lat = jnp.pad(jnp.transpose(w_kvb[..., :MLA_NOPE], (1, 2, 0))[:, :, None, :] * same_head, pad_n)
    sel = jnp.eye(HEAD_PAD, MLA_ROPE, k=-MLA_NOPE, dtype=F32)
    rope = sel[None, :, None, :] * same_head
    rows = MLA_HEADS * HEAD_PAD
    return jnp.concatenate([lat.reshape(rows, -1), rope.reshape(rows, -1)], axis=1).astype(BF16)


def _prep_w_lat_out(w_kvb):
    c = w_kvb.shape[0]
    same_head = jnp.eye(MLA_HEADS, dtype=F32)[:, None, :, None]
    w = jnp.transpose(w_kvb[..., MLA_NOPE:], (1, 0, 2))[:, :, None, :] * same_head
    return w.reshape(MLA_HEADS * c, MLA_HEADS * MLA_V).astype(BF16)


def _prep_w_o(w_o):
    n_mla = MLA_HEADS * MLA_V
    return w_o[:n_mla].astype(BF16), w_o[n_mla:].astype(BF16)


def _prep_gate(w_gate_b):
    return jnp.pad(w_gate_b, ((MLA_ROPE, LANE - MLA_ROPE - GLA_GATE_RANK), (0, 0))).astype(BF16)


def _rope_table(pos):
    hr = MLA_ROPE // 2
    inv = ROPE_THETA ** (-jnp.arange(0, MLA_ROPE, 2, dtype=F32) / MLA_ROPE)
    ang = pos[:, None] * inv[None, :]
    cos, sin = jnp.cos(ang), jnp.sin(ang)
    t = pos.shape[0]
    z = lambda w: jnp.zeros((t, w), F32)

    def tabs(start):
        after = LANE - start - 2 * hr
        c = jnp.concatenate([jnp.ones((t, start), F32), cos, cos, z(after)], axis=1)
        s_from_below = jnp.concatenate([z(start + hr), sin, z(after)], axis=1)
        s_from_above = jnp.concatenate([z(start), -sin, z(hr + after)], axis=1)
        return [c, s_from_below, s_from_above]

    return jnp.concatenate(tabs(MLA_NOPE) + tabs(0), axis=1)


TM_TOKENS = 512
TM_FFN = 1024
TM_SAMPLE = 256
TQ_PROMPT = 512
HEAD_PAIRS_PER_STEP = 4
PAGES_PER_CHUNK = 64
GLA_SAMPLE_PAD = 16
GLA_SAMPLE_SEQS = 8
GLA_UNROLL = 8


def kernel(x_prompt, x_sample, cache_ckv, cache_krope, state_gla, page_table, c_prompt, c_sample,
           w_ada, b_ada, norm_ffn1, ffn1_w1, ffn1_w3, ffn1_w2, norm_mix, w_in, g_qa, w_qb, g_kva, w_kvb,
           w_gate_b, b_gate, g_gla_o, w_o, norm_ffn2, ffn2_w1, ffn2_w3, ffn2_w2, norm_final):
    bp, tp, d = x_prompt.shape
    bs, ts, _ = x_sample.shape
    depth = w_ada.shape[0]
    n_pages = page_table.shape[1]
    past_len = n_pages * PAGE_SIZE

    tab_p = _rope_table(jnp.arange(tp, dtype=F32))
    tm_s = min(TM_SAMPLE, bs * ts)
    tab_s = jnp.tile(_rope_table(past_len + jnp.arange(ts, dtype=F32)), (tm_s // ts, 1))

    hp = x_prompt.reshape(bp * tp, d)
    hs = x_sample.reshape(bs * ts, d)
    outs = [[] for _ in range(6)]
    for l in range(depth):
        m = _ada(jnp.concatenate([c_prompt, c_sample], axis=0), w_ada[l], b_ada[l])
        mod_p = _Mod(m[:bp], False, tp)
        mod_s = _Mod(jnp.repeat(m[bp:], ts, axis=0), True, ts)

        f1 = (norm_ffn1[l], ffn1_w1[l].astype(BF16), ffn1_w3[l].astype(BF16), ffn1_w2[l].astype(BF16))
        f2 = (norm_ffn2[l], ffn2_w1[l].astype(BF16), ffn2_w3[l].astype(BF16), ffn2_w2[l].astype(BF16))
        w_in_r = _prep_w_in(w_in[l])
        wq = _prep_wq(w_qb[l])
        wkv_p = _prep_wkv_prompt(w_kvb[l])
        wq_s = _prep_wq_sample(w_kvb[l])
        w_lat = _prep_w_lat_out(w_kvb[l])
        w_o_mla, w_o_gla = _prep_w_o(w_o[l])
        wgate = _prep_gate(w_gate_b[l])
        last = l == depth - 1

        hp = _ffn(hp, mod_p, 0, *f1, TM_FFN)
        q, k, v, ckv_p, kr_p, gq, gk, gv, la, gg = _mix_in(
            hp, mod_p, norm_mix[l], w_in_r, g_qa[l], wq, g_kva[l], wkv_p, wgate, b_gate[l], tab_p,
            False, TM_TOKENS, bp)
        o_mla = _attn_prompt(q, k, v, TQ_PROMPT, HEAD_PAIRS_PER_STEP).reshape(bp * tp, -1)
        seq3 = lambda a: a.reshape(bp, tp, -1)
        s0_p = jnp.zeros((bp, GLA_HEADS, GLA_DK, GLA_DV), F32)
        o_gla, s_p = _gla(seq3(gq), seq3(gk), seq3(gv), seq3(la), s0_p, GLA_CHUNK, 1)
        hp = _mix_out_ffn(hp, mod_p, o_mla, o_gla.reshape(bp * tp, -1), gg, g_gla_o[l], w_lat, w_o_mla,
                          w_o_gla, *f2, norm_final, False, last, TM_TOKENS)

        hs = _ffn(hs, mod_s, 0, *f1, tm_s)
        ql, ckv_s, kr_s, gq, gk, gv, la, gg = _mix_in(
            hs, mod_s, norm_mix[l], w_in_r, g_qa[l], wq, g_kva[l], wq_s, wgate, b_gate[l], tab_s,
            True, tm_s, bs)
        n_lat = MLA_HEADS * MLA_KV_LORA
        rows = ts * MLA_HEADS
        q_lat = ql[:, :n_lat].reshape(bs, rows, MLA_KV_LORA)
        q_rope = ql[:, n_lat:].reshape(bs, rows, MLA_ROPE)
        o_lat = _attn_sample(page_table, q_lat, q_rope, ckv_s.reshape(bs, ts, -1), kr_s.reshape(bs, ts, -1),
                             cache_ckv, jnp.swapaxes(cache_krope, 2, 3), l, PAGES_PER_CHUNK)
        pad = lambda a: jnp.pad(a.reshape(bs, ts, -1), ((0, 0), (0, GLA_SAMPLE_PAD - ts), (0, 0)))
        o_gla, s_s = _gla(pad(gq), pad(gk), pad(gv), pad(la), state_gla[l], GLA_SAMPLE_PAD, GLA_SAMPLE_SEQS)
        hs = _mix_out_ffn(hs, mod_s, o_lat.reshape(bs * ts, n_lat), o_gla[:, :ts].reshape(bs * ts, -1), gg,
                          g_gla_o[l], w_lat, w_o_mla, w_o_gla, *f2, norm_final, True, last, tm_s)

        for lst, a in zip(outs, (ckv_p.reshape(bp, tp, -1), kr_p.reshape(bp, tp, -1), s_p,
                                 ckv_s.reshape(bs, ts, -1), kr_s.reshape(bs, ts, -1), s_s)):
            lst.append(a)

    return (hp.reshape(bp, tp, d), hs.reshape(bs, ts, d)) + tuple(jnp.stack(o) for o in outs)
```

```python
import functools
import math

import jax
import jax.numpy as jnp
from jax import lax
from jax.experimental import pallas as pl
from jax.experimental.pallas import tpu as pltpu

F32 = jnp.float32
BF16 = jnp.bfloat16

PAGE_SIZE = 128
MLA_HEADS = 8
MLA_NOPE = 64
MLA_ROPE = 32
MLA_V = 64
MLA_Q_LORA = 384
MLA_KV_LORA = 256
MLA_SCALE = (MLA_NOPE + MLA_ROPE) ** -0.5
LOG2_E = math.log2(math.e)
ROPE_THETA = 10000.0
GLA_HEADS = 4
GLA_DK = 64
GLA_DV = 128
GLA_GATE_RANK = 16
GLA_GATE_TAU = 16.0
GLA_CHUNK = 64
N_ADA = 9
EPS = 1e-6

LANE = 128
HEAD_PAD = 128
VMEM_LIMIT_BYTES = 56 * 1024 * 1024

C_QA = 0
C_KVA = C_QA + MLA_Q_LORA
C_GQ = C_KVA + MLA_KV_LORA
C_GK = C_GQ + GLA_HEADS * GLA_DK
C_GV = C_GK + GLA_HEADS * GLA_DK
C_GG = C_GV + GLA_HEADS * GLA_DV
C_BLKA = C_GG + GLA_HEADS * GLA_DV
C_END = C_BLKA + LANE
N_ROPE_TABS = 6


def _cparams(sem):
    return pltpu.CompilerParams(dimension_semantics=sem, vmem_limit_bytes=VMEM_LIMIT_BYTES)


def _resident(shape):
    return pl.BlockSpec(shape, lambda *_: (0,) * len(shape), pipeline_mode=pl.Buffered(1))


def _rms(x, g):
    return x * lax.rsqrt(jnp.mean(x * x, axis=-1, keepdims=True) + EPS) * g


def _silu(x):
    return x * jax.nn.sigmoid(x)


def _dot(a, b):
    return jnp.dot(a, b, preferred_element_type=F32)


def _dot_nt(a, b):
    return lax.dot_general(a, b, (((1,), (1,)), ((), ())), preferred_element_type=F32)


def _dot_tn(a, b):
    return lax.dot_general(a, b, (((0,), (0,)), ((), ())), preferred_element_type=F32)


def _ada_body(c_ref, w_ref, b_ref, o_ref):
    c = _silu(c_ref[...]).astype(BF16)
    o_ref[...] = _dot(c, w_ref[...].astype(BF16)) + b_ref[...]


def _ada(c_all, w_ada, b_ada):
    n, d = c_all.shape
    n_out = w_ada.shape[1]
    tn = 1024
    return pl.pallas_call(
        _ada_body,
        grid=(n_out // tn,),
        in_specs=[pl.BlockSpec((n, d), lambda j: (0, 0)),
                  pl.BlockSpec((d, tn), lambda j: (0, j)),
                  pl.BlockSpec((1, tn), lambda j: (0, j))],
        out_specs=pl.BlockSpec((n, tn), lambda j: (0, j)),
        out_shape=jax.ShapeDtypeStruct((n, n_out), F32),
        compiler_params=_cparams(("arbitrary",)),
        name="ada",
    )(c_all, w_ada, b_ada.reshape(1, n_out))


class _Mod:
    def __init__(self, m, tokens_per_seq, tm):
        self.tokens_per_seq = tokens_per_seq
        self.rep = tokens_per_seq if tm > tokens_per_seq else 1
        self.arr = m if self.rep > 1 else m.reshape(m.shape[0], 1, m.shape[1])

    def spec(self, tm, d, blk, width=3):
        if self.rep > 1:
            return pl.BlockSpec((tm // self.rep, width * d), lambda i: (i, blk))
        per = self.tokens_per_seq // tm
        return pl.BlockSpec((None, 1, width * d), lambda i: (i // per, 0, blk))


def _mod_cols(mod_ref, lo, hi, rep):
    v = mod_ref[:, lo:hi]
    if rep > 1:
        g, w = v.shape
        v = jnp.broadcast_to(v[:, None, :], (g, rep, w)).reshape(g * rep, w)
    return v


FF_STEP = 1024


def _ff_chunks(ff):
    return tuple((c, min(c + FF_STEP, ff)) for c in range(0, ff, FF_STEP))


def _ffn_half_step(x, mod_ref, g_ref, w1_ref, w3_ref, w2_ref, d, rep):
    shift, scale, gate = (_mod_cols(mod_ref, k * d, (k + 1) * d, rep) for k in range(3))
    h = (_rms(x, g_ref[...]) * (1.0 + scale) + shift).astype(BF16)
    y = None
    for c0, c1 in _ff_chunks(w1_ref.shape[1]):
        a = _dot(h, w1_ref[:, c0:c1])
        b = _dot(h, w3_ref[:, c0:c1])
        u = (_silu(a) * b).astype(BF16)
        part = _dot(u, w2_ref[c0:c1, :])
        y = part if y is None else y + part
    return x + 0.5 * gate * y


def _ffn_body(x_ref, mod_ref, g_ref, w1_ref, w3_ref, w2_ref, o_ref, *, d, rep):
    o_ref[...] = _ffn_half_step(x_ref[...], mod_ref, g_ref, w1_ref, w3_ref, w2_ref, d, rep)


def _ffn(x, mod, third, g, w1, w3, w2, tm):
    n, d = x.shape
    ff = w1.shape[1]
    return pl.pallas_call(
        functools.partial(_ffn_body, d=d, rep=mod.rep),
        grid=(n // tm,),
        in_specs=[pl.BlockSpec((tm, d), lambda i: (i, 0)),
                  mod.spec(tm, d, third),
                  _resident((1, d)), _resident((d, ff)), _resident((d, ff)), _resident((ff, d))],
        out_specs=pl.BlockSpec((tm, d), lambda i: (i, 0)),
        out_shape=jax.ShapeDtypeStruct((n, d), F32),
        compiler_params=_cparams(("parallel",)),
        name="ffn",
    )(x, mod.arr, g.reshape(1, d), w1, w3, w2)


def _mix_in_body(x_ref, mod_ref, g_ref, win_ref, gqa_ref, wq_ref, gkva_ref, wkv_ref, wgate_ref,
                 bgate_ref, tab_ref, *out_refs, d, sample, rep):
    if sample:
        (ql_ref, ckv_ref, kr_ref, gq_ref, gk_ref, gv_ref, la_ref, gg_ref) = out_refs
    else:
        (q_ref, k_ref, v_ref, ckv_ref, kr_ref, gq_ref, gk_ref, gv_ref, la_ref, gg_ref) = out_refs
    x = x_ref[...]
    shift, scale = _mod_cols(mod_ref, 0, d, rep), _mod_cols(mod_ref, d, 2 * d, rep)
    h = (_rms(x, g_ref[...]) * (1.0 + scale) + shift).astype(BF16)
    proj = _dot(h, win_ref[...])

    q_tabs = tuple(tab_ref[:, i * LANE:(i + 1) * LANE] for i in range(3))
    k_tabs = tuple(tab_ref[:, i * LANE:(i + 1) * LANE] for i in range(3, 6))
    hr = MLA_ROPE // 2

    def rotate(x, tabs):
        c, s_from_below, s_from_above = tabs
        return x * c + pltpu.roll(x, hr, 1) * s_from_below + pltpu.roll(x, LANE - hr, 1) * s_from_above

    qn = _rms(proj[:, C_QA:C_KVA], gqa_ref[...]).astype(BF16)
    qq = _dot(qn, wq_ref[...])
    half = MLA_HEADS * HEAD_PAD
    q_heads = [rotate(qq[:, hh * HEAD_PAD:(hh + 1) * HEAD_PAD], q_tabs) for hh in range(MLA_HEADS)]

    ckv = _rms(proj[:, C_KVA:C_GQ], gkva_ref[...])
    ckv_ref[...] = ckv
    blk_a = proj[:, C_BLKA:C_END]
    kr_blk = rotate(blk_a, k_tabs)
    kr_ref[...] = kr_blk[:, 0:MLA_ROPE]

    if sample:
        q_all = jnp.concatenate(q_heads, axis=1).astype(BF16)
        ql_ref[...] = _dot(q_all, wkv_ref[...]).astype(BF16)
    else:
        for hh in range(MLA_HEADS):
            q_ref[hh] = q_heads[hh].astype(BF16)
        kv = _dot(ckv.astype(BF16), wkv_ref[...])
        kr_in_head = pltpu.roll(kr_blk, MLA_NOPE, 1)
        for hh in range(MLA_HEADS):
            k_ref[hh] = (kv[:, hh * HEAD_PAD:(hh + 1) * HEAD_PAD] + kr_in_head).astype(BF16)
        for pr in range(MLA_HEADS * MLA_V // LANE):
            v_ref[pr] = kv[:, half + pr * LANE:half + (pr + 1) * LANE].astype(BF16)

    gq_ref[...] = proj[:, C_GQ:C_GK]
    gk_ref[...] = proj[:, C_GK:C_GV]
    gv_ref[...] = proj[:, C_GV:C_GG]
    gg_ref[...] = proj[:, C_GG:C_BLKA]
    z = _dot(blk_a.astype(BF16), wgate_ref[...]) + bgate_ref[...]
    la_ref[...] = (jnp.minimum(z, 0.0) - jnp.log1p(jnp.exp(-jnp.abs(z)))) / GLA_GATE_TAU


def _mix_in(x, mod, g, w_in_r, g_qa, wq, g_kva, wkv, wgate, b_gate, tab, sample, tm, n_seq):
    n, d = x.shape
    t_seq = n // n_seq
    n_pos_tiles = tab.shape[0] // tm
    hk = GLA_HEADS * GLA_DK
    hv = GLA_HEADS * GLA_DV
    tok = lambda w: pl.BlockSpec((tm, w), lambda i: (i, 0))
    tok_shape = lambda w, dt=F32: jax.ShapeDtypeStruct((n, w), dt)
    gla_specs = [tok(hk), tok(hk), tok(hv), tok(hk), tok(hv)]
    gla_shapes = [tok_shape(hk), tok_shape(hk), tok_shape(hv), tok_shape(hk), tok_shape(hv)]
    if sample:
        wl = wkv.shape[1]
        out_specs = [tok(wl), tok(MLA_KV_LORA), tok(MLA_ROPE)] + gla_specs
        out_shape = [tok_shape(wl, BF16), tok_shape(MLA_KV_LORA), tok_shape(MLA_ROPE)] + gla_shapes
    else:
        per = t_seq // tm
        n_vp = MLA_HEADS * MLA_V // LANE
        head_spec = lambda nh: pl.BlockSpec((None, nh, tm, HEAD_PAD), lambda i: (i // per, 0, i % per, 0))
        head_shape = lambda nh: jax.ShapeDtypeStruct((n_seq, nh, t_seq, HEAD_PAD), BF16)
        out_specs = ([head_spec(MLA_HEADS)] * 2 + [head_spec(n_vp), tok(MLA_KV_LORA), tok(MLA_ROPE)]
                     + gla_specs)
        out_shape = ([head_shape(MLA_HEADS)] * 2 + [head_shape(n_vp), tok_shape(MLA_KV_LORA),
                                                    tok_shape(MLA_ROPE)] + gla_shapes)
    body = functools.partial(_mix_in_body, d=d, sample=sample, rep=mod.rep)
    return pl.pallas_call(
        body,
        grid=(n // tm,),
        in_specs=[pl.BlockSpec((tm, d), lambda i: (i, 0)),
                  mod.spec(tm, d, 1),
                  _resident((1, d)), _resident(w_in_r.shape), _resident((1, MLA_Q_LORA)),
                  _resident(wq.shape), _resident((1, MLA_KV_LORA)), _resident(wkv.shape),
                  _resident(wgate.shape), _resident((1, hk)),
                  pl.BlockSpec((tm, N_ROPE_TABS * LANE), lambda i: (i % n_pos_tiles, 0))],
        out_specs=out_specs,
        out_shape=out_shape,
        compiler_params=_cparams(("parallel",)),
        name="mix_in_sample" if sample else "mix_in_prompt",
    )(x, mod.arr, g.reshape(1, d), w_in_r, g_qa.reshape(1, -1), wq, g_kva.reshape(1, -1), wkv,
      wgate, b_gate.reshape(1, hk), tab)


def _attn_prompt_body(q_ref, k_ref, v_ref, o_ref, *, tq, n_pairs):
    n_heads = 2 * n_pairs
    i = pl.program_id(2)
    qs = [q_ref[hh] for hh in range(n_heads)]

    def scores(hh, j):
        off = pl.multiple_of(j * tq, tq)
        return _dot_nt(qs[hh], k_ref[hh, pl.ds(off, tq), :]), v_ref[hh // 2, pl.ds(off, tq), :]

    init = []
    for hh in range(n_heads):
        s, v = scores(hh, i)
        row = lax.broadcasted_iota(jnp.int32, s.shape, 0)
        col = lax.broadcasted_iota(jnp.int32, s.shape, 1)
        s = jnp.where(col <= row, s, jnp.finfo(F32).min)
        m = jnp.max(s, axis=-1, keepdims=True)
        p = jnp.exp2(s - m)
        init += [m, jnp.sum(p, axis=-1, keepdims=True), _dot(p.astype(BF16), v)]

    def step(j, carry):
        out = []
        for hh in range(n_heads):
            m, l, acc = carry[3 * hh:3 * hh + 3]
            s, v = scores(hh, j)
            m_new = jnp.maximum(m, jnp.max(s, axis=-1, keepdims=True))
            alpha = jnp.exp2(m - m_new)
            p = jnp.exp2(s - m_new)
            out += [m_new, alpha * l + jnp.sum(p, axis=-1, keepdims=True),
                    alpha * acc + _dot(p.astype(BF16), v)]
        return tuple(out)

    fin = lax.fori_loop(0, i, step, tuple(init))
    for pr in range(n_pairs):
        e, o = 6 * pr, 6 * pr + 3
        o_even, o_odd = fin[e + 2] / fin[e + 1], fin[o + 2] / fin[o + 1]
        lane = lax.broadcasted_iota(jnp.int32, o_even.shape, 1)
        o_ref[:, pr * LANE:(pr + 1) * LANE] = jnp.where(lane < MLA_V, o_even, o_odd).astype(o_ref.dtype)


def _attn_prompt(q, k, v, tq, n_pairs):
    b, hh, t, w = q.shape
    nh = 2 * n_pairs
    body = functools.partial(_attn_prompt_body, tq=tq, n_pairs=n_pairs)
    return pl.pallas_call(
        body,
        grid=(b, hh // nh, t // tq),
        in_specs=[pl.BlockSpec((None, nh, tq, w), lambda bi, hi, i: (bi, hi, i, 0)),
                  pl.BlockSpec((None, nh, t, w), lambda bi, hi, i: (bi, hi, 0, 0)),
                  pl.BlockSpec((None, n_pairs, t, w), lambda bi, hi, i: (bi, hi, 0, 0))],
        out_specs=pl.BlockSpec((None, tq, n_pairs * w), lambda bi, hi, i: (bi, i, hi)),
        out_shape=jax.ShapeDtypeStruct((b, t, hh // 2 * w), BF16),
        compiler_params=_cparams(("parallel", "parallel", "arbitrary")),
        name="attn_prompt",
    )(q, k, v)


def _attn_sample_body(pt_ref, ql_ref, qr_ref, cn_ref, kn_ref, ckv_hbm, kr_hbm, o_ref, ckv_buf, kr_buf,
                      ckb_buf, krb_buf, sems, *, layer, n_chunks, pg):
    b = pl.program_id(0)
    n_seq = pl.num_programs(0)

    def copies(seq, c):
        out = []
        for p in range(pg):
            page = pt_ref[seq, c * pg + p]
            rows = pl.ds(p * PAGE_SIZE, PAGE_SIZE)
            out.append(pltpu.make_async_copy(ckv_hbm.at[layer, page], ckv_buf.at[c, rows, :], sems.at[c, 0]))
            out.append(pltpu.make_async_copy(kr_hbm.at[layer, page], kr_buf.at[c, :, rows], sems.at[c, 1]))
        return out

    @pl.when(b == 0)
    def _():
        for c in range(n_chunks):
            for cp in copies(0, c):
                cp.start()

    ql = ql_ref[...]
    qr = qr_ref[...]

    cn = cn_ref[...]
    s = _dot_nt(ql.astype(F32), cn) + _dot_nt(qr.astype(F32), kn_ref[...])
    row_tok = lax.broadcasted_iota(jnp.int32, s.shape, 0) // MLA_HEADS
    col = lax.broadcasted_iota(jnp.int32, s.shape, 1)
    s = jnp.where(col <= row_tok, s, jnp.finfo(F32).min)
    m = jnp.max(s, axis=-1, keepdims=True)
    p = jnp.exp2(s - m)
    l = jnp.sum(p, axis=-1, keepdims=True)
    acc = _dot(p, cn)

    nxt = jnp.minimum(b + 1, n_seq - 1)

    def slot_filled(c):
        return (pltpu.make_async_copy(ckv_buf.at[c], ckv_buf.at[c], sems.at[c, 0]),
                pltpu.make_async_copy(kr_buf.at[c], kr_buf.at[c], sems.at[c, 1]))

    def arrive(c):
        for cp in slot_filled(c):
            cp.wait()

    def stage(c):
        ckb_buf[c] = ckv_buf[c].astype(BF16)
        krb_buf[c] = kr_buf[c].astype(BF16)
        for cp in copies(nxt, c):
            cp.start()

    arrive(0)
    stage(0)
    pending = None
    for c in range(n_chunks):
        s = _dot_nt(ql, ckb_buf[c]) + _dot(qr, krb_buf[c])
        if c + 1 < n_chunks:
            arrive(c + 1)
            stage(c + 1)
        if pending is not None:
            p_prev, alpha_prev = pending
            acc = alpha_prev * acc + _dot(p_prev, ckb_buf[c - 1])
        m_new = jnp.maximum(m, jnp.max(s, axis=-1, keepdims=True))
        alpha = jnp.exp2(m - m_new)
        p = jnp.exp2(s - m_new)
        l = alpha * l + jnp.sum(p, axis=-1, keepdims=True)
        pending = (p.astype(BF16), alpha)
        m = m_new
    p_prev, alpha_prev = pending
    acc = alpha_prev * acc + _dot(p_prev, ckb_buf[n_chunks - 1])
    o_ref[...] = acc / l

    @pl.when(b == n_seq - 1)
    def _():
        for c in range(n_chunks):
            for cp in slot_filled(c):
                cp.wait()


def _attn_sample(page_table, q_lat, q_rope, ckv_new, kr_new, cache_ckv, cache_krope_t, layer, pg):
    n_seq, rows, c = q_lat.shape
    t_new = ckv_new.shape[1]
    r = q_rope.shape[2]
    n_chunks = page_table.shape[1] // pg
    seq = lambda shape: pl.BlockSpec((None,) + shape, lambda b, pt: (b, 0, 0))
    body = functools.partial(_attn_sample_body, layer=layer, n_chunks=n_chunks, pg=pg)
    grid_spec = pltpu.PrefetchScalarGridSpec(
        num_scalar_prefetch=1,
        grid=(n_seq,),
        in_specs=[seq((rows, c)), seq((rows, r)), seq((t_new, c)), seq((t_new, r)),
                  pl.BlockSpec(memory_space=pl.ANY), pl.BlockSpec(memory_space=pl.ANY)],
        out_specs=seq((rows, c)),
        scratch_shapes=[pltpu.VMEM((n_chunks, pg * PAGE_SIZE, c), F32),
                        pltpu.VMEM((n_chunks, r, pg * PAGE_SIZE), F32),
                        pltpu.VMEM((n_chunks, pg * PAGE_SIZE, c), BF16),
                        pltpu.VMEM((n_chunks, r, pg * PAGE_SIZE), BF16),
                        pltpu.SemaphoreType.DMA((n_chunks, 2))],
    )
    return pl.pallas_call(
        body,
        grid_spec=grid_spec,
        out_shape=jax.ShapeDtypeStruct((n_seq, rows, c), F32),
        compiler_params=_cparams(("arbitrary",)),
        name="attn_sample",
    )(page_table, q_lat, q_rope, ckv_new, kr_new, cache_ckv, cache_krope_t)


def _gla_body(q_ref, k_ref, v_ref, la_ref, s0_ref, o_ref, s_ref, st_sc, *, n_seq, n_chunks, chunk):
    hk = GLA_HEADS * GLA_DK
    for sq in range(n_seq):
        st_sc[sq] = s0_ref[sq].reshape(hk, GLA_DV).T
    rows = GLA_HEADS * chunk
    r = lax.broadcasted_iota(jnp.int32, (chunk, chunk), 0)
    c = lax.broadcasted_iota(jnp.int32, (chunk, chunk), 1)
    tri = (c <= r).astype(BF16)
    rs = lax.broadcasted_iota(jnp.int32, (rows, chunk), 0)
    cs = lax.broadcasted_iota(jnp.int32, (rows, chunk), 1)
    causal_stacked = cs <= rs % chunk
    lane_head = lax.broadcasted_iota(jnp.int32, (chunk, hk), 1) // GLA_DK
    zero = jnp.zeros((chunk, hk), BF16)

    def stack_heads(x):
        return jnp.concatenate([jnp.where(lane_head == hh, x, zero) for hh in range(GLA_HEADS)], axis=0)

    def one_chunk(ci, _):
        off = pl.multiple_of(ci * chunk, chunk)
        for sq in range(n_seq):
            g = la_ref[sq, pl.ds(off, chunk), :]
            g_hi = g.astype(BF16)
            g_lo = (g - g_hi.astype(F32)).astype(BF16)
            b = _dot(tri, g_hi) + _dot(tri, g_lo)
            b_last = b[chunk - 1:chunk, :]
            q = q_ref[sq, pl.ds(off, chunk), :]
            k = k_ref[sq, pl.ds(off, chunk), :]
            v = v_ref[sq, pl.ds(off, chunk), :].astype(BF16)
            qe = stack_heads((q * jnp.exp(b)).astype(BF16))
            ke = (k * jnp.exp(-b)).astype(BF16)
            kd = stack_heads((k * jnp.exp(b_last - b)).astype(BF16))
            st = st_sc[sq]
            a = jnp.where(causal_stacked, _dot_nt(qe, ke), 0.0).astype(BF16)
            intra = _dot(a, v)
            inter = _dot_nt(qe, st.astype(BF16))
            o_ref[sq, pl.ds(off, chunk), :] = jnp.concatenate(
                [intra[hh * chunk:(hh + 1) * chunk, hh * GLA_DV:(hh + 1) * GLA_DV]
                 + inter[hh * chunk:(hh + 1) * chunk] for hh in range(GLA_HEADS)], axis=1)
            v_stacked = jnp.concatenate([v[:, hh * GLA_DV:(hh + 1) * GLA_DV] for hh in range(GLA_HEADS)], axis=0)
            st_sc[sq] = st * jnp.exp(b_last) + _dot_tn(v_stacked, kd)
        return 0

    lax.fori_loop(0, n_chunks, one_chunk, 0, unroll=min(GLA_UNROLL, n_chunks))
    for sq in range(n_seq):
        s_ref[sq] = st_sc[sq].T.reshape(GLA_HEADS, GLA_DK, GLA_DV)


def _gla(gq, gk, gv, la, s0, chunk, seqs_per_step):
    b, t, hk = gq.shape
    hv = gv.shape[2]
    g = seqs_per_step
    blk = lambda w: pl.BlockSpec((g, t, w), lambda i: (i, 0, 0))
    st = pl.BlockSpec((g, GLA_HEADS, GLA_DK, GLA_DV), lambda i: (i, 0, 0, 0))
    body = functools.partial(_gla_body, n_seq=g, n_chunks=t // chunk, chunk=chunk)
    return pl.pallas_call(
        body,
        grid=(b // g,),
        in_specs=[blk(hk), blk(hk), blk(hv), blk(hk), st],
        out_specs=[blk(hv), st],
        out_shape=[jax.ShapeDtypeStruct((b, t, hv), F32),
                   jax.ShapeDtypeStruct((b, GLA_HEADS, GLA_DK, GLA_DV), F32)],
        scratch_shapes=[pltpu.VMEM((g, GLA_DV, GLA_HEADS * GLA_DK), F32)],
        compiler_params=_cparams(("parallel",)),
        name="gla",
    )(gq, gk, gv, la, s0)


def _mix_out_ffn_body(x_ref, mod2_ref, mod3_ref, om_ref, og_ref, gg_ref, go_ref, wlat_ref, wom_ref, wog_ref,
                      g_ref, w1_ref, w3_ref, w2_ref, gf_ref, o_ref, *, d, sample, final, rep):
    gate = _mod_cols(mod2_ref, 0, d, rep)
    om = om_ref[...]
    if sample:
        om = _dot(om.astype(BF16), wlat_ref[...]).astype(BF16)
    parts = []
    for hh in range(GLA_HEADS):
        vs = slice(hh * GLA_DV, (hh + 1) * GLA_DV)
        parts.append(_rms(og_ref[:, vs], go_ref[...]) * _silu(gg_ref[:, vs]))
    og = jnp.concatenate(parts, axis=1).astype(BF16)
    mix = _dot(om, wom_ref[...]) + _dot(og, wog_ref[...])
    x = x_ref[...] + gate * mix
    out = _ffn_half_step(x, mod3_ref, g_ref, w1_ref, w3_ref, w2_ref, d, rep)
    if final:
        out = _rms(out, gf_ref[...])
    o_ref[...] = out


def _mix_out_ffn(x, mod, o_mla, o_gla, gg, g_gla_o, w_lat, w_o_mla, w_o_gla, g, w1, w3, w2, g_final,
                 sample, final, tm):
    n, d = x.shape
    ff = w1.shape[1]
    hv = GLA_HEADS * GLA_DV
    tok = lambda w: pl.BlockSpec((tm, w), lambda i: (i, 0))
    body = functools.partial(_mix_out_ffn_body, d=d, sample=sample, final=final, rep=mod.rep)
    return pl.pallas_call(
        body,
        grid=(n // tm,),
        in_specs=[tok(d), mod.spec(tm, d, 5, 1), mod.spec(tm, d, 2), tok(o_mla.shape[1]), tok(hv), tok(hv),
                  _resident((1, GLA_DV)), _resident(w_lat.shape), _resident(w_o_mla.shape),
                  _resident(w_o_gla.shape),
                  _resident((1, d)), _resident((d, ff)), _resident((d, ff)), _resident((ff, d)),
                  _resident((1, d))],
        out_specs=tok(d),
        out_shape=jax.ShapeDtypeStruct((n, d), F32),
        compiler_params=_cparams(("parallel",)),
        name="mix_out_ffn_sample" if sample else "mix_out_ffn_prompt",
    )(x, mod.arr, mod.arr, o_mla, o_gla, gg, g_gla_o.reshape(1, GLA_DV), w_lat, w_o_mla, w_o_gla,
      g.reshape(1, d), w1, w3, w2, g_final.reshape(1, d))


def _prep_w_in(w_in):
    d = w_in.shape[0]
    o = 0
    parts = {}
    for name, size in (("qa", MLA_Q_LORA), ("kva", MLA_KV_LORA), ("kr", MLA_ROPE),
                       ("gq", GLA_HEADS * GLA_DK), ("gk", GLA_HEADS * GLA_DK), ("gv", GLA_HEADS * GLA_DV),
                       ("ga", GLA_GATE_RANK), ("gg", GLA_HEADS * GLA_DV)):
        parts[name] = w_in[:, o:o + size]
        o += size
    blk_a = jnp.pad(jnp.concatenate([parts["kr"], parts["ga"]], axis=1),
                    ((0, 0), (0, LANE - MLA_ROPE - GLA_GATE_RANK)))
    w = jnp.concatenate([parts["qa"], parts["kva"], parts["gq"] * (GLA_DK ** -0.5), parts["gk"],
                         parts["gv"], parts["gg"], blk_a], axis=1)
    return w.astype(BF16)


def _prep_wq(w_qb):
    ql = w_qb.shape[0]
    w = jnp.pad(w_qb, ((0, 0), (0, 0), (0, HEAD_PAD - MLA_NOPE - MLA_ROPE)))
    return (w.reshape(ql, MLA_HEADS * HEAD_PAD) * (MLA_SCALE * LOG2_E)).astype(BF16)


def _prep_wkv_prompt(w_kvb):
    rows = w_kvb.shape[0]
    k = jnp.pad(w_kvb[..., :MLA_NOPE], ((0, 0), (0, 0), (0, HEAD_PAD - MLA_NOPE)))
    v = w_kvb[..., MLA_NOPE:]
    return jnp.concatenate([k.reshape(rows, -1), v.reshape(rows, -1)], axis=1).astype(BF16)


def _prep_wq_sample(w_kvb):
    same_head = jnp.eye(MLA_HEADS, dtype=F32)[:, None, :, None]
    pad_n = ((0, 0), (0, HEAD_PAD - MLA_NOPE), (0, 0), (0, 0))
    lat = jnp.pad(jnp.transpose(w_kvb[..., :MLA_NOPE], (1, 2, 0))[:, :, None, :] * same_head, pad_n)
    sel = jnp.eye(HEAD_PAD, MLA_ROPE, k=-MLA_NOPE, dtype=F32)
    rope = sel[None, :, None, :] * same_head
    rows = MLA_HEADS * HEAD_PAD
    return jnp.concatenate([lat.reshape(rows, -1), rope.reshape(rows, -1)], axis=1).astype(BF16)


def _prep_w_lat_out(w_kvb):
    c = w_kvb.shape[0]
    same_head = jnp.eye(MLA_HEADS, dtype=F32)[:, None, :, None]
    w = jnp.transpose(w_kvb[..., MLA_NOPE:], (1, 0, 2))[:, :, None, :] * same_head
    return w.reshape(MLA_HEADS * c, MLA_HEADS * MLA_V).astype(BF16)


def _prep_w_o(w_o):
    n_mla = MLA_HEADS * MLA_V
    return w_o[:n_mla].astype(BF16), w_o[n_mla:].astype(BF16)


def _prep_gate(w_gate_b):
    return jnp.pad(w_gate_b, ((MLA_ROPE, LANE - MLA_ROPE - GLA_GATE_RANK), (0, 0))).astype(BF16)


def _rope_table(pos):
    hr = MLA_ROPE // 2
    inv = ROPE_THETA ** (-jnp.arange(0, MLA_ROPE, 2, dtype=F32) / MLA_ROPE)
    ang = pos[:, None] * inv[None, :]
    cos, sin = jnp.cos(ang), jnp.sin(ang)
    t = pos.shape[0]
    z = lambda w: jnp.zeros((t, w), F32)

    def tabs(start):
        after = LANE - start - 2 * hr
        c = jnp.concatenate([jnp.ones((t, start), F32), cos, cos, z(after)], axis=1)
        s_from_below = jnp.concatenate([z(start + hr), sin, z(after)], axis=1)
        s_from_above = jnp.concatenate([z(start), -sin, z(hr + after)], axis=1)
        return [c, s_from_below, s_from_above]

    return jnp.concatenate(tabs(MLA_NOPE) + tabs(0), axis=1)


TM_TOKENS = 512
TM_FFN = 1024
TM_SAMPLE = 512
TQ_PROMPT = 512
HEAD_PAIRS_PER_STEP = 4
PAGES_PER_CHUNK = 64
GLA_SAMPLE_PAD = 16
GLA_SAMPLE_SEQS = 8
GLA_UNROLL = 8


def kernel(x_prompt, x_sample, cache_ckv, cache_krope, state_gla, page_table, c_prompt, c_sample,
           w_ada, b_ada, norm_ffn1, ffn1_w1, ffn1_w3, ffn1_w2, norm_mix, w_in, g_qa, w_qb, g_kva, w_kvb,
           w_gate_b, b_gate, g_gla_o, w_o, norm_ffn2, ffn2_w1, ffn2_w3, ffn2_w2, norm_final):
    bp, tp, d = x_prompt.shape
    bs, ts, _ = x_sample.shape
    depth = w_ada.shape[0]
    n_pages = page_table.shape[1]
    past_len = n_pages * PAGE_SIZE

    tab_p = _rope_table(jnp.arange(tp, dtype=F32))
    tm_s = min(TM_SAMPLE, bs * ts)
    tab_s = jnp.tile(_rope_table(past_len + jnp.arange(ts, dtype=F32)), (tm_s // ts, 1))

    hp = x_prompt.reshape(bp * tp, d)
    hs = x_sample.reshape(bs * ts, d)
    outs = [[] for _ in range(6)]
    for l in range(depth):
        m = _ada(jnp.concatenate([c_prompt, c_sample], axis=0), w_ada[l], b_ada[l])
        mod_p = _Mod(m[:bp], tp, TM_FFN)
        mod_s = _Mod(m[bp:], ts, tm_s)

        f1 = (norm_ffn1[l], ffn1_w1[l].astype(BF16), ffn1_w3[l].astype(BF16), ffn1_w2[l].astype(BF16))
        f2 = (norm_ffn2[l], ffn2_w1[l].astype(BF16), ffn2_w3[l].astype(BF16), ffn2_w2[l].astype(BF16))
        w_in_r = _prep_w_in(w_in[l])
        wq = _prep_wq(w_qb[l])
        wkv_p = _prep_wkv_prompt(w_kvb[l])
        wq_s = _prep_wq_sample(w_kvb[l])
        w_lat = _prep_w_lat_out(w_kvb[l])
        w_o_mla, w_o_gla = _prep_w_o(w_o[l])
        wgate = _prep_gate(w_gate_b[l])
        last = l == depth - 1

        hp = _ffn(hp, mod_p, 0, *f1, TM_FFN)
        q, k, v, ckv_p, kr_p, gq, gk, gv, la, gg = _mix_in(
            hp, mod_p, norm_mix[l], w_in_r, g_qa[l], wq, g_kva[l], wkv_p, wgate, b_gate[l], tab_p,
            False, TM_TOKENS, bp)
        o_mla = _attn_prompt(q, k, v, TQ_PROMPT, HEAD_PAIRS_PER_STEP).reshape(bp * tp, -1)
        seq3 = lambda a: a.reshape(bp, tp, -1)
        s0_p = jnp.zeros((bp, GLA_HEADS, GLA_DK, GLA_DV), F32)
        o_gla, s_p = _gla(seq3(gq), seq3(gk), seq3(gv), seq3(la), s0_p, GLA_CHUNK, 1)
        hp = _mix_out_ffn(hp, mod_p, o_mla, o_gla.reshape(bp * tp, -1), gg, g_gla_o[l], w_lat, w_o_mla,
                          w_o_gla, *f2, norm_final, False, last, TM_TOKENS)

        hs = _ffn(hs, mod_s, 0, *f1, tm_s)
        ql, ckv_s, kr_s, gq, gk, gv, la, gg = _mix_in(
            hs, mod_s, norm_mix[l], w_in_r, g_qa[l], wq, g_kva[l], wq_s, wgate, b_gate[l], tab_s,
            True, tm_s, bs)
        n_lat = MLA_HEADS * MLA_KV_LORA
        rows = ts * MLA_HEADS
        q_lat = ql[:, :n_lat].reshape(bs, rows, MLA_KV_LORA)
        q_rope = ql[:, n_lat:].reshape(bs, rows, MLA_ROPE)
        o_lat = _attn_sample(page_table, q_lat, q_rope, ckv_s.reshape(bs, ts, -1), kr_s.reshape(bs, ts, -1),
                             cache_ckv, jnp.swapaxes(cache_krope, 2, 3), l, PAGES_PER_CHUNK)
        pad = lambda a: jnp.pad(a.reshape(bs, ts, -1), ((0, 0), (0, GLA_SAMPLE_PAD - ts), (0, 0)))
        o_gla, s_s = _gla(pad(gq), pad(gk), pad(gv), pad(la), state_gla[l], GLA_SAMPLE_PAD, GLA_SAMPLE_SEQS)
        hs = _mix_out_ffn(hs, mod_s, o_lat.reshape(bs * ts, n_lat), o_gla[:, :ts].reshape(bs * ts, -1), gg,
                          g_gla_o[l], w_lat, w_o_mla, w_o_gla, *f2, norm_final, True, last, tm_s)

        for lst, a in zip(outs, (ckv_p.reshape(bp, tp, -1), kr_p.reshape(bp, tp, -1), s_p,
                                 ckv_s.reshape(bs, ts, -1), kr_s.reshape(bs, ts, -1), s_s)):
            lst.append(a)

    return (hp.reshape(bp, tp, d), hs.reshape(bs, ts, d)) + tuple(jnp.stack(o) for o in outs)
```

```python
import functools
import math

import jax
import jax.numpy as jnp
from jax import lax
from jax.experimental import pallas as pl
from jax.experimental.pallas import tpu as pltpu

F32 = jnp.float32
BF16 = jnp.bfloat16

PAGE_SIZE = 128
MLA_HEADS = 8
MLA_NOPE = 64
MLA_ROPE = 32
MLA_V = 64
MLA_Q_LORA = 384
MLA_KV_LORA = 256
MLA_SCALE = (MLA_NOPE + MLA_ROPE) ** -0.5
LOG2_E = math.log2(math.e)
ROPE_THETA = 10000.0
GLA_HEADS = 4
GLA_DK = 64
GLA_DV = 128
GLA_GATE_RANK = 16
GLA_GATE_TAU = 16.0
GLA_CHUNK = 64
N_ADA = 9
EPS = 1e-6

LANE = 128
HEAD_PAD = 128
VMEM_LIMIT_BYTES = 56 * 1024 * 1024

C_QA = 0
C_KVA = C_QA + MLA_Q_LORA
C_GQ = C_KVA + MLA_KV_LORA
C_GK = C_GQ + GLA_HEADS * GLA_DK
C_GV = C_GK + GLA_HEADS * GLA_DK
C_GG = C_GV + GLA_HEADS * GLA_DV
C_BLKA = C_GG + GLA_HEADS * GLA_DV
C_END = C_BLKA + LANE
N_ROPE_TABS = 6


def _cparams(sem):
    return pltpu.CompilerParams(dimension_semantics=sem, vmem_limit_bytes=VMEM_LIMIT_BYTES)


def _resident(shape):
    return pl.BlockSpec(shape, lambda *_: (0,) * len(shape), pipeline_mode=pl.Buffered(1))


def _rms(x, g):
    return x * lax.rsqrt(jnp.mean(x * x, axis=-1, keepdims=True) + EPS) * g


def _silu(x):
    return x * jax.nn.sigmoid(x)


def _dot(a, b):
    return jnp.dot(a, b, preferred_element_type=F32)


def _dot_nt(a, b):
    return lax.dot_general(a, b, (((1,), (1,)), ((), ())), preferred_element_type=F32)


def _dot_tn(a, b):
    return lax.dot_general(a, b, (((0,), (0,)), ((), ())), preferred_element_type=F32)


def _ada_body(c_ref, w_ref, b_ref, o_ref):
    c = _silu(c_ref[...]).astype(BF16)
    o_ref[...] = _dot(c, w_ref[...].astype(BF16)) + b_ref[...]


def _ada(c_all, w_ada, b_ada):
    n, d = c_all.shape
    n_out = w_ada.shape[1]
    tn = 1024
    return pl.pallas_call(
        _ada_body,
        grid=(n_out // tn,),
        in_specs=[pl.BlockSpec((n, d), lambda j: (0, 0)),
                  pl.BlockSpec((d, tn), lambda j: (0, j)),
                  pl.BlockSpec((1, tn), lambda j: (0, j))],
        out_specs=pl.BlockSpec((n, tn), lambda j: (0, j)),
        out_shape=jax.ShapeDtypeStruct((n, n_out), F32),
        compiler_params=_cparams(("arbitrary",)),
        name="ada",
    )(c_all, w_ada, b_ada.reshape(1, n_out))


class _Mod:
    def __init__(self, m, tokens_per_seq, tm):
        self.tokens_per_seq = tokens_per_seq
        self.rep = tokens_per_seq if tm > tokens_per_seq else 1
        self.arr = m if self.rep > 1 else m.reshape(m.shape[0], 1, m.shape[1])

    def spec(self, tm, d, blk, width=3):
        if self.rep > 1:
            return pl.BlockSpec((tm // self.rep, width * d), lambda i: (i, blk))
        per = self.tokens_per_seq // tm
        return pl.BlockSpec((None, 1, width * d), lambda i: (i // per, 0, blk))


def _mod_cols(mod_ref, lo, hi, rep):
    v = mod_ref[:, lo:hi]
    if rep > 1:
        g, w = v.shape
        v = jnp.broadcast_to(v[:, None, :], (g, rep, w)).reshape(g * rep, w)
    return v


FF_STEP = 1024


def _ff_chunks(ff):
    return tuple((c, min(c + FF_STEP, ff)) for c in range(0, ff, FF_STEP))


def _ffn_half_step(x, mod_ref, g_ref, w1_ref, w3_ref, w2_ref, d, rep):
    shift, scale, gate = (_mod_cols(mod_ref, k * d, (k + 1) * d, rep) for k in range(3))
    h = (_rms(x, g_ref[...]) * (1.0 + scale) + shift).astype(BF16)
    y = None
    for c0, c1 in _ff_chunks(w1_ref.shape[1]):
        a = _dot(h, w1_ref[:, c0:c1])
        b = _dot(h, w3_ref[:, c0:c1])
        u = (_silu(a) * b).astype(BF16)
        part = _dot(u, w2_ref[c0:c1, :])
        y = part if y is None else y + part
    return x + 0.5 * gate * y


def _ffn_body(x_ref, mod_ref, g_ref, w1_ref, w3_ref, w2_ref, o_ref, *, d, rep):
    o_ref[...] = _ffn_half_step(x_ref[...], mod_ref, g_ref, w1_ref, w3_ref, w2_ref, d, rep)


def _ffn(x, mod, third, g, w1, w3, w2, tm):
    n, d = x.shape
    ff = w1.shape[1]
    return pl.pallas_call(
        functools.partial(_ffn_body, d=d, rep=mod.rep),
        grid=(n // tm,),
        in_specs=[pl.BlockSpec((tm, d), lambda i: (i, 0)),
                  mod.spec(tm, d, third),
                  _resident((1, d)), _resident((d, ff)), _resident((d, ff)), _resident((ff, d))],
        out_specs=pl.BlockSpec((tm, d), lambda i: (i, 0)),
        out_shape=jax.ShapeDtypeStruct((n, d), F32),
        compiler_params=_cparams(("parallel",)),
        name="ffn",
    )(x, mod.arr, g.reshape(1, d), w1, w3, w2)


def _mix_in_body(x_ref, mod_ref, g_ref, win_ref, gqa_ref, wq_ref, gkva_ref, wkv_ref, wgate_ref,
                 bgate_ref, tab_ref, *out_refs, d, sample, rep):
    if sample:
        (ql_ref, ckv_ref, kr_ref, gq_ref, gk_ref, gv_ref, la_ref, gg_ref) = out_refs
    else:
        (q_ref, k_ref, v_ref, ckv_ref, kr_ref, gq_ref, gk_ref, gv_ref, la_ref, gg_ref) = out_refs
    x = x_ref[...]
    shift, scale = _mod_cols(mod_ref, 0, d, rep), _mod_cols(mod_ref, d, 2 * d, rep)
    h = (_rms(x, g_ref[...]) * (1.0 + scale) + shift).astype(BF16)
    proj = _dot(h, win_ref[...])

    q_tabs = tuple(tab_ref[:, i * LANE:(i + 1) * LANE] for i in range(3))
    k_tabs = tuple(tab_ref[:, i * LANE:(i + 1) * LANE] for i in range(3, 6))
    hr = MLA_ROPE // 2

    def rotate(x, tabs):
        c, s_from_below, s_from_above = tabs
        return x * c + pltpu.roll(x, hr, 1) * s_from_below + pltpu.roll(x, LANE - hr, 1) * s_from_above

    qn = _rms(proj[:, C_QA:C_KVA], gqa_ref[...]).astype(BF16)
    qq = _dot(qn, wq_ref[...])
    half = MLA_HEADS * HEAD_PAD
    q_heads = [rotate(qq[:, hh * HEAD_PAD:(hh + 1) * HEAD_PAD], q_tabs) for hh in range(MLA_HEADS)]

    ckv = _rms(proj[:, C_KVA:C_GQ], gkva_ref[...])
    ckv_ref[...] = ckv
    blk_a = proj[:, C_BLKA:C_END]
    kr_blk = rotate(blk_a, k_tabs)
    kr_ref[...] = kr_blk[:, 0:MLA_ROPE]

    if sample:
        q_all = jnp.concatenate(q_heads, axis=1).astype(BF16)
        ql_ref[...] = _dot(q_all, wkv_ref[...]).astype(BF16)
    else:
        for hh in range(MLA_HEADS):
            q_ref[hh] = q_heads[hh].astype(BF16)
        kv = _dot(ckv.astype(BF16), wkv_ref[...])
        kr_in_head = pltpu.roll(kr_blk, MLA_NOPE, 1)
        for hh in range(MLA_HEADS):
            k_ref[hh] = (kv[:, hh * HEAD_PAD:(hh + 1) * HEAD_PAD] + kr_in_head).astype(BF16)
        for pr in range(MLA_HEADS * MLA_V // LANE):
            v_ref[pr] = kv[:, half + pr * LANE:half + (pr + 1) * LANE].astype(BF16)

    gq_ref[...] = proj[:, C_GQ:C_GK]
    gk_ref[...] = proj[:, C_GK:C_GV]
    gv_ref[...] = proj[:, C_GV:C_GG].astype(BF16)
    gg_ref[...] = proj[:, C_GG:C_BLKA]
    z = _dot(blk_a.astype(BF16), wgate_ref[...]) + bgate_ref[...]
    la_ref[...] = (jnp.minimum(z, 0.0) - jnp.log1p(jnp.exp(-jnp.abs(z)))) / GLA_GATE_TAU


def _mix_in(x, mod, g, w_in_r, g_qa, wq, g_kva, wkv, wgate, b_gate, tab, sample, tm, n_seq):
    n, d = x.shape
    t_seq = n // n_seq
    n_pos_tiles = tab.shape[0] // tm
    hk = GLA_HEADS * GLA_DK
    hv = GLA_HEADS * GLA_DV
    tok = lambda w: pl.BlockSpec((tm, w), lambda i: (i, 0))
    tok_shape = lambda w, dt=F32: jax.ShapeDtypeStruct((n, w), dt)
    gla_specs = [tok(hk), tok(hk), tok(hv), tok(hk), tok(hv)]
    gla_shapes = [tok_shape(hk), tok_shape(hk), tok_shape(hv, BF16), tok_shape(hk), tok_shape(hv)]
    if sample:
        wl = wkv.shape[1]
        out_specs = [tok(wl), tok(MLA_KV_LORA), tok(MLA_ROPE)] + gla_specs
        out_shape = [tok_shape(wl, BF16), tok_shape(MLA_KV_LORA), tok_shape(MLA_ROPE)] + gla_shapes
    else:
        per = t_seq // tm
        n_vp = MLA_HEADS * MLA_V // LANE
        head_spec = lambda nh: pl.BlockSpec((None, nh, tm, HEAD_PAD), lambda i: (i // per, 0, i % per, 0))
        head_shape = lambda nh: jax.ShapeDtypeStruct((n_seq, nh, t_seq, HEAD_PAD), BF16)
        out_specs = ([head_spec(MLA_HEADS)] * 2 + [head_spec(n_vp), tok(MLA_KV_LORA), tok(MLA_ROPE)]
                     + gla_specs)
        out_shape = ([head_shape(MLA_HEADS)] * 2 + [head_shape(n_vp), tok_shape(MLA_KV_LORA),
                                                    tok_shape(MLA_ROPE)] + gla_shapes)
    body = functools.partial(_mix_in_body, d=d, sample=sample, rep=mod.rep)
    return pl.pallas_call(
        body,
        grid=(n // tm,),
        in_specs=[pl.BlockSpec((tm, d), lambda i: (i, 0)),
                  mod.spec(tm, d, 1),
                  _resident((1, d)), _resident(w_in_r.shape), _resident((1, MLA_Q_LORA)),
                  _resident(wq.shape), _resident((1, MLA_KV_LORA)), _resident(wkv.shape),
                  _resident(wgate.shape), _resident((1, hk)),
                  pl.BlockSpec((tm, N_ROPE_TABS * LANE), lambda i: (i % n_pos_tiles, 0))],
        out_specs=out_specs,
        out_shape=out_shape,
        compiler_params=_cparams(("parallel",)),
        name="mix_in_sample" if sample else "mix_in_prompt",
    )(x, mod.arr, g.reshape(1, d), w_in_r, g_qa.reshape(1, -1), wq, g_kva.reshape(1, -1), wkv,
      wgate, b_gate.reshape(1, hk), tab)


def _attn_prompt_body(q_ref, k_ref, v_ref, o_ref, *, tq, n_pairs):
    n_heads = 2 * n_pairs
    i = pl.program_id(2)
    qs = [q_ref[hh] for hh in range(n_heads)]

    def scores(hh, j):
        off = pl.multiple_of(j * tq, tq)
        return _dot_nt(qs[hh], k_ref[hh, pl.ds(off, tq), :]), v_ref[hh // 2, pl.ds(off, tq), :]

    init = []
    for hh in range(n_heads):
        s, v = scores(hh, i)
        row = lax.broadcasted_iota(jnp.int32, s.shape, 0)
        col = lax.broadcasted_iota(jnp.int32, s.shape, 1)
        s = jnp.where(col <= row, s, jnp.finfo(F32).min)
        m = jnp.max(s, axis=-1, keepdims=True)
        p = jnp.exp2(s - m)
        init += [m, jnp.sum(p, axis=-1, keepdims=True), _dot(p.astype(BF16), v)]

    def step(j, carry):
        out = []
        for hh in range(n_heads):
            m, l, acc = carry[3 * hh:3 * hh + 3]
            s, v = scores(hh, j)
            m_new = jnp.maximum(m, jnp.max(s, axis=-1, keepdims=True))
            alpha = jnp.exp2(m - m_new)
            p = jnp.exp2(s - m_new)
            out += [m_new, alpha * l + jnp.sum(p, axis=-1, keepdims=True),
                    alpha * acc + _dot(p.astype(BF16), v)]
        return tuple(out)

    fin = lax.fori_loop(0, i, step, tuple(init))
    for pr in range(n_pairs):
        e, o = 6 * pr, 6 * pr + 3
        o_even, o_odd = fin[e + 2] / fin[e + 1], fin[o + 2] / fin[o + 1]
        lane = lax.broadcasted_iota(jnp.int32, o_even.shape, 1)
        o_ref[:, pr * LANE:(pr + 1) * LANE] = jnp.where(lane < MLA_V, o_even, o_odd).astype(o_ref.dtype)


def _attn_prompt(q, k, v, tq, n_pairs):
    b, hh, t, w = q.shape
    nh = 2 * n_pairs
    body = functools.partial(_attn_prompt_body, tq=tq, n_pairs=n_pairs)
    return pl.pallas_call(
        body,
        grid=(b, hh // nh, t // tq),
        in_specs=[pl.BlockSpec((None, nh, tq, w), lambda bi, hi, i: (bi, hi, i, 0)),
                  pl.BlockSpec((None, nh, t, w), lambda bi, hi, i: (bi, hi, 0, 0)),
                  pl.BlockSpec((None, n_pairs, t, w), lambda bi, hi, i: (bi, hi, 0, 0))],
        out_specs=pl.BlockSpec((None, tq, n_pairs * w), lambda bi, hi, i: (bi, i, hi)),
        out_shape=jax.ShapeDtypeStruct((b, t, hh // 2 * w), BF16),
        compiler_params=_cparams(("parallel", "parallel", "arbitrary")),
        name="attn_prompt",
    )(q, k, v)


def _attn_sample_body(pt_ref, ql_ref, qr_ref, cn_ref, kn_ref, ckv_hbm, kr_hbm, o_ref, ckv_buf, kr_buf,
                      ckb_buf, krb_buf, sems, *, layer, n_chunks, pg):
    b = pl.program_id(0)
    n_seq = pl.num_programs(0)

    def copies(seq, c):
        out = []
        for p in range(pg):
            page = pt_ref[seq, c * pg + p]
            rows = pl.ds(p * PAGE_SIZE, PAGE_SIZE)
            out.append(pltpu.make_async_copy(ckv_hbm.at[layer, page], ckv_buf.at[c, rows, :], sems.at[c, 0]))
            out.append(pltpu.make_async_copy(kr_hbm.at[layer, page], kr_buf.at[c, :, rows], sems.at[c, 1]))
        return out

    @pl.when(b == 0)
    def _():
        for c in range(n_chunks):
            for cp in copies(0, c):
                cp.start()

    ql = ql_ref[...]
    qr = qr_ref[...]

    cn = cn_ref[...]
    s = _dot_nt(ql.astype(F32), cn) + _dot_nt(qr.astype(F32), kn_ref[...])
    row_tok = lax.broadcasted_iota(jnp.int32, s.shape, 0) // MLA_HEADS
    col = lax.broadcasted_iota(jnp.int32, s.shape, 1)
    s = jnp.where(col <= row_tok, s, jnp.finfo(F32).min)
    m = jnp.max(s, axis=-1, keepdims=True)
    p = jnp.exp2(s - m)
    l = jnp.sum(p, axis=-1, keepdims=True)
    acc = _dot(p, cn)

    nxt = jnp.minimum(b + 1, n_seq - 1)

    def slot_filled(c):
        return (pltpu.make_async_copy(ckv_buf.at[c], ckv_buf.at[c], sems.at[c, 0]),
                pltpu.make_async_copy(kr_buf.at[c], kr_buf.at[c], sems.at[c, 1]))

    def arrive(c):
        for cp in slot_filled(c):
            cp.wait()

    def stage(c):
        ckb_buf[c] = ckv_buf[c].astype(BF16)
        krb_buf[c] = kr_buf[c].astype(BF16)
        for cp in copies(nxt, c):
            cp.start()

    arrive(0)
    stage(0)
    pending = None
    for c in range(n_chunks):
        s = _dot_nt(ql, ckb_buf[c]) + _dot(qr, krb_buf[c])
        if c + 1 < n_chunks:
            arrive(c + 1)
            stage(c + 1)
        if pending is not None:
            p_prev, alpha_prev = pending
            acc = alpha_prev * acc + _dot(p_prev, ckb_buf[c - 1])
        m_new = jnp.maximum(m, jnp.max(s, axis=-1, keepdims=True))
        alpha = jnp.exp2(m - m_new)
        p = jnp.exp2(s - m_new)
        l = alpha * l + jnp.sum(p, axis=-1, keepdims=True)
        pending = (p.astype(BF16), alpha)
        m = m_new
    p_prev, alpha_prev = pending
    acc = alpha_prev * acc + _dot(p_prev, ckb_buf[n_chunks - 1])
    o_ref[...] = acc / l

    @pl.when(b == n_seq - 1)
    def _():
        for c in range(n_chunks):
            for cp in slot_filled(c):
                cp.wait()


def _attn_sample(page_table, q_lat, q_rope, ckv_new, kr_new, cache_ckv, cache_krope_t, layer, pg):
    n_seq, rows, c = q_lat.shape
    t_new = ckv_new.shape[1]
    r = q_rope.shape[2]
    n_chunks = page_table.shape[1] // pg
    seq = lambda shape: pl.BlockSpec((None,) + shape, lambda b, pt: (b, 0, 0))
    body = functools.partial(_attn_sample_body, layer=layer, n_chunks=n_chunks, pg=pg)
    grid_spec = pltpu.PrefetchScalarGridSpec(
        num_scalar_prefetch=1,
        grid=(n_seq,),
        in_specs=[seq((rows, c)), seq((rows, r)), seq((t_new, c)), seq((t_new, r)),
                  pl.BlockSpec(memory_space=pl.ANY), pl.BlockSpec(memory_space=pl.ANY)],
        out_specs=seq((rows, c)),
        scratch_shapes=[pltpu.VMEM((n_chunks, pg * PAGE_SIZE, c), F32),
                        pltpu.VMEM((n_chunks, r, pg * PAGE_SIZE), F32),
                        pltpu.VMEM((n_chunks, pg * PAGE_SIZE, c), BF16),
                        pltpu.VMEM((n_chunks, r, pg * PAGE_SIZE), BF16),
                        pltpu.SemaphoreType.DMA((n_chunks, 2))],
    )
    return pl.pallas_call(
        body,
        grid_spec=grid_spec,
        out_shape=jax.ShapeDtypeStruct((n_seq, rows, c), F32),
        compiler_params=_cparams(("arbitrary",)),
        name="attn_sample",
    )(page_table, q_lat, q_rope, ckv_new, kr_new, cache_ckv, cache_krope_t)


def _gla_body(q_ref, k_ref, v_ref, la_ref, s0_ref, o_ref, s_ref, st_sc, *, n_seq, n_chunks, chunk):
    hk = GLA_HEADS * GLA_DK
    for sq in range(n_seq):
        st_sc[sq] = s0_ref[sq].reshape(hk, GLA_DV).T
    rows = GLA_HEADS * chunk
    r = lax.broadcasted_iota(jnp.int32, (chunk, chunk), 0)
    c = lax.broadcasted_iota(jnp.int32, (chunk, chunk), 1)
    tri = (c <= r).astype(BF16)
    rs = lax.broadcasted_iota(jnp.int32, (rows, chunk), 0)
    cs = lax.broadcasted_iota(jnp.int32, (rows, chunk), 1)
    causal_stacked = cs <= rs % chunk
    lane_head = lax.broadcasted_iota(jnp.int32, (chunk, hk), 1) // GLA_DK
    zero = jnp.zeros((chunk, hk), BF16)

    def stack_heads(x):
        return jnp.concatenate([jnp.where(lane_head == hh, x, zero) for hh in range(GLA_HEADS)], axis=0)

    def one_chunk(ci, _):
        off = pl.multiple_of(ci * chunk, chunk)
        for sq in range(n_seq):
            g = la_ref[sq, pl.ds(off, chunk), :]
            g_hi = g.astype(BF16)
            g_lo = (g - g_hi.astype(F32)).astype(BF16)
            b = _dot(tri, g_hi) + _dot(tri, g_lo)
            b_last = b[chunk - 1:chunk, :]
            q = q_ref[sq, pl.ds(off, chunk), :]
            k = k_ref[sq, pl.ds(off, chunk), :]
            v = v_ref[sq, pl.ds(off, chunk), :].astype(BF16)
            qe = stack_heads((q * jnp.exp(b)).astype(BF16))
            ke = (k * jnp.exp(-b)).astype(BF16)
            kd = stack_heads((k * jnp.exp(b_last - b)).astype(BF16))
            st = st_sc[sq]
            a = jnp.where(causal_stacked, _dot_nt(qe, ke), 0.0).astype(BF16)
            intra = _dot(a, v)
            inter = _dot_nt(qe, st.astype(BF16))
            o_ref[sq, pl.ds(off, chunk), :] = jnp.concatenate(
                [intra[hh * chunk:(hh + 1) * chunk, hh * GLA_DV:(hh + 1) * GLA_DV]
                 + inter[hh * chunk:(hh + 1) * chunk] for hh in range(GLA_HEADS)], axis=1)
            v_stacked = jnp.concatenate([v[:, hh * GLA_DV:(hh + 1) * GLA_DV] for hh in range(GLA_HEADS)], axis=0)
            st_sc[sq] = st * jnp.exp(b_last) + _dot_tn(v_stacked, kd)
        return 0

    lax.fori_loop(0, n_chunks, one_chunk, 0, unroll=min(GLA_UNROLL, n_chunks))
    for sq in range(n_seq):
        s_ref[sq] = st_sc[sq].T.reshape(GLA_HEADS, GLA_DK, GLA_DV)


def _gla(gq, gk, gv, la, s0, chunk, seqs_per_step):
    b, t, hk = gq.shape
    hv = gv.shape[2]
    g = seqs_per_step
    blk = lambda w: pl.BlockSpec((g, t, w), lambda i: (i, 0, 0))
    st = pl.BlockSpec((g, GLA_HEADS, GLA_DK, GLA_DV), lambda i: (i, 0, 0, 0))
    body = functools.partial(_gla_body, n_seq=g, n_chunks=t // chunk, chunk=chunk)
    return pl.pallas_call(
        body,
        grid=(b // g,),
        in_specs=[blk(hk), blk(hk), blk(hv), blk(hk), st],
        out_specs=[blk(hv), st],
        out_shape=[jax.ShapeDtypeStruct((b, t, hv), F32),
                   jax.ShapeDtypeStruct((b, GLA_HEADS, GLA_DK, GLA_DV), F32)],
        scratch_shapes=[pltpu.VMEM((g, GLA_DV, GLA_HEADS * GLA_DK), F32)],
        compiler_params=_cparams(("parallel",)),
        name="gla",
    )(gq, gk, gv, la, s0)


def _mix_out_ffn_body(x_ref, mod2_ref, mod3_ref, om_ref, og_ref, gg_ref, go_ref, wlat_ref, wom_ref, wog_ref,
                      g_ref, w1_ref, w3_ref, w2_ref, gf_ref, o_ref, *, d, sample, final, rep):
    gate = _mod_cols(mod2_ref, 0, d, rep)
    om = om_ref[...]
    if sample:
        om = _dot(om.astype(BF16), wlat_ref[...]).astype(BF16)
    parts = []
    for hh in range(GLA_HEADS):
        vs = slice(hh * GLA_DV, (hh + 1) * GLA_DV)
        parts.append(_rms(og_ref[:, vs], go_ref[...]) * _silu(gg_ref[:, vs]))
    og = jnp.concatenate(parts, axis=1).astype(BF16)
    mix = _dot(om, wom_ref[...]) + _dot(og, wog_ref[...])
    x = x_ref[...] + gate * mix
    out = _ffn_half_step(x, mod3_ref, g_ref, w1_ref, w3_ref, w2_ref, d, rep)
    if final:
        out = _rms(out, gf_ref[...])
    o_ref[...] = out


def _mix_out_ffn(x, mod, o_mla, o_gla, gg, g_gla_o, w_lat, w_o_mla, w_o_gla, g, w1, w3, w2, g_final,
                 sample, final, tm):
    n, d = x.shape
    ff = w1.shape[1]
    hv = GLA_HEADS * GLA_DV
    tok = lambda w: pl.BlockSpec((tm, w), lambda i: (i, 0))
    body = functools.partial(_mix_out_ffn_body, d=d, sample=sample, final=final, rep=mod.rep)
    return pl.pallas_call(
        body,
        grid=(n // tm,),
        in_specs=[tok(d), mod.spec(tm, d, 5, 1), mod.spec(tm, d, 2), tok(o_mla.shape[1]), tok(hv), tok(hv),
                  _resident((1, GLA_DV)), _resident(w_lat.shape), _resident(w_o_mla.shape),
                  _resident(w_o_gla.shape),
                  _resident((1, d)), _resident((d, ff)), _resident((d, ff)), _resident((ff, d)),
                  _resident((1, d))],
        out_specs=tok(d),
        out_shape=jax.ShapeDtypeStruct((n, d), F32),
        compiler_params=_cparams(("parallel",)),
        name="mix_out_ffn_sample" if sample else "mix_out_ffn_prompt",
    )(x, mod.arr, mod.arr, o_mla, o_gla, gg, g_gla_o.reshape(1, GLA_DV), w_lat, w_o_mla, w_o_gla,
      g.reshape(1, d), w1, w3, w2, g_final.reshape(1, d))


def _prep_w_in(w_in):
    d = w_in.shape[0]
    o = 0
    parts = {}
    for name, size in (("qa", MLA_Q_LORA), ("kva", MLA_KV_LORA), ("kr", MLA_ROPE),
                       ("gq", GLA_HEADS * GLA_DK), ("gk", GLA_HEADS * GLA_DK), ("gv", GLA_HEADS * GLA_DV),
                       ("ga", GLA_GATE_RANK), ("gg", GLA_HEADS * GLA_DV)):
        parts[name] = w_in[:, o:o + size]
        o += size
    blk_a = jnp.pad(jnp.concatenate([parts["kr"], parts["ga"]], axis=1),
                    ((0, 0), (0, LANE - MLA_ROPE - GLA_GATE_RANK)))
    w = jnp.concatenate([parts["qa"], parts["kva"], parts["gq"] * (GLA_DK ** -0.5), parts["gk"],
                         parts["gv"], parts["gg"], blk_a], axis=1)
    return w.astype(BF16)


def _prep_wq(w_qb):
    ql = w_qb.shape[0]
    w = jnp.pad(w_qb, ((0, 0), (0, 0), (0, HEAD_PAD - MLA_NOPE - MLA_ROPE)))
    return (w.reshape(ql, MLA_HEADS * HEAD_PAD) * (MLA_SCALE * LOG2_E)).astype(BF16)


def _prep_wkv_prompt(w_kvb):
    rows = w_kvb.shape[0]
    k = jnp.pad(w_kvb[..., :MLA_NOPE], ((0, 0), (0, 0), (0, HEAD_PAD - MLA_NOPE)))
    v = w_kvb[..., MLA_NOPE:]
    return jnp.concatenate([k.reshape(rows, -1), v.reshape(rows, -1)], axis=1).astype(BF16)


def _prep_wq_sample(w_kvb):
    same_head = jnp.eye(MLA_HEADS, dtype=F32)[:, None, :, None]
    pad_n = ((0, 0), (0, HEAD_PAD - MLA_NOPE), (0, 0), (0, 0))
    lat = jnp.pad(jnp.transpose(w_kvb[..., :MLA_NOPE], (1, 2, 0))[:, :, None, :] * same_head, pad_n)
    sel = jnp.eye(HEAD_PAD, MLA_ROPE, k=-MLA_NOPE, dtype=F32)
    rope = sel[None, :, None, :] * same_head
    rows = MLA_HEADS * HEAD_PAD
    return jnp.concatenate([lat.reshape(rows, -1), rope.reshape(rows, -1)], axis=1).astype(BF16)


def _prep_w_lat_out(w_kvb):
    c = w_kvb.shape[0]
    same_head = jnp.eye(MLA_HEADS, dtype=F32)[:, None, :, None]
    w = jnp.transpose(w_kvb[..., MLA_NOPE:], (1, 0, 2))[:, :, None, :] * same_head
    return w.reshape(MLA_HEADS * c, MLA_HEADS * MLA_V).astype(BF16)


def _prep_w_o(w_o):
    n_mla = MLA_HEADS * MLA_V
    return w_o[:n_mla].astype(BF16), w_o[n_mla:].astype(BF16)


def _prep_gate(w_gate_b):
    return jnp.pad(w_gate_b, ((MLA_ROPE, LANE - MLA_ROPE - GLA_GATE_RANK), (0, 0))).astype(BF16)


def _rope_table(pos):
    hr = MLA_ROPE // 2
    inv = ROPE_THETA ** (-jnp.arange(0, MLA_ROPE, 2, dtype=F32) / MLA_ROPE)
    ang = pos[:, None] * inv[None, :]
    cos, sin = jnp.cos(ang), jnp.sin(ang)
    t = pos.shape[0]
    z = lambda w: jnp.zeros((t, w), F32)

    def tabs(start):
        after = LANE - start - 2 * hr
        c = jnp.concatenate([jnp.ones((t, start), F32), cos, cos, z(after)], axis=1)
        s_from_below = jnp.concatenate([z(start + hr), sin, z(after)], axis=1)
        s_from_above = jnp.concatenate([z(start), -sin, z(hr + after)], axis=1)
        return [c, s_from_below, s_from_above]

    return jnp.concatenate(tabs(MLA_NOPE) + tabs(0), axis=1)


TM_TOKENS = 512
TM_FFN = 1024
TM_SAMPLE = 512
TQ_PROMPT = 512
HEAD_PAIRS_PER_STEP = 4
PAGES_PER_CHUNK = 64
GLA_SAMPLE_PAD = 16
GLA_SAMPLE_SEQS = 8
GLA_UNROLL = 8


def kernel(x_prompt, x_sample, cache_ckv, cache_krope, state_gla, page_table, c_prompt, c_sample,
           w_ada, b_ada, norm_ffn1, ffn1_w1, ffn1_w3, ffn1_w2, norm_mix, w_in, g_qa, w_qb, g_kva, w_kvb,
           w_gate_b, b_gate, g_gla_o, w_o, norm_ffn2, ffn2_w1, ffn2_w3, ffn2_w2, norm_final):
    bp, tp, d = x_prompt.shape
    bs, ts, _ = x_sample.shape
    depth = w_ada.shape[0]
    n_pages = page_table.shape[1]
    past_len = n_pages * PAGE_SIZE

    tab_p = _rope_table(jnp.arange(tp, dtype=F32))
    tm_s = min(TM_SAMPLE, bs * ts)
    tab_s = jnp.tile(_rope_table(past_len + jnp.arange(ts, dtype=F32)), (tm_s // ts, 1))

    hp = x_prompt.reshape(bp * tp, d)
    hs = x_sample.reshape(bs * ts, d)
    outs = [[] for _ in range(6)]
    for l in range(depth):
        m = _ada(jnp.concatenate([c_prompt, c_sample], axis=0), w_ada[l], b_ada[l])
        mod_p = _Mod(m[:bp], tp, TM_FFN)
        mod_s = _Mod(m[bp:], ts, tm_s)

        f1 = (norm_ffn1[l], ffn1_w1[l].astype(BF16), ffn1_w3[l].astype(BF16), ffn1_w2[l].astype(BF16))
        f2 = (norm_ffn2[l], ffn2_w1[l].astype(BF16), ffn2_w3[l].astype(BF16), ffn2_w2[l].astype(BF16))
        w_in_r = _prep_w_in(w_in[l])
        wq = _prep_wq(w_qb[l])
        wkv_p = _prep_wkv_prompt(w_kvb[l])
        wq_s = _prep_wq_sample(w_kvb[l])
        w_lat = _prep_w_lat_out(w_kvb[l])
        w_o_mla, w_o_gla = _prep_w_o(w_o[l])
        wgate = _prep_gate(w_gate_b[l])
        last = l == depth - 1

        hp = _ffn(hp, mod_p, 0, *f1, TM_FFN)
        q, k, v, ckv_p, kr_p, gq, gk, gv, la, gg = _mix_in(
            hp, mod_p, norm_mix[l], w_in_r, g_qa[l], wq, g_kva[l], wkv_p, wgate, b_gate[l], tab_p,
            False, TM_TOKENS, bp)
        o_mla = _attn_prompt(q, k, v, TQ_PROMPT, HEAD_PAIRS_PER_STEP).reshape(bp * tp, -1)
        seq3 = lambda a: a.reshape(bp, tp, -1)
        s0_p = jnp.zeros((bp, GLA_HEADS, GLA_DK, GLA_DV), F32)
        o_gla, s_p = _gla(seq3(gq), seq3(gk), seq3(gv), seq3(la), s0_p, GLA_CHUNK, 1)
        hp = _mix_out_ffn(hp, mod_p, o_mla, o_gla.reshape(bp * tp, -1), gg, g_gla_o[l], w_lat, w_o_mla,
                          w_o_gla, *f2, norm_final, False, last, TM_TOKENS)

        hs = _ffn(hs, mod_s, 0, *f1, tm_s)
        ql, ckv_s, kr_s, gq, gk, gv, la, gg = _mix_in(
            hs, mod_s, norm_mix[l], w_in_r, g_qa[l], wq, g_kva[l], wq_s, wgate, b_gate[l], tab_s,
            True, tm_s, bs)
        n_lat = MLA_HEADS * MLA_KV_LORA
        rows = ts * MLA_HEADS
        q_lat = ql[:, :n_lat].reshape(bs, rows, MLA_KV_LORA)
        q_rope = ql[:, n_lat:].reshape(bs, rows, MLA_ROPE)
        o_lat = _attn_sample(page_table, q_lat, q_rope, ckv_s.reshape(bs, ts, -1), kr_s.reshape(bs, ts, -1),
                             cache_ckv, jnp.swapaxes(cache_krope, 2, 3), l, PAGES_PER_CHUNK)
        pad = lambda a: jnp.pad(a.reshape(bs, ts, -1), ((0, 0), (0, GLA_SAMPLE_PAD - ts), (0, 0)))
        o_gla, s_s = _gla(pad(gq), pad(gk), pad(gv), pad(la), state_gla[l], GLA_SAMPLE_PAD, GLA_SAMPLE_SEQS)
        hs = _mix_out_ffn(hs, mod_s, o_lat.reshape(bs * ts, n_lat), o_gla[:, :ts].reshape(bs * ts, -1), gg,
                          g_gla_o[l], w_lat, w_o_mla, w_o_gla, *f2, norm_final, True, last, tm_s)

        for lst, a in zip(outs, (ckv_p.reshape(bp, tp, -1), kr_p.reshape(bp, tp, -1), s_p,
                                 ckv_s.reshape(bs, ts, -1), kr_s.reshape(bs, ts, -1), s_s)):
            lst.append(a)

    return (hp.reshape(bp, tp, d), hs.reshape(bs, ts, d)) + tuple(jnp.stack(o) for o in outs)
```

```python
import functools
import math

import jax
import jax.numpy as jnp
from jax import lax
from jax.experimental import pallas as pl
from jax.experimental.pallas import tpu as pltpu

F32 = jnp.float32
BF16 = jnp.bfloat16

PAGE_SIZE = 128
MLA_HEADS = 8
MLA_NOPE = 64
MLA_ROPE = 32
MLA_V = 64
MLA_Q_LORA = 384
MLA_KV_LORA = 256
MLA_SCALE = (MLA_NOPE + MLA_ROPE) ** -0.5
LOG2_E = math.log2(math.e)
ROPE_THETA = 10000.0
GLA_HEADS = 4
GLA_DK = 64
GLA_DV = 128
GLA_GATE_RANK = 16
GLA_GATE_TAU = 16.0
GLA_CHUNK = 64
N_ADA = 9
EPS = 1e-6

LANE = 128
HEAD_PAD = 128
VMEM_LIMIT_BYTES = 56 * 1024 * 1024

C_QA = 0
C_KVA = C_QA + MLA_Q_LORA
C_GQ = C_KVA + MLA_KV_LORA
C_GK = C_GQ + GLA_HEADS * GLA_DK
C_GV = C_GK + GLA_HEADS * GLA_DK
C_GG = C_GV + GLA_HEADS * GLA_DV
C_BLKA = C_GG + GLA_HEADS * GLA_DV
C_END = C_BLKA + LANE
N_ROPE_TABS = 6


def _cparams(sem):
    return pltpu.CompilerParams(dimension_semantics=sem, vmem_limit_bytes=VMEM_LIMIT_BYTES)


def _resident(shape):
    return pl.BlockSpec(shape, lambda *_: (0,) * len(shape), pipeline_mode=pl.Buffered(1))


def _rms(x, g):
    return x * lax.rsqrt(jnp.mean(x * x, axis=-1, keepdims=True) + EPS) * g


def _silu(x):
    return x * jax.nn.sigmoid(x)


def _dot(a, b):
    return jnp.dot(a, b, preferred_element_type=F32)


def _dot_nt(a, b):
    return lax.dot_general(a, b, (((1,), (1,)), ((), ())), preferred_element_type=F32)


def _dot_tn(a, b):
    return lax.dot_general(a, b, (((0,), (0,)), ((), ())), preferred_element_type=F32)


def _ada_body(c_ref, w_ref, b_ref, o_ref):
    c = _silu(c_ref[...]).astype(BF16)
    o_ref[...] = _dot(c, w_ref[...].astype(BF16)) + b_ref[...]


def _ada(c_all, w_ada, b_ada):
    n, d = c_all.shape
    n_out = w_ada.shape[1]
    tn = 1024
    return pl.pallas_call(
        _ada_body,
        grid=(n_out // tn,),
        in_specs=[pl.BlockSpec((n, d), lambda j: (0, 0)),
                  pl.BlockSpec((d, tn), lambda j: (0, j)),
                  pl.BlockSpec((1, tn), lambda j: (0, j))],
        out_specs=pl.BlockSpec((n, tn), lambda j: (0, j)),
        out_shape=jax.ShapeDtypeStruct((n, n_out), F32),
        compiler_params=_cparams(("arbitrary",)),
        name="ada",
    )(c_all, w_ada, b_ada.reshape(1, n_out))


class _Mod:
    def __init__(self, m, tokens_per_seq, tm):
        self.tokens_per_seq = tokens_per_seq
        self.rep = tokens_per_seq if tm > tokens_per_seq else 1
        self.arr = m if self.rep > 1 else m.reshape(m.shape[0], 1, m.shape[1])

    def spec(self, tm, d, blk, width=3):
        if self.rep > 1:
            return pl.BlockSpec((tm // self.rep, width * d), lambda i: (i, blk))
        per = self.tokens_per_seq // tm
        return pl.BlockSpec((None, 1, width * d), lambda i: (i // per, 0, blk))


def _mod_cols(mod_ref, lo, hi, rep):
    v = mod_ref[:, lo:hi]
    if rep > 1:
        g, w = v.shape
        v = jnp.broadcast_to(v[:, None, :], (g, rep, w)).reshape(g * rep, w)
    return v


FF_STEP = 1024


def _ff_chunks(ff):
    return tuple((c, min(c + FF_STEP, ff)) for c in range(0, ff, FF_STEP))


def _ffn_half_step(x, mod_ref, g_ref, w1_ref, w3_ref, w2_ref, d, rep):
    shift, scale, gate = (_mod_cols(mod_ref, k * d, (k + 1) * d, rep) for k in range(3))
    h = (_rms(x, g_ref[...]) * (1.0 + scale) + shift).astype(BF16)
    y = None
    for c0, c1 in _ff_chunks(w1_ref.shape[1]):
        a = _dot(h, w1_ref[:, c0:c1])
        b = _dot(h, w3_ref[:, c0:c1])
        u = (_silu(a) * b).astype(BF16)
        part = _dot(u, w2_ref[c0:c1, :])
        y = part if y is None else y + part
    return x + 0.5 * gate * y


def _ffn_body(x_ref, mod_ref, g_ref, w1_ref, w3_ref, w2_ref, o_ref, *, d, rep):
    o_ref[...] = _ffn_half_step(x_ref[...], mod_ref, g_ref, w1_ref, w3_ref, w2_ref, d, rep)


def _ffn(x, mod, third, g, w1, w3, w2, tm):
    n, d = x.shape
    ff = w1.shape[1]
    return pl.pallas_call(
        functools.partial(_ffn_body, d=d, rep=mod.rep),
        grid=(n // tm,),
        in_specs=[pl.BlockSpec((tm, d), lambda i: (i, 0)),
                  mod.spec(tm, d, third),
                  _resident((1, d)), _resident((d, ff)), _resident((d, ff)), _resident((ff, d))],
        out_specs=pl.BlockSpec((tm, d), lambda i: (i, 0)),
        out_shape=jax.ShapeDtypeStruct((n, d), F32),
        compiler_params=_cparams(("parallel",)),
        name="ffn",
    )(x, mod.arr, g.reshape(1, d), w1, w3, w2)


def _mix_in_body(x_ref, mod_ref, g_ref, win_ref, gqa_ref, wq_ref, gkva_ref, wkv_ref, wgate_ref,
                 bgate_ref, tab_ref, *out_refs, d, sample, rep):
    if sample:
        (ql_ref, ckv_ref, kr_ref, gq_ref, gk_ref, gv_ref, la_ref, gg_ref) = out_refs
    else:
        (q_ref, k_ref, v_ref, ckv_ref, kr_ref, gq_ref, gk_ref, gv_ref, la_ref, gg_ref) = out_refs
    x = x_ref[...]
    shift, scale = _mod_cols(mod_ref, 0, d, rep), _mod_cols(mod_ref, d, 2 * d, rep)
    h = (_rms(x, g_ref[...]) * (1.0 + scale) + shift).astype(BF16)
    proj = _dot(h, win_ref[...])

    q_tabs = tuple(tab_ref[:, i * LANE:(i + 1) * LANE] for i in range(3))
    k_tabs = tuple(tab_ref[:, i * LANE:(i + 1) * LANE] for i in range(3, 6))
    hr = MLA_ROPE // 2

    def rotate(x, tabs):
        c, s_from_below, s_from_above = tabs
        return x * c + pltpu.roll(x, hr, 1) * s_from_below + pltpu.roll(x, LANE - hr, 1) * s_from_above

    qn = _rms(proj[:, C_QA:C_KVA], gqa_ref[...]).astype(BF16)
    qq = _dot(qn, wq_ref[...])
    half = MLA_HEADS * HEAD_PAD
    q_heads = [rotate(qq[:, hh * HEAD_PAD:(hh + 1) * HEAD_PAD], q_tabs) for hh in range(MLA_HEADS)]

    ckv = _rms(proj[:, C_KVA:C_GQ], gkva_ref[...])
    ckv_ref[...] = ckv
    blk_a = proj[:, C_BLKA:C_END]
    kr_blk = rotate(blk_a, k_tabs)
    kr_ref[...] = kr_blk[:, 0:MLA_ROPE]

    if sample:
        q_all = jnp.concatenate(q_heads, axis=1).astype(BF16)
        ql_ref[...] = _dot(q_all, wkv_ref[...]).astype(BF16)
    else:
        for hh in range(MLA_HEADS):
            q_ref[hh] = q_heads[hh].astype(BF16)
        kv = _dot(ckv.astype(BF16), wkv_ref[...])
        kr_in_head = pltpu.roll(kr_blk, MLA_NOPE, 1)
        for hh in range(MLA_HEADS):
            k_ref[hh] = (kv[:, hh * HEAD_PAD:(hh + 1) * HEAD_PAD] + kr_in_head).astype(BF16)
        for pr in range(MLA_HEADS * MLA_V // LANE):
            v_ref[pr] = kv[:, half + pr * LANE:half + (pr + 1) * LANE].astype(BF16)

    gq_ref[...] = proj[:, C_GQ:C_GK]
    gk_ref[...] = proj[:, C_GK:C_GV]
    gv_ref[...] = proj[:, C_GV:C_GG]
    gg_ref[...] = proj[:, C_GG:C_BLKA]
    z = _dot(blk_a.astype(BF16), wgate_ref[...]) + bgate_ref[...]
    la_ref[...] = (jnp.minimum(z, 0.0) - jnp.log1p(jnp.exp(-jnp.abs(z)))) / GLA_GATE_TAU


def _mix_in(x, mod, g, w_in_r, g_qa, wq, g_kva, wkv, wgate, b_gate, tab, sample, tm, n_seq):
    n, d = x.shape
    t_seq = n // n_seq
    n_pos_tiles = tab.shape[0] // tm
    hk = GLA_HEADS * GLA_DK
    hv = GLA_HEADS * GLA_DV
    tok = lambda w: pl.BlockSpec((tm, w), lambda i: (i, 0))
    tok_shape = lambda w, dt=F32: jax.ShapeDtypeStruct((n, w), dt)
    gla_specs = [tok(hk), tok(hk), tok(hv), tok(hk), tok(hv)]
    gla_shapes = [tok_shape(hk), tok_shape(hk), tok_shape(hv), tok_shape(hk), tok_shape(hv)]
    if sample:
        wl = wkv.shape[1]
        out_specs = [tok(wl), tok(MLA_KV_LORA), tok(MLA_ROPE)] + gla_specs
        out_shape = [tok_shape(wl, BF16), tok_shape(MLA_KV_LORA), tok_shape(MLA_ROPE)] + gla_shapes
    else:
        per = t_seq // tm
        n_vp = MLA_HEADS * MLA_V // LANE
        head_spec = lambda nh: pl.BlockSpec((None, nh, tm, HEAD_PAD), lambda i: (i // per, 0, i % per, 0))
        head_shape = lambda nh: jax.ShapeDtypeStruct((n_seq, nh, t_seq, HEAD_PAD), BF16)
        out_specs = ([head_spec(MLA_HEADS)] * 2 + [head_spec(n_vp), tok(MLA_KV_LORA), tok(MLA_ROPE)]
                     + gla_specs)
        out_shape = ([head_shape(MLA_HEADS)] * 2 + [head_shape(n_vp), tok_shape(MLA_KV_LORA),
                                                    tok_shape(MLA_ROPE)] + gla_shapes)
    body = functools.partial(_mix_in_body, d=d, sample=sample, rep=mod.rep)
    return pl.pallas_call(
        body,
        grid=(n // tm,),
        in_specs=[pl.BlockSpec((tm, d), lambda i: (i, 0)),
                  mod.spec(tm, d, 1),
                  _resident((1, d)), _resident(w_in_r.shape), _resident((1, MLA_Q_LORA)),
                  _resident(wq.shape), _resident((1, MLA_KV_LORA)), _resident(wkv.shape),
                  _resident(wgate.shape), _resident((1, hk)),
                  pl.BlockSpec((tm, N_ROPE_TABS * LANE), lambda i: (i % n_pos_tiles, 0))],
        out_specs=out_specs,
        out_shape=out_shape,
        compiler_params=_cparams(("parallel",)),
        name="mix_in_sample" if sample else "mix_in_prompt",
    )(x, mod.arr, g.reshape(1, d), w_in_r, g_qa.reshape(1, -1), wq, g_kva.reshape(1, -1), wkv,
      wgate, b_gate.reshape(1, hk), tab)


def _attn_prompt_body(q_ref, k_ref, v_ref, o_ref, *, tq, n_pairs):
    n_heads = 2 * n_pairs
    i = pl.program_id(2)
    qs = [q_ref[hh] for hh in range(n_heads)]

    def scores(hh, j):
        off = pl.multiple_of(j * tq, tq)
        return _dot_nt(qs[hh], k_ref[hh, pl.ds(off, tq), :]), v_ref[hh // 2, pl.ds(off, tq), :]

    init = []
    for hh in range(n_heads):
        s, v = scores(hh, i)
        row = lax.broadcasted_iota(jnp.int32, s.shape, 0)
        col = lax.broadcasted_iota(jnp.int32, s.shape, 1)
        s = jnp.where(col <= row, s, jnp.finfo(F32).min)
        m = jnp.max(s, axis=-1, keepdims=True)
        p = jnp.exp2(s - m)
        init += [m, jnp.sum(p, axis=-1, keepdims=True), _dot(p.astype(BF16), v)]

    def step(j, carry):
        out = []
        for hh in range(n_heads):
            m, l, acc = carry[3 * hh:3 * hh + 3]
            s, v = scores(hh, j)
            m_new = jnp.maximum(m, jnp.max(s, axis=-1, keepdims=True))
            alpha = jnp.exp2(m - m_new)
            p = jnp.exp2(s - m_new)
            out += [m_new, alpha * l + jnp.sum(p, axis=-1, keepdims=True),
                    alpha * acc + _dot(p.astype(BF16), v)]
        return tuple(out)

    fin = lax.fori_loop(0, i, step, tuple(init))
    for pr in range(n_pairs):
        e, o = 6 * pr, 6 * pr + 3
        o_even, o_odd = fin[e + 2] / fin[e + 1], fin[o + 2] / fin[o + 1]
        lane = lax.broadcasted_iota(jnp.int32, o_even.shape, 1)
        o_ref[:, pr * LANE:(pr + 1) * LANE] = jnp.where(lane < MLA_V, o_even, o_odd).astype(o_ref.dtype)


def _attn_prompt(q, k, v, tq, n_pairs):
    b, hh, t, w = q.shape
    nh = 2 * n_pairs
    body = functools.partial(_attn_prompt_body, tq=tq, n_pairs=n_pairs)
    return pl.pallas_call(
        body,
        grid=(b, hh // nh, t // tq),
        in_specs=[pl.BlockSpec((None, nh, tq, w), lambda bi, hi, i: (bi, hi, i, 0)),
                  pl.BlockSpec((None, nh, t, w), lambda bi, hi, i: (bi, hi, 0, 0)),
                  pl.BlockSpec((None, n_pairs, t, w), lambda bi, hi, i: (bi, hi, 0, 0))],
        out_specs=pl.BlockSpec((None, tq, n_pairs * w), lambda bi, hi, i: (bi, i, hi)),
        out_shape=jax.ShapeDtypeStruct((b, t, hh // 2 * w), BF16),
        compiler_params=_cparams(("parallel", "parallel", "arbitrary")),
        name="attn_prompt",
    )(q, k, v)


def _attn_sample_body(pt_ref, ql_ref, qr_ref, cn_ref, kn_ref, ckv_hbm, kr_hbm, o_ref, ckv_buf, kr_buf,
                      ckb_buf, krb_buf, sems, *, layer, n_chunks, pg):
    b = pl.program_id(0)
    n_seq = pl.num_programs(0)

    def copies(seq, c):
        out = []
        for p in range(pg):
            page = pt_ref[seq, c * pg + p]
            rows = pl.ds(p * PAGE_SIZE, PAGE_SIZE)
            out.append(pltpu.make_async_copy(ckv_hbm.at[layer, page], ckv_buf.at[c, rows, :], sems.at[c, 0]))
            out.append(pltpu.make_async_copy(kr_hbm.at[layer, page], kr_buf.at[c, :, rows], sems.at[c, 1]))
        return out

    @pl.when(b == 0)
    def _():
        for c in range(n_chunks):
            for cp in copies(0, c):
                cp.start()

    ql = ql_ref[...]
    qr = qr_ref[...]

    cn = cn_ref[...]
    s = _dot_nt(ql.astype(F32), cn) + _dot_nt(qr.astype(F32), kn_ref[...])
    row_tok = lax.broadcasted_iota(jnp.int32, s.shape, 0) // MLA_HEADS
    col = lax.broadcasted_iota(jnp.int32, s.shape, 1)
    s = jnp.where(col <= row_tok, s, jnp.finfo(F32).min)
    m = jnp.max(s, axis=-1, keepdims=True)
    p = jnp.exp2(s - m)
    l = jnp.sum(p, axis=-1, keepdims=True)
    acc = _dot(p, cn)

    nxt = jnp.minimum(b + 1, n_seq - 1)

    def slot_filled(c):
        return (pltpu.make_async_copy(ckv_buf.at[c], ckv_buf.at[c], sems.at[c, 0]),
                pltpu.make_async_copy(kr_buf.at[c], kr_buf.at[c], sems.at[c, 1]))

    def arrive(c):
        for cp in slot_filled(c):
            cp.wait()

    def stage(c):
        ckb_buf[c] = ckv_buf[c].astype(BF16)
        krb_buf[c] = kr_buf[c].astype(BF16)
        for k, cp in enumerate(copies(nxt, c)):
            cp.start(priority=(k // 2) % 2)

    arrive(0)
    stage(0)
    pending = None
    for c in range(n_chunks):
        s = _dot_nt(ql, ckb_buf[c]) + _dot(qr, krb_buf[c])
        if c + 1 < n_chunks:
            arrive(c + 1)
            stage(c + 1)
        if pending is not None:
            p_prev, alpha_prev = pending
            acc = alpha_prev * acc + _dot(p_prev, ckb_buf[c - 1])
        m_new = jnp.maximum(m, jnp.max(s, axis=-1, keepdims=True))
        alpha = jnp.exp2(m - m_new)
        p = jnp.exp2(s - m_new)
        l = alpha * l + jnp.sum(p, axis=-1, keepdims=True)
        pending = (p.astype(BF16), alpha)
        m = m_new
    p_prev, alpha_prev = pending
    acc = alpha_prev * acc + _dot(p_prev, ckb_buf[n_chunks - 1])
    o_ref[...] = acc / l

    @pl.when(b == n_seq - 1)
    def _():
        for c in range(n_chunks):
            for cp in slot_filled(c):
                cp.wait()


def _attn_sample(page_table, q_lat, q_rope, ckv_new, kr_new, cache_ckv, cache_krope_t, layer, pg):
    n_seq, rows, c = q_lat.shape
    t_new = ckv_new.shape[1]
    r = q_rope.shape[2]
    n_chunks = page_table.shape[1] // pg
    seq = lambda shape: pl.BlockSpec((None,) + shape, lambda b, pt: (b, 0, 0))
    body = functools.partial(_attn_sample_body, layer=layer, n_chunks=n_chunks, pg=pg)
    grid_spec = pltpu.PrefetchScalarGridSpec(
        num_scalar_prefetch=1,
        grid=(n_seq,),
        in_specs=[seq((rows, c)), seq((rows, r)), seq((t_new, c)), seq((t_new, r)),
                  pl.BlockSpec(memory_space=pl.ANY), pl.BlockSpec(memory_space=pl.ANY)],
        out_specs=seq((rows, c)),
        scratch_shapes=[pltpu.VMEM((n_chunks, pg * PAGE_SIZE, c), F32),
                        pltpu.VMEM((n_chunks, r, pg * PAGE_SIZE), F32),
                        pltpu.VMEM((n_chunks, pg * PAGE_SIZE, c), BF16),
                        pltpu.VMEM((n_chunks, r, pg * PAGE_SIZE), BF16),
                        pltpu.SemaphoreType.DMA((n_chunks, 2))],
    )
    return pl.pallas_call(
        body,
        grid_spec=grid_spec,
        out_shape=jax.ShapeDtypeStruct((n_seq, rows, c), F32),
        compiler_params=_cparams(("arbitrary",)),
        name="attn_sample",
    )(page_table, q_lat, q_rope, ckv_new, kr_new, cache_ckv, cache_krope_t)


def _gla_body(q_ref, k_ref, v_ref, la_ref, s0_ref, o_ref, s_ref, st_sc, *, n_seq, n_chunks, chunk):
    hk = GLA_HEADS * GLA_DK
    for sq in range(n_seq):
        st_sc[sq] = s0_ref[sq].reshape(hk, GLA_DV).T
    rows = GLA_HEADS * chunk
    r = lax.broadcasted_iota(jnp.int32, (chunk, chunk), 0)
    c = lax.broadcasted_iota(jnp.int32, (chunk, chunk), 1)
    tri = (c <= r).astype(BF16)
    rs = lax.broadcasted_iota(jnp.int32, (rows, chunk), 0)
    cs = lax.broadcasted_iota(jnp.int32, (rows, chunk), 1)
    causal_stacked = cs <= rs % chunk
    lane_head = lax.broadcasted_iota(jnp.int32, (chunk, hk), 1) // GLA_DK
    zero = jnp.zeros((chunk, hk), BF16)

    def stack_heads(x):
        return jnp.concatenate([jnp.where(lane_head == hh, x, zero) for hh in range(GLA_HEADS)], axis=0)

    def one_chunk(ci, _):
        off = pl.multiple_of(ci * chunk, chunk)
        for sq in range(n_seq):
            g = la_ref[sq, pl.ds(off, chunk), :]
            g_hi = g.astype(BF16)
            g_lo = (g - g_hi.astype(F32)).astype(BF16)
            b = _dot(tri, g_hi) + _dot(tri, g_lo)
            b_last = b[chunk - 1:chunk, :]
            q = q_ref[sq, pl.ds(off, chunk), :]
            k = k_ref[sq, pl.ds(off, chunk), :]
            v = v_ref[sq, pl.ds(off, chunk), :].astype(BF16)
            qe = stack_heads((q * jnp.exp(b)).astype(BF16))
            ke = (k * jnp.exp(-b)).astype(BF16)
            kd = stack_heads((k * jnp.exp(b_last - b)).astype(BF16))
            st = st_sc[sq]
            a = jnp.where(causal_stacked, _dot_nt(qe, ke), 0.0).astype(BF16)
            intra = _dot(a, v)
            inter = _dot_nt(qe, st.astype(BF16))
            o_ref[sq, pl.ds(off, chunk), :] = jnp.concatenate(
                [intra[hh * chunk:(hh + 1) * chunk, hh * GLA_DV:(hh + 1) * GLA_DV]
                 + inter[hh * chunk:(hh + 1) * chunk] for hh in range(GLA_HEADS)], axis=1)
            v_stacked = jnp.concatenate([v[:, hh * GLA_DV:(hh + 1) * GLA_DV] for hh in range(GLA_HEADS)], axis=0)
            st_sc[sq] = st * jnp.exp(b_last) + _dot_tn(v_stacked, kd)
        return 0

    lax.fori_loop(0, n_chunks, one_chunk, 0, unroll=min(GLA_UNROLL, n_chunks))
    for sq in range(n_seq):
        s_ref[sq] = st_sc[sq].T.reshape(GLA_HEADS, GLA_DK, GLA_DV)


def _gla(gq, gk, gv, la, s0, chunk, seqs_per_step):
    b, t, hk = gq.shape
    hv = gv.shape[2]
    g = seqs_per_step
    blk = lambda w: pl.BlockSpec((g, t, w), lambda i: (i, 0, 0))
    st = pl.BlockSpec((g, GLA_HEADS, GLA_DK, GLA_DV), lambda i: (i, 0, 0, 0))
    body = functools.partial(_gla_body, n_seq=g, n_chunks=t // chunk, chunk=chunk)
    return pl.pallas_call(
        body,
        grid=(b // g,),
        in_specs=[blk(hk), blk(hk), blk(hv), blk(hk), st],
        out_specs=[blk(hv), st],
        out_shape=[jax.ShapeDtypeStruct((b, t, hv), F32),
                   jax.ShapeDtypeStruct((b, GLA_HEADS, GLA_DK, GLA_DV), F32)],
        scratch_shapes=[pltpu.VMEM((g, GLA_DV, GLA_HEADS * GLA_DK), F32)],
        compiler_params=_cparams(("parallel",)),
        name="gla",
    )(gq, gk, gv, la, s0)


def _mix_out_ffn_body(x_ref, mod2_ref, mod3_ref, om_ref, og_ref, gg_ref, go_ref, wlat_ref, wom_ref, wog_ref,
                      g_ref, w1_ref, w3_ref, w2_ref, gf_ref, o_ref, *, d, sample, final, rep):
    gate = _mod_cols(mod2_ref, 0, d, rep)
    om = om_ref[...]
    if sample:
        om = _dot(om.astype(BF16), wlat_ref[...]).astype(BF16)
    parts = []
    for hh in range(GLA_HEADS):
        vs = slice(hh * GLA_DV, (hh + 1) * GLA_DV)
        parts.append(_rms(og_ref[:, vs], go_ref[...]) * _silu(gg_ref[:, vs]))
    og = jnp.concatenate(parts, axis=1).astype(BF16)
    mix = _dot(om, wom_ref[...]) + _dot(og, wog_ref[...])
    x = x_ref[...] + gate * mix
    out = _ffn_half_step(x, mod3_ref, g_ref, w1_ref, w3_ref, w2_ref, d, rep)
    if final:
        out = _rms(out, gf_ref[...])
    o_ref[...] = out


def _mix_out_ffn(x, mod, o_mla, o_gla, gg, g_gla_o, w_lat, w_o_mla, w_o_gla, g, w1, w3, w2, g_final,
                 sample, final, tm):
    n, d = x.shape
    ff = w1.shape[1]
    hv = GLA_HEADS * GLA_DV
    tok = lambda w: pl.BlockSpec((tm, w), lambda i: (i, 0))
    body = functools.partial(_mix_out_ffn_body, d=d, sample=sample, final=final, rep=mod.rep)
    return pl.pallas_call(
        body,
        grid=(n // tm,),
        in_specs=[tok(d), mod.spec(tm, d, 5, 1), mod.spec(tm, d, 2), tok(o_mla.shape[1]), tok(hv), tok(hv),
                  _resident((1, GLA_DV)), _resident(w_lat.shape), _resident(w_o_mla.shape),
                  _resident(w_o_gla.shape),
                  _resident((1, d)), _resident((d, ff)), _resident((d, ff)), _resident((ff, d)),
                  _resident((1, d))],
        out_specs=tok(d),
        out_shape=jax.ShapeDtypeStruct((n, d), F32),
        compiler_params=_cparams(("parallel",)),
        name="mix_out_ffn_sample" if sample else "mix_out_ffn_prompt",
    )(x, mod.arr, mod.arr, o_mla, o_gla, gg, g_gla_o.reshape(1, GLA_DV), w_lat, w_o_mla, w_o_gla,
      g.reshape(1, d), w1, w3, w2, g_final.reshape(1, d))


def _prep_w_in(w_in):
    d = w_in.shape[0]
    o = 0
    parts = {}
    for name, size in (("qa", MLA_Q_LORA), ("kva", MLA_KV_LORA), ("kr", MLA_ROPE),
                       ("gq", GLA_HEADS * GLA_DK), ("gk", GLA_HEADS * GLA_DK), ("gv", GLA_HEADS * GLA_DV),
                       ("ga", GLA_GATE_RANK), ("gg", GLA_HEADS * GLA_DV)):
        parts[name] = w_in[:, o:o + size]
        o += size
    blk_a = jnp.pad(jnp.concatenate([parts["kr"], parts["ga"]], axis=1),
                    ((0, 0), (0, LANE - MLA_ROPE - GLA_GATE_RANK)))
    w = jnp.concatenate([parts["qa"], parts["kva"], parts["gq"] * (GLA_DK ** -0.5), parts["gk"],
                         parts["gv"], parts["gg"], blk_a], axis=1)
    return w.astype(BF16)


def _prep_wq(w_qb):
    ql = w_qb.shape[0]
    w = jnp.pad(w_qb, ((0, 0), (0, 0), (0, HEAD_PAD - MLA_NOPE - MLA_ROPE)))
    return (w.reshape(ql, MLA_HEADS * HEAD_PAD) * (MLA_SCALE * LOG2_E)).astype(BF16)


def _prep_wkv_prompt(w_kvb):
    rows = w_kvb.shape[0]
    k = jnp.pad(w_kvb[..., :MLA_NOPE], ((0, 0), (0, 0), (0, HEAD_PAD - MLA_NOPE)))
    v = w_kvb[..., MLA_NOPE:]
    return jnp.concatenate([k.reshape(rows, -1), v.reshape(rows, -1)], axis=1).astype(BF16)


def _prep_wq_sample(w_kvb):
    same_head = jnp.eye(MLA_HEADS, dtype=F32)[:, None, :, None]
    pad_n = ((0, 0), (0, HEAD_PAD - MLA_NOPE), (0, 0), (0, 0))
    lat = jnp.pad(jnp.transpose(w_kvb[..., :MLA_NOPE], (1, 2, 0))[:, :, None, :] * same_head, pad_n)
    sel = jnp.eye(HEAD_PAD, MLA_ROPE, k=-MLA_NOPE, dtype=F32)
    rope = sel[None, :, None, :] * same_head
    rows = MLA_HEADS * HEAD_PAD
    return jnp.concatenate([lat.reshape(rows, -1), rope.reshape(rows, -1)], axis=1).astype(BF16)


def _prep_w_lat_out(w_kvb):
    c = w_kvb.shape[0]
    same_head = jnp.eye(MLA_HEADS, dtype=F32)[:, None, :, None]
    w = jnp.transpose(w_kvb[..., MLA_NOPE:], (1, 0, 2))[:, :, None, :] * same_head
    return w.reshape(MLA_HEADS * c, MLA_HEADS * MLA_V).astype(BF16)


def _prep_w_o(w_o):
    n_mla = MLA_HEADS * MLA_V
    return w_o[:n_mla].astype(BF16), w_o[n_mla:].astype(BF16)


def _prep_gate(w_gate_b):
    return jnp.pad(w_gate_b, ((MLA_ROPE, LANE - MLA_ROPE - GLA_GATE_RANK), (0, 0))).astype(BF16)


def _rope_table(pos):
    hr = MLA_ROPE // 2
    inv = ROPE_THETA ** (-jnp.arange(0, MLA_ROPE, 2, dtype=F32) / MLA_ROPE)
    ang = pos[:, None] * inv[None, :]
    cos, sin = jnp.cos(ang), jnp.sin(ang)
    t = pos.shape[0]
    z = lambda w: jnp.zeros((t, w), F32)

    def tabs(start):
        after = LANE - start - 2 * hr
        c = jnp.concatenate([jnp.ones((t, start), F32), cos, cos, z(after)], axis=1)
        s_from_below = jnp.concatenate([z(start + hr), sin, z(after)], axis=1)
        s_from_above = jnp.concatenate([z(start), -sin, z(hr + after)], axis=1)
        return [c, s_from_below, s_from_above]

    return jnp.concatenate(tabs(MLA_NOPE) + tabs(0), axis=1)


TM_TOKENS = 512
TM_FFN = 1024
TM_SAMPLE = 512
TQ_PROMPT = 512
HEAD_PAIRS_PER_STEP = 4
PAGES_PER_CHUNK = 64
GLA_SAMPLE_PAD = 16
GLA_SAMPLE_SEQS = 8
GLA_UNROLL = 8


def kernel(x_prompt, x_sample, cache_ckv, cache_krope, state_gla, page_table, c_prompt, c_sample,
           w_ada, b_ada, norm_ffn1, ffn1_w1, ffn1_w3, ffn1_w2, norm_mix, w_in, g_qa, w_qb, g_kva, w_kvb,
           w_gate_b, b_gate, g_gla_o, w_o, norm_ffn2, ffn2_w1, ffn2_w3, ffn2_w2, norm_final):
    bp, tp, d = x_prompt.shape
    bs, ts, _ = x_sample.shape
    depth = w_ada.shape[0]
    n_pages = page_table.shape[1]
    past_len = n_pages * PAGE_SIZE

    tab_p = _rope_table(jnp.arange(tp, dtype=F32))
    tm_s = min(TM_SAMPLE, bs * ts)
    tab_s = jnp.tile(_rope_table(past_len + jnp.arange(ts, dtype=F32)), (tm_s // ts, 1))

    hp = x_prompt.reshape(bp * tp, d)
    hs = x_sample.reshape(bs * ts, d)
    outs = [[] for _ in range(6)]
    for l in range(depth):
        m = _ada(jnp.concatenate([c_prompt, c_sample], axis=0), w_ada[l], b_ada[l])
        mod_p = _Mod(m[:bp], tp, TM_FFN)
        mod_s = _Mod(m[bp:], ts, tm_s)

        f1 = (norm_ffn1[l], ffn1_w1[l].astype(BF16), ffn1_w3[l].astype(BF16), ffn1_w2[l].astype(BF16))
        f2 = (norm_ffn2[l], ffn2_w1[l].astype(BF16), ffn2_w3[l].astype(BF16), ffn2_w2[l].astype(BF16))
        w_in_r = _prep_w_in(w_in[l])
        wq = _prep_wq(w_qb[l])
        wkv_p = _prep_wkv_prompt(w_kvb[l])
        wq_s = _prep_wq_sample(w_kvb[l])
        w_lat = _prep_w_lat_out(w_kvb[l])
        w_o_mla, w_o_gla = _prep_w_o(w_o[l])
        wgate = _prep_gate(w_gate_b[l])
        last = l == depth - 1

        hp = _ffn(hp, mod_p, 0, *f1, TM_FFN)
        q, k, v, ckv_p, kr_p, gq, gk, gv, la, gg = _mix_in(
            hp, mod_p, norm_mix[l], w_in_r, g_qa[l], wq, g_kva[l], wkv_p, wgate, b_gate[l], tab_p,
            False, TM_TOKENS, bp)
        o_mla = _attn_prompt(q, k, v, TQ_PROMPT, HEAD_PAIRS_PER_STEP).reshape(bp * tp, -1)
        seq3 = lambda a: a.reshape(bp, tp, -1)
        s0_p = jnp.zeros((bp, GLA_HEADS, GLA_DK, GLA_DV), F32)
        o_gla, s_p = _gla(seq3(gq), seq3(gk), seq3(gv), seq3(la), s0_p, GLA_CHUNK, 1)
        hp = _mix_out_ffn(hp, mod_p, o_mla, o_gla.reshape(bp * tp, -1), gg, g_gla_o[l], w_lat, w_o_mla,
                          w_o_gla, *f2, norm_final, False, last, TM_TOKENS)

        hs = _ffn(hs, mod_s, 0, *f1, tm_s)
        ql, ckv_s, kr_s, gq, gk, gv, la, gg = _mix_in(
            hs, mod_s, norm_mix[l], w_in_r, g_qa[l], wq, g_kva[l], wq_s, wgate, b_gate[l], tab_s,
            True, tm_s, bs)
        n_lat = MLA_HEADS * MLA_KV_LORA
        rows = ts * MLA_HEADS
        q_lat = ql[:, :n_lat].reshape(bs, rows, MLA_KV_LORA)
        q_rope = ql[:, n_lat:].reshape(bs, rows, MLA_ROPE)
        o_lat = _attn_sample(page_table, q_lat, q_rope, ckv_s.reshape(bs, ts, -1), kr_s.reshape(bs, ts, -1),
                             cache_ckv, jnp.swapaxes(cache_krope, 2, 3), l, PAGES_PER_CHUNK)
        pad = lambda a: jnp.pad(a.reshape(bs, ts, -1), ((0, 0), (0, GLA_SAMPLE_PAD - ts), (0, 0)))
        o_gla, s_s = _gla(pad(gq), pad(gk), pad(gv), pad(la), state_gla[l], GLA_SAMPLE_PAD, GLA_SAMPLE_SEQS)
        hs = _mix_out_ffn(hs, mod_s, o_lat.reshape(bs * ts, n_lat), o_gla[:, :ts].reshape(bs * ts, -1), gg,
                          g_gla_o[l], w_lat, w_o_mla, w_o_gla, *f2, norm_final, True, last, tm_s)

        for lst, a in zip(outs, (ckv_p.reshape(bp, tp, -1), kr_p.reshape(bp, tp, -1), s_p,
                                 ckv_s.reshape(bs, ts, -1), kr_s.reshape(bs, ts, -1), s_s)):
            lst.append(a)

    return (hp.reshape(bp, tp, d), hs.reshape(bs, ts, d)) + tuple(jnp.stack(o) for o in outs)
```
